```python
import math
import jax, jax.numpy as jnp
from jax import lax
import numpy as np

D_MODEL = 1024
BATCH = 8
SEQ = 2048
DEPTH = 4
DEC_BATCH = 128
DEC_SEQ = 8
PAST_LEN = 16384
PAGE_SIZE = 128

N_EVEN = (DEPTH + 1) // 2
N_ODD = DEPTH // 2
MIX_WIDTH = D_MODEL
A_WIDTH = MIX_WIDTH // 2
A_HEADS = 4
A_DK = A_WIDTH // A_HEADS
B_WIDTH = MIX_WIDTH - A_WIDTH
B_BLOCKS = 8
CONV_K = 4
LRU_C = 8.0
C_WIDTH = MIX_WIDTH // 2
C_HEADS = 4
C_DH = C_WIDTH // C_HEADS
D_WIDTH = MIX_WIDTH - C_WIDTH
D_HEADS = 4
D_DV = D_WIDTH // D_HEADS
D_DK = D_DV // 2
D_RANK = 16
D_TAU = 16.0
D_FF = -(-8 * D_MODEL // (3 * 256)) * 256
CHUNK = 64
EPS = 1e-6
EVEN_SPLIT = (A_WIDTH, A_WIDTH, A_WIDTH, A_WIDTH, B_WIDTH, B_WIDTH)
ODD_SPLIT = (C_WIDTH, C_WIDTH, C_WIDTH, C_WIDTH, C_HEADS, C_HEADS, D_HEADS * D_DK, D_HEADS * D_DK, D_WIDTH, D_WIDTH, D_RANK)
EVEN_IN = sum(EVEN_SPLIT)
ODD_IN = sum(ODD_SPLIT)

kernel_name = 'hybrid_hgrn2_rglru_mlstm_gla_step'


def split_cols(z, widths):
    offsets = np.cumsum(np.asarray(widths))[:-1].tolist()
    return jnp.split(z, offsets, axis=-1)


def rmsnorm(x, g):
    xf = x.astype(jnp.float32)
    y = xf * lax.rsqrt(jnp.mean(xf * xf, axis=-1, keepdims=True) + EPS)
    return (y * g.astype(jnp.float32)).astype(x.dtype)


def head_rmsnorm(o, g, n_heads):
    B, T, W = o.shape
    oh = o.reshape(B, T, n_heads, W // n_heads)
    oh = oh * lax.rsqrt(jnp.mean(oh * oh, axis=-1, keepdims=True) + EPS)
    return oh.reshape(B, T, W) * g.astype(jnp.float32)


def to_chunks(a, L):
    B, T, H = a.shape[:3]
    a = a.reshape((B, T // L, L, H) + a.shape[3:])
    return jnp.moveaxis(a, (1, 3), (0, 2))


def from_chunks(a):
    NC, B, H, L = a.shape[:4]
    a = jnp.moveaxis(a, (0, 2), (1, 3))
    return a.reshape((B, NC * L, H) + a.shape[4:])


def gated_linear_attention(q, k, v, log_f, s0):
    T = q.shape[1]
    L = math.gcd(T, CHUNK)
    tril = jnp.tril(jnp.ones((L, L), dtype=bool))

    def step(S, inp):
        qc, kc, vc, gc = inp
        b = jnp.cumsum(gc, axis=2)
        diff = jnp.where(tril[:, :, None], b[:, :, :, None, :] - b[:, :, None, :, :], -jnp.inf)
        scores = jnp.einsum('bhtd,bhtsd,bhsd->bhts', qc, jnp.exp(diff), kc)
        o = jnp.einsum('bhts,bhsv->bhtv', scores, vc) + jnp.einsum('bhtd,bhdv->bhtv', qc * jnp.exp(b), S)
        b_last = b[:, :, -1]
        S = jnp.exp(b_last)[..., None] * S + jnp.einsum('bhsd,bhsv->bhdv', kc * jnp.exp(b_last[:, :, None] - b), vc)
        return S, o

    xs = (to_chunks(q, L), to_chunks(k, L), to_chunks(v, L), to_chunks(log_f, L))
    S, o = lax.scan(step, s0.astype(jnp.float32), xs)
    return from_chunks(o), S


def mlstm_chunked(q, k, v, i_pre, log_f, C0, n0, m0):
    T = q.shape[1]
    L = math.gcd(T, CHUNK)
    tril = jnp.tril(jnp.ones((L, L), dtype=bool))

    def step(carry, inp):
        C, n, m = carry
        qc, kc, vc, ic, fc = inp
        b = jnp.cumsum(fc, axis=-1)
        dlog = jnp.where(tril, b[..., :, None] - b[..., None, :] + ic[..., None, :], -jnp.inf)
        inter = b + m[..., None]
        m_t = jnp.maximum(inter, jnp.max(dlog, axis=-1))
        w_intra = jnp.exp(dlog - m_t[..., None])
        w_inter = jnp.exp(inter - m_t)
        qk = jnp.einsum('bhtd,bhsd->bhts', qc, kc) * w_intra
        num = jnp.einsum('bhts,bhsv->bhtv', qk, vc) + w_inter[..., None] * jnp.einsum('bhtd,bhdv->bhtv', qc, C)
        den = jnp.sum(qk, axis=-1) + w_inter * jnp.einsum('bhtd,bhd->bht', qc, n)
        h = num / jnp.maximum(jnp.abs(den), jnp.exp(-m_t))[..., None]
        b_last = b[..., -1]
        wlog = b_last[..., None] - b + ic
        m_new = jnp.maximum(b_last + m, jnp.max(wlog, axis=-1))
        w_s = jnp.exp(wlog - m_new[..., None])
        decay = jnp.exp(b_last + m - m_new)
        C = decay[..., None, None] * C + jnp.einsum('bhs,bhsd,bhsv->bhdv', w_s, kc, vc)
        n = decay[..., None] * n + jnp.einsum('bhs,bhsd->bhd', w_s, kc)
        return (C, n, m_new), h

    f32 = jnp.float32
    carry0 = (C0.astype(f32), n0.astype(f32), m0.astype(f32))
    xs = (to_chunks(q, L), to_chunks(k, L), to_chunks(v, L), to_chunks(i_pre, L), to_chunks(log_f, L))
    (C, n, m), h = lax.scan(step, carry0, xs)
    return from_chunks(h), C, n, m


def causal_conv(x, buf, w, b):
    T = x.shape[1]
    xp = jnp.concatenate([buf.astype(x.dtype), x], axis=1)
    out = b + sum(xp[:, j:j + T] * w[j] for j in range(CONV_K))
    return out, xp[:, T:]


def rg_lru(x, pos0, h0, wa, ba, wx, bx, lru_a):
    B, T, W = x.shape
    xb = x.reshape(B, T, B_BLOCKS, W // B_BLOCKS)
    r = jax.nn.sigmoid(jnp.einsum('btnc,ncd->btnd', xb, wa).reshape(B, T, W) + ba)
    ig = jax.nn.sigmoid(jnp.einsum('btnc,ncd->btnd', xb, wx).reshape(B, T, W) + bx)
    log_a = -LRU_C * r * jax.nn.softplus(-lru_a)
    a = jnp.exp(log_a)
    mult = jnp.sqrt(-jnp.expm1(2.0 * log_a))
    pos = pos0 + jnp.arange(T)
    mult = jnp.where((pos == 0)[None, :, None], 1.0, mult)
    u = (mult * ig * x).at[:, 0].add(a[:, 0] * h0)

    def combine(c1, c2):
        a1, b1 = c1
        a2, b2 = c2
        return a1 * a2, a2 * b1 + b2

    _, h = lax.associative_scan(combine, (a, u), axis=1)
    return h


def even_mixer(h, pos0, s_hgrn, s_lru_h, s_lru_conv, lb, w_in, hgrn_norm, conv_w, conv_b, wa, ba, wx, bx, lru_a, w_out):
    B, T, _ = h.shape
    z = jnp.einsum('btd,de->bte', h, w_in).astype(jnp.float32)
    q, f_pre, i_in, g, xb, yb = split_cols(z, EVEN_SPLIT)
    log_f = jax.nn.log_sigmoid(f_pre) + jnp.logaddexp(0.0, jnp.log(lb) - f_pre)
    k = -jnp.expm1(log_f)
    heads = lambda a: a.reshape(B, T, A_HEADS, A_DK)
    o, S = gated_linear_attention(heads(q) * A_DK ** -0.5, heads(k), heads(i_in), heads(log_f), s_hgrn)
    out_a = head_rmsnorm(o.reshape(B, T, A_WIDTH), hgrn_norm, A_HEADS) * jax.nn.silu(g)
    xc, conv_buf = causal_conv(xb, s_lru_conv, conv_w, conv_b)
    hl = rg_lru(xc, pos0, s_lru_h.astype(jnp.float32), wa, ba, wx, bx, lru_a)
    out_b = hl * jax.nn.gelu(yb)
    mix = jnp.einsum('bte,ed->btd', jnp.concatenate([out_a, out_b], axis=-1).astype(h.dtype), w_out)
    return mix, S, hl[:, -1], conv_buf


def odd_mixer(h, s_C, s_n, s_m, s_gla, w_in, bi, bf, mlstm_norm, wa2, ba, gla_norm, w_out):
    B, T, _ = h.shape
    z = jnp.einsum('btd,de->bte', h, w_in).astype(jnp.float32)
    qc, kc, vc, oc, ic, fc, qd, kd, vd, gd, ad = split_cols(z, ODD_SPLIT)
    hc = lambda a: a.reshape(B, T, C_HEADS, C_DH)
    hcell, C, n, m = mlstm_chunked(hc(qc), hc(kc) * C_DH ** -0.5, hc(vc), ic + bi, jax.nn.log_sigmoid(fc + bf), s_C, s_n, s_m)
    out_c = head_rmsnorm(hcell.reshape(B, T, C_WIDTH), mlstm_norm, C_HEADS) * jax.nn.sigmoid(oc)
    log_alpha = jax.nn.log_sigmoid(jnp.einsum('btr,re->bte', ad, wa2) + ba) / D_TAU
    hd = lambda a, dh: a.reshape(B, T, D_HEADS, dh)
    o, Sg = gated_linear_attention(hd(qd, D_DK) * D_DK ** -0.5, hd(kd, D_DK), hd(vd, D_DV), hd(log_alpha, D_DK), s_gla)
    out_d = head_rmsnorm(o.reshape(B, T, D_WIDTH), gla_norm, D_HEADS) * jax.nn.silu(gd)
    mix = jnp.einsum('bte,ed->btd', jnp.concatenate([out_c, out_d], axis=-1).astype(h.dtype), w_out)
    return mix, C, n, m, Sg


def swiglu(h, w_up, w_down):
    gate, val = jnp.split(jnp.einsum('btd,de->bte', h, w_up), [D_FF], axis=-1)
    return jnp.einsum('btf,fd->btd', jax.nn.silu(gate) * val, w_down)


def trunk(x, pos0, st_hgrn, st_lru_h, st_lru_conv, st_C, st_n, st_m, st_gla, weights):
    (norm_mix, norm_ffn, norm_final, w_in_even, hgrn_lb, hgrn_norm, lru_conv_w, lru_conv_b, lru_wa, lru_ba,
     lru_wx, lru_bx, lru_a, w_out_even, w_in_odd, mlstm_bi, mlstm_bf, mlstm_norm, gla_wa2, gla_ba, gla_norm,
     w_out_odd, ffn_w_up, ffn_w_down) = weights
    lb_all = jnp.cumsum(jax.nn.softmax(hgrn_lb.astype(jnp.float32), axis=0), axis=0)
    lb_all = lb_all - lb_all[:1]
    hgrn_l, lru_h_l, lru_conv_l, C_l, n_l, m_l, gla_l = [], [], [], [], [], [], []
    for l in range(DEPTH):
        j = l // 2
        hn = rmsnorm(x, norm_mix[l])
        if l % 2 == 0:
            mix, S, hl, cb = even_mixer(hn, pos0, st_hgrn[j], st_lru_h[j], st_lru_conv[j], lb_all[j], w_in_even[j],
                                        hgrn_norm[j], lru_conv_w[j], lru_conv_b[j], lru_wa[j], lru_ba[j], lru_wx[j],
                                        lru_bx[j], lru_a[j], w_out_even[j])
            hgrn_l.append(S)
            lru_h_l.append(hl)
            lru_conv_l.append(cb)
        else:
            mix, C, n, m, Sg = odd_mixer(hn, st_C[j], st_n[j], st_m[j], st_gla[j], w_in_odd[j], mlstm_bi[j], mlstm_bf[j],
                                         mlstm_norm[j], gla_wa2[j], gla_ba[j], gla_norm[j], w_out_odd[j])
            C_l.append(C)
            n_l.append(n)
            m_l.append(m)
            gla_l.append(Sg)
        x = x + mix
        x = x + swiglu(rmsnorm(x, norm_ffn[l]), ffn_w_up[l], ffn_w_down[l])
    y = rmsnorm(x, norm_final)
    return (y, jnp.stack(hgrn_l), jnp.stack(lru_h_l), jnp.stack(lru_conv_l), jnp.stack(C_l), jnp.stack(n_l),
            jnp.stack(m_l), jnp.stack(gla_l))


def setup_inputs(seed: int = 0) -> dict:
    key = jax.random.key(seed)
    keys = iter(jax.random.split(key, 48))
    f32 = jnp.float32

    def nrm(shape, scale):
        return scale * jax.random.normal(next(keys), shape, f32)

    def gain(shape):
        return 1.0 + 0.05 * jax.random.normal(next(keys), shape, f32)

    s_d = D_MODEL ** -0.5
    u = jax.random.uniform(next(keys), (N_EVEN, B_WIDTH), f32, 0.9, 0.999)
    sig = u ** (1.0 / LRU_C)
    lru_a = jnp.log(sig) - jnp.log1p(-sig)
    bw = B_WIDTH // B_BLOCKS
    return {
        'x_prompt': nrm((BATCH, SEQ, D_MODEL), 1.0),
        'x_sample': nrm((DEC_BATCH, DEC_SEQ, D_MODEL), 1.0),
        'state_hgrn': nrm((N_EVEN, DEC_BATCH, A_HEADS, A_DK, A_DK), 0.5),
        'state_lru_h': nrm((N_EVEN, DEC_BATCH, B_WIDTH), 0.5),
        'state_lru_conv': nrm((N_EVEN, DEC_BATCH, CONV_K - 1, B_WIDTH), 1.0),
        'state_mlstm_C': nrm((N_ODD, DEC_BATCH, C_HEADS, C_DH, C_DH), 0.1),
        'state_mlstm_n': nrm((N_ODD, DEC_BATCH, C_HEADS, C_DH), 0.3),
        'state_mlstm_m': nrm((N_ODD, DEC_BATCH, C_HEADS), 1.0),
        'state_gla': nrm((N_ODD, DEC_BATCH, D_HEADS, D_DK, D_DV), 0.3),
        'norm_mix': gain((DEPTH, D_MODEL)),
        'norm_ffn': gain((DEPTH, D_MODEL)),
        'norm_final': gain((D_MODEL,)),
        'w_in_even': nrm((N_EVEN, D_MODEL, EVEN_IN), s_d),
        'hgrn_lb': nrm((N_EVEN, A_WIDTH), 1.0),
        'hgrn_norm': gain((N_EVEN, A_WIDTH)),
        'lru_conv_w': nrm((N_EVEN, CONV_K, B_WIDTH), 0.5),
        'lru_conv_b': nrm((N_EVEN, B_WIDTH), 0.01),
        'lru_wa': nrm((N_EVEN, B_BLOCKS, bw, bw), bw ** -0.5),
        'lru_ba': nrm((N_EVEN, B_WIDTH), 0.01),
        'lru_wx': nrm((N_EVEN, B_BLOCKS, bw, bw), bw ** -0.5),
        'lru_bx': nrm((N_EVEN, B_WIDTH), 0.01),
        'lru_a': lru_a,
        'w_out_even': nrm((N_EVEN, MIX_WIDTH, D_MODEL), MIX_WIDTH ** -0.5),
        'w_in_odd': nrm((N_ODD, D_MODEL, ODD_IN), s_d),
        'mlstm_bi': nrm((N_ODD, C_HEADS), 0.1),
        'mlstm_bf': jnp.linspace(3.0, 6.0, C_HEADS, dtype=f32)[None, :] + nrm((N_ODD, C_HEADS), 0.1),
        'mlstm_norm': gain((N_ODD, C_WIDTH)),
        'gla_wa2': nrm((N_ODD, D_RANK, D_HEADS * D_DK), D_RANK ** -0.5),
        'gla_ba': nrm((N_ODD, D_HEADS * D_DK), 0.1),
        'gla_norm': gain((N_ODD, D_WIDTH)),
        'w_out_odd': nrm((N_ODD, MIX_WIDTH, D_MODEL), MIX_WIDTH ** -0.5),
        'ffn_w_up': nrm((DEPTH, D_MODEL, 2 * D_FF), s_d),
        'ffn_w_down': nrm((DEPTH, D_FF, D_MODEL), D_FF ** -0.5),
    }


def reference(x_prompt, x_sample, state_hgrn, state_lru_h, state_lru_conv, state_mlstm_C, state_mlstm_n, state_mlstm_m,
              state_gla, norm_mix, norm_ffn, norm_final, w_in_even, hgrn_lb, hgrn_norm, lru_conv_w, lru_conv_b, lru_wa,
              lru_ba, lru_wx, lru_bx, lru_a, w_out_even, w_in_odd, mlstm_bi, mlstm_bf, mlstm_norm, gla_wa2, gla_ba,
              gla_norm, w_out_odd, ffn_w_up, ffn_w_down):
    weights = (norm_mix, norm_ffn, norm_final, w_in_even, hgrn_lb, hgrn_norm, lru_conv_w, lru_conv_b, lru_wa, lru_ba,
               lru_wx, lru_bx, lru_a, w_out_even, w_in_odd, mlstm_bi, mlstm_bf, mlstm_norm, gla_wa2, gla_ba, gla_norm,
               w_out_odd, ffn_w_up, ffn_w_down)
    f32 = jnp.float32
    y_prompt, hgrn_p, lru_h_p, lru_conv_p, mlstm_C_p, mlstm_n_p, mlstm_m_p, gla_p = trunk(
        x_prompt, 0,
        jnp.zeros((N_EVEN, BATCH, A_HEADS, A_DK, A_DK), f32),
        jnp.zeros((N_EVEN, BATCH, B_WIDTH), f32),
        jnp.zeros((N_EVEN, BATCH, CONV_K - 1, B_WIDTH), f32),
        jnp.zeros((N_ODD, BATCH, C_HEADS, C_DH, C_DH), f32),
        jnp.zeros((N_ODD, BATCH, C_HEADS, C_DH), f32),
        jnp.zeros((N_ODD, BATCH, C_HEADS), f32),
        jnp.zeros((N_ODD, BATCH, D_HEADS, D_DK, D_DV), f32),
        weights)
    y_sample, hgrn_s, lru_h_s, lru_conv_s, mlstm_C_s, mlstm_n_s, mlstm_m_s, gla_s = trunk(
        x_sample, PAST_LEN, state_hgrn, state_lru_h, state_lru_conv, state_mlstm_C, state_mlstm_n, state_mlstm_m,
        state_gla, weights)
    return (y_prompt, y_sample, hgrn_p, hgrn_s, lru_h_p, lru_h_s, lru_conv_p, lru_conv_s,
            mlstm_C_p, mlstm_C_s, mlstm_n_p, mlstm_n_s, mlstm_m_p, mlstm_m_s, gla_p, gla_s)
```

```python
import functools
import math

import jax
import jax.numpy as jnp
from jax import lax
from jax.experimental import pallas as pl
from jax.experimental.pallas import tpu as pltpu

F32 = jnp.float32
BF16 = jnp.bfloat16
EPS = 1e-6
LANES = 128
GROUP = 64
CHUNK = 64
CONV_K = 4
LRU_C = 8.0
GLA_TAU = 16.0
PAST_LEN = 16384
MIB = 1024 * 1024
HIGHEST = lax.Precision.HIGHEST
NEG_INF = float("-inf")


def _log_sigmoid(x):
    return jnp.minimum(x, 0.0) - jnp.log1p(jnp.exp(-jnp.abs(x)))


def _softplus(x):
    return jnp.maximum(x, 0.0) + jnp.log1p(jnp.exp(-jnp.abs(x)))


def _neg_expm1(y):
    return -jnp.tanh(0.5 * y) * (jnp.exp(y) + 1.0)


def _silu(x):
    return x * jax.nn.sigmoid(x)


def _gelu_tanh(x):
    return 0.5 * x * (1.0 + jnp.tanh(math.sqrt(2.0 / math.pi) * (x + 0.044715 * (x * x * x))))


def _rms(x):
    return x * lax.rsqrt(jnp.mean(x * x, axis=-1, keepdims=True) + EPS)


def _dot(a, b):
    return jnp.dot(a, b, preferred_element_type=F32)


def _dot_nt(a, b):
    return lax.dot_general(a, b, (((1,), (1,)), ((), ())), preferred_element_type=F32)


def _dot_tn(a, b):
    return lax.dot_general(a, b, (((0,), (0,)), ((), ())), preferred_element_type=F32)


def _dot_f32(a, b):
    return jnp.dot(a, b, preferred_element_type=F32, precision=HIGHEST)


def _cumsum_matrices(L, c):
    row = lax.broadcasted_iota(jnp.int32, (GROUP, GROUP), 0)
    col = lax.broadcasted_iota(jnp.int32, (GROUP, GROUP), 1)
    same_chunk = (row // L) == (col // L)
    m_chunk = jnp.where(same_chunk & (col <= row), 1.0, 0.0)
    m_sub = jnp.where(((row // c) == (col // c)) & (col <= row), 1.0, 0.0)
    m_anchor = jnp.where(same_chunk & ((col // c) < (row // c)), 1.0, 0.0)
    return m_chunk, m_sub, m_anchor


def _gla_chunk(qsc, qs, k, bl, r, vs, masks, S, L, c):
    nsb = L // c
    b = bl if r is None else bl + r
    b_last = b[L - 1:L, :]
    qe = qs if r is None else qs * jnp.exp(r)
    kd = k * jnp.exp(b_last - b)
    s_bf = S.astype(BF16)
    row = lax.broadcasted_iota(jnp.int32, (c, LANES), 0)

    def sel(a, m):
        return a if m is None else jnp.where(m, a, 0.0)

    vbs = [v.astype(BF16) for v in vs]
    o_inter = [_dot(sel(qe, m).astype(BF16), s_bf) for m in masks]
    pieces = [[] for _ in vs]
    for i in range(nsb):
        sl = slice(c * i, c * (i + 1))
        o_i = [o[sl] for o in o_inter]
        if i > 0:
            ks = (k[0:c * i] * jnp.exp(r[c * i:c * i + 1, :] - b[0:c * i])).astype(BF16)
            for u, m in enumerate(masks):
                p = _dot_nt(sel(qs[sl], m).astype(BF16), ks)
                o_i[u] = o_i[u] + _dot(p.astype(BF16), vbs[u][0:c * i])
        bq, bk, bb = qsc[sl], k[sl], bl[sl]
        for s in range(c):
            e = jnp.exp(jnp.where(row >= s, bb - bb[s:s + 1, :], NEG_INF))
            term = bq * e * bk[s:s + 1, :]
            for u, m in enumerate(masks):
                col = jnp.sum(sel(term, m), axis=-1, keepdims=True)
                o_i[u] = o_i[u] + col * vs[u][c * i + s:c * i + s + 1, :]
        for u in range(len(vs)):
            pieces[u].append(o_i[u])
    outs = [p[0] if nsb == 1 else jnp.concatenate(p, axis=0) for p in pieces]
    upd = None
    for u, m in enumerate(masks):
        t = _dot_tn(sel(kd, m).astype(BF16), vbs[u])
        upd = t if upd is None else upd + t
    decay_col = jnp.transpose(jnp.broadcast_to(jnp.exp(b_last), (8, LANES)))[:, 0:1]
    return outs, decay_col * S + upd


def _norm_matmul_kernel(x_ref, g_ref, w_ref, o_ref, xn_ref):
    @pl.when(pl.program_id(1) == 0)
    def _():
        xn_ref[...] = (_rms(x_ref[...]) * g_ref[...]).astype(BF16)

    o_ref[...] = _dot(xn_ref[...], w_ref[...])


def _norm_matmul(x, g, w, tm, tn):
    n, d = x.shape
    e = w.shape[1]
    return pl.pallas_call(
        _norm_matmul_kernel,
        out_shape=jax.ShapeDtypeStruct((n, e), F32),
        grid=(n // tm, e // tn),
        in_specs=[
            pl.BlockSpec((tm, d), lambda i, j: (i, 0)),
            pl.BlockSpec((1, d), lambda i, j: (0, 0)),
            pl.BlockSpec((d, tn), lambda i, j: (0, j)),
        ],
        out_specs=pl.BlockSpec((tm, tn), lambda i, j: (i, j)),
        scratch_shapes=[pltpu.VMEM((tm, d), BF16)],
        compiler_params=pltpu.CompilerParams(
            dimension_semantics=("parallel", "arbitrary"), vmem_limit_bytes=48 * MIB),
        name="norm_in_proj",
    )(x, g, w)


def _proj_ffn_kernel(x_ref, mix_ref, wo_ref, g_ref, wup_ref, wdn_ref, gf_ref, o_ref, *, d_ff, tf, final):
    xnew = x_ref[...] + _dot(mix_ref[...], wo_ref[...])
    h2 = (_rms(xnew) * g_ref[...]).astype(BF16)
    o_ref[...] = xnew
    for f in range(d_ff // tf):
        gate = _dot(h2, wup_ref[:, f * tf:(f + 1) * tf])
        val = _dot(h2, wup_ref[:, d_ff + f * tf:d_ff + (f + 1) * tf])
        act = (_silu(gate) * val).astype(BF16)
        o_ref[...] += _dot(act, wdn_ref[f * tf:(f + 1) * tf, :])
    if final:
        o_ref[...] = _rms(o_ref[...]) * gf_ref[...]


def _proj_ffn(x, mix, wo, g, wup, wdn, gf, final, tm=512, tf=256):
    n, d = x.shape
    d_ff = wdn.shape[0]
    const = lambda shape: pl.BlockSpec(shape, lambda i: (0, 0), pipeline_mode=pl.Buffered(1))
    return pl.pallas_call(
        functools.partial(_proj_ffn_kernel, d_ff=d_ff, tf=tf, final=final),
        out_shape=jax.ShapeDtypeStruct((n, d), F32),
        grid=(n // tm,),
        in_specs=[
            pl.BlockSpec((tm, d), lambda i: (i, 0)),
            pl.BlockSpec((tm, d), lambda i: (i, 0)),
            const((d, d)),
            const((1, d)),
            const((d, 2 * d_ff)),
            const((d_ff, d)),
            const((1, d)),
        ],
        out_specs=pl.BlockSpec((tm, d), lambda i: (i, 0)),
        compiler_params=pltpu.CompilerParams(
            dimension_semantics=("parallel",), vmem_limit_bytes=52 * MIB),
        name="out_proj_ffn",
    )(x, mix, wo, g, wup, wdn, gf)


def _even_mixer_kernel(z_ref, s_in, h_in, c_in, lb_ref, hnorm_ref, cw_ref, cb_ref, wa_ref, ba_ref,
                       wx_ref, bx_ref, la_ref,
                       mix_ref, s_out, h_out, c_out,
                       mixf, qsc_s, qs_s, k_s, bl_s, r_s, xpad,
                       *, layer_j, nseq, tb, seq, L, c, pos0, heads, width):
    ti = pl.program_id(1)
    R = nseq * tb
    dk = width // heads

    @pl.when(ti == 0)
    def _():
        s_out[...] = s_in[...]
        h_out[...] = h_in[...]
        c_out[...] = c_in[...]

    hl = lb_ref[...]
    e = jnp.exp(hl - jnp.max(hl, axis=0, keepdims=True))
    sm = e / jnp.sum(e, axis=0, keepdims=True)
    lb = jnp.zeros((1, width), F32)
    for i in range(1, layer_j + 1):
        lb = lb + sm[i:i + 1, :]
    log_lb = jnp.log(lb)

    m_chunk, m_sub, m_anchor = _cumsum_matrices(L, c)
    for g in range(R // GROUP):
        rows = slice(GROUP * g, GROUP * (g + 1))
        x = z_ref[rows, width:2 * width]
        y = log_lb - x
        logf = _log_sigmoid(x) + jnp.maximum(y, 0.0) + jnp.log1p(jnp.exp(-jnp.abs(y)))
        k_s[rows, :] = _neg_expm1(logf)
        bl = _dot_f32(m_sub, logf)
        bl_s[rows, :] = bl
        if c < L:
            r_s[rows, :] = _dot_f32(m_anchor, logf)
        qsc = z_ref[rows, 0:width] * (dk ** -0.5)
        qsc_s[rows, :] = qsc
        qs_s[rows, :] = qsc * jnp.exp(bl)

    def chunk_body(ci, carry):
        r0 = pl.multiple_of(ci * L, L)
        si = ci // (tb // L)
        rows = pl.ds(r0, L)
        for h in range(heads):
            lanes = slice(dk * h, dk * (h + 1))
            v = z_ref[rows, 2 * width + dk * h:2 * width + dk * (h + 1)]
            outs, s_new = _gla_chunk(
                qsc_s[rows, lanes], qs_s[rows, lanes], k_s[rows, lanes], bl_s[rows, lanes],
                r_s[rows, lanes] if c < L else None, [v], [None], s_out[si, h], L, c)
            s_out[si, h] = s_new
            gate = z_ref[rows, 3 * width + dk * h:3 * width + dk * (h + 1)]
            mixf[rows, lanes] = _rms(outs[0]) * hnorm_ref[:, lanes] * _silu(gate)
        return carry

    lax.fori_loop(0, R // L, chunk_body, 0)

    xoff, yoff = 4 * width, 5 * width
    xpad[:, 8 - (CONV_K - 1):8, :] = c_out[...]
    xpad[:, 8:, :] = z_ref[:, xoff:xoff + width].reshape(nseq, tb, width)
    xc = cb_ref[...].reshape(1, 1, width) + jnp.zeros((nseq, tb, width), F32)
    for j in range(CONV_K):
        lo = 8 - (CONV_K - 1) + j
        xc = xc + xpad[:, lo:lo + tb, :] * cw_ref[j:j + 1, :].reshape(1, 1, width)
    c_out[...] = xpad[:, tb + 8 - (CONV_K - 1):tb + 8, :]
    xc = xc.reshape(R, width)
    xcb = xc.astype(BF16)
    rg = jax.nn.sigmoid(_dot(xcb, wa_ref[...]) + ba_ref[...])
    ig = jax.nn.sigmoid(_dot(xcb, wx_ref[...]) + bx_ref[...])
    log_a = (-LRU_C) * rg * _softplus(-la_ref[...])
    a = jnp.exp(log_a)
    mult = jnp.sqrt(_neg_expm1(2.0 * log_a))
    row_in_seq = lax.broadcasted_iota(jnp.int32, (R, width), 0) % tb
    if pos0 <= 0 < pos0 + seq:
        mult = jnp.where(row_in_seq + ti * tb + pos0 == 0, 1.0, mult)
    hh = mult * ig * xc
    aa = a
    shift = 1
    while shift < tb:
        m = row_in_seq >= shift
        hh_new = jnp.where(m, aa * pltpu.roll(hh, shift, 0) + hh, hh)
        aa = jnp.where(m, aa * pltpu.roll(aa, shift, 0), aa)
        hh = hh_new
        shift *= 2
    hseq = hh.reshape(nseq, tb, width) + aa.reshape(nseq, tb, width) * h_out[...]
    h_out[...] = hseq[:, tb - 1:tb, :]
    mixf[:, width:2 * width] = hseq.reshape(R, width) * _gelu_tanh(z_ref[:, yoff:yoff + width])
    mix_ref[...] = mixf[...].astype(BF16)


def _even_mixer(z, s_all, h_all, c_all, layer_j, batch, seq, nseq, tb, L, c, pos0, params):
    (hgrn_lb, hgrn_norm, conv_w, conv_b, wa, ba, wx, bx, lru_a) = params
    n, e = z.shape
    heads, dk = s_all.shape[2], s_all.shape[3]
    width = heads * dk
    R = nseq * tb
    nt = seq // tb
    const = lambda a: pl.BlockSpec(a.shape, lambda b, t: (0,) * a.ndim)
    kern = functools.partial(_even_mixer_kernel, layer_j=layer_j, nseq=nseq, tb=tb, seq=seq, L=L, c=c, pos0=pos0,
                             heads=heads, width=width)
    sblk = (None, nseq, heads, dk, dk)
    hblk = (None, nseq, 1, width)
    cblk = (None, nseq, CONV_K - 1, width)
    return pl.pallas_call(
        kern,
        out_shape=(
            jax.ShapeDtypeStruct((n, 2 * width), BF16),
            jax.ShapeDtypeStruct((batch, heads, dk, dk), F32),
            jax.ShapeDtypeStruct((batch, 1, width), F32),
            jax.ShapeDtypeStruct((batch, CONV_K - 1, width), F32),
        ),
        grid=(batch // nseq, nt),
        in_specs=[
            pl.BlockSpec((R, e), lambda b, t: (b * nt + t, 0)),
            pl.BlockSpec(sblk, lambda b, t: (layer_j, b, 0, 0, 0)),
            pl.BlockSpec(hblk, lambda b, t: (layer_j, b, 0, 0)),
            pl.BlockSpec(cblk, lambda b, t: (layer_j, b, 0, 0)),
            const(hgrn_lb), const(hgrn_norm), const(conv_w), const(conv_b),
            const(wa), const(ba), const(wx), const(bx), const(lru_a),
        ],
        out_specs=(
            pl.BlockSpec((R, 2 * width), lambda b, t: (b * nt + t, 0)),
            pl.BlockSpec(sblk[1:], lambda b, t: (b, 0, 0, 0)),
            pl.BlockSpec(hblk[1:], lambda b, t: (b, 0, 0)),
            pl.BlockSpec(cblk[1:], lambda b, t: (b, 0, 0)),
        ),
        scratch_shapes=[
            pltpu.VMEM((R, 2 * width), F32),
            pltpu.VMEM((R, width), F32), pltpu.VMEM((R, width), F32), pltpu.VMEM((R, width), F32),
            pltpu.VMEM((R, width), F32), pltpu.VMEM((R, width), F32),
            pltpu.VMEM((nseq, tb + 8, width), F32),
        ],
        compiler_params=pltpu.CompilerParams(
            dimension_semantics=("parallel", "arbitrary"), vmem_limit_bytes=48 * MIB),
        name="even_mixer",
    )(z, s_all, h_all, c_all, hgrn_lb, hgrn_norm, conv_w, conv_b, wa, ba, wx, bx, lru_a)


def _odd_mixer_kernel(z_ref, c_in, n_in, m_in, g_in, bias_ref, mnorm_ref, wa2_ref, gba_ref, gnorm_ref,
                      mix_ref, c_out, n_out, m_out, g_out,
                      mixf, gate_s, qsc_s, qs_s, bl_s, r_s,
                      *, nseq, tb, L, c, heads, dh, gla_pairs):
    ti = pl.program_id(1)
    R = nseq * tb
    cw = heads * dh
    gk = gla_pairs * LANES
    q_off, k_off, v_off, o_off = 0, cw, 2 * cw, 3 * cw
    gq_off = 4 * cw
    gk_off = gq_off + gk
    gv_off = gk_off + gk
    gg_off = gv_off + cw
    small_off = gg_off + cw
    gla_dk = gk // heads

    @pl.when(ti == 0)
    def _():
        c_out[...] = c_in[...]
        n_out[...] = n_in[...]
        m_out[...] = m_in[...]
        g_out[...] = g_in[...]

    m_chunk, m_sub, m_anchor = _cumsum_matrices(L, c)
    lane = lax.broadcasted_iota(jnp.int32, (1, LANES), 1)
    is_f = (lane >= heads) & (lane < 2 * heads)
    for g in range(R // GROUP):
        rows = slice(GROUP * g, GROUP * (g + 1))
        small = z_ref[rows, small_off:small_off + LANES]
        g0 = small + bias_ref[...]
        cums = _dot_f32(m_chunk, _log_sigmoid(g0))
        gate_s[rows, :] = jnp.where(is_f, cums, g0)
        pre = _dot(small.astype(BF16), wa2_ref[...]) + gba_ref[...]
        la = _log_sigmoid(pre) * (1.0 / GLA_TAU)
        bl = _dot_f32(m_sub, la)
        bl_s[rows, :] = bl
        if c < L:
            r_s[rows, :] = _dot_f32(m_anchor, la)
        qsc = z_ref[rows, gq_off:gq_off + gk] * (gla_dk ** -0.5)
        qsc_s[rows, :] = qsc
        qs_s[rows, :] = qsc * jnp.exp(bl)

    tril = (lax.broadcasted_iota(jnp.int32, (L, L), 0) >= lax.broadcasted_iota(jnp.int32, (L, L), 1))
    half = LANES // 2
    gla_masks = [lane < half, lane >= half]

    def chunk_body(ci, carry):
        r0 = pl.multiple_of(ci * L, L)
        si = ci // (tb // L)
        rows = pl.ds(r0, L)
        gt = gate_s[rows, :]
        gtt = jnp.transpose(gt)
        m_vec = m_out[si]
        m_next = m_vec
        for h in range(heads):
            lanes = slice(dh * h, dh * (h + 1))
            i_col, b_col = gt[:, h:h + 1], gt[:, heads + h:heads + h + 1]
            i_row, b_row = gtt[h:h + 1, :], gtt[heads + h:heads + h + 1, :]
            m_prev = m_vec[:, h:h + 1]
            q = z_ref[rows, q_off + dh * h:q_off + dh * (h + 1)]
            k = z_ref[rows, k_off + dh * h:k_off + dh * (h + 1)] * (dh ** -0.5)
            v = z_ref[rows, v_off + dh * h:v_off + dh * (h + 1)]
            qb, kb, vb = q.astype(BF16), k.astype(BF16), v.astype(BF16)
            cmat = c_out[si, h]
            nrow = n_out[si, :, lanes]
            dlog = jnp.where(tril, b_col - b_row + i_row, NEG_INF)
            inter = b_col + m_prev
            m_t = jnp.maximum(inter, jnp.max(dlog, axis=-1, keepdims=True))
            w_intra = jnp.exp(dlog - m_t)
            w_inter = jnp.exp(inter - m_t)
            qk = _dot_nt(qb, kb) * w_intra
            num = _dot(qk.astype(BF16), vb) + w_inter * _dot(qb, cmat.astype(BF16))
            den = jnp.sum(qk, axis=-1, keepdims=True) + w_inter * jnp.sum(q * nrow, axis=-1, keepdims=True)
            hcell = num / jnp.maximum(jnp.abs(den), jnp.exp(-m_t))
            b_last = b_col[L - 1:L, :]
            m_new = jnp.maximum(b_last + m_prev, jnp.max(b_last - b_row + i_row, axis=-1, keepdims=True))
            w_s = jnp.exp(b_last - b_col + i_col - m_new)
            decay = jnp.exp(b_last + m_prev - m_new)
            kw = k * w_s
            c_out[si, h] = decay * cmat + _dot_tn(kw.astype(BF16), vb)
            n_out[si, :, lanes] = decay * nrow + jnp.sum(kw, axis=0, keepdims=True)
            m_next = jnp.where(lane == h, m_new, m_next)
            ogate = z_ref[rows, o_off + dh * h:o_off + dh * (h + 1)]
            mixf[rows, lanes] = _rms(hcell) * mnorm_ref[:, lanes] * jax.nn.sigmoid(ogate)
        m_out[si] = m_next
        for p in range(gla_pairs):
            lanes = slice(LANES * p, LANES * (p + 1))
            hs = [2 * p, 2 * p + 1]
            vs = [z_ref[rows, gv_off + dh * h:gv_off + dh * (h + 1)] for h in hs]
            kk = z_ref[rows, gk_off + LANES * p:gk_off + LANES * (p + 1)]
            outs, s_new = _gla_chunk(
                qsc_s[rows, lanes], qs_s[rows, lanes], kk, bl_s[rows, lanes],
                r_s[rows, lanes] if c < L else None, vs, gla_masks, g_out[si, p], L, c)
            g_out[si, p] = s_new
            for h, o in zip(hs, outs):
                vl = slice(dh * h, dh * (h + 1))
                gg = z_ref[rows, gg_off + dh * h:gg_off + dh * (h + 1)]
                mixf[rows, cw + dh * h:cw + dh * (h + 1)] = _rms(o) * gnorm_ref[:, vl] * _silu(gg)
        return carry

    lax.fori_loop(0, R // L, chunk_body, 0)
    mix_ref[...] = mixf[...].astype(BF16)


def _odd_mixer(z, c_all, n_all, m_all, g_all, layer_j, batch, seq, nseq, tb, L, c, params):
    (bias_small, mnorm, wa2_pad, gla_ba, gnorm) = params
    n, e = z.shape
    heads, dh = c_all.shape[2], c_all.shape[3]
    cw = heads * dh
    gla_pairs = g_all.shape[2]
    gk = gla_pairs * LANES
    R = nseq * tb
    nt = seq // tb
    const = lambda a: pl.BlockSpec(a.shape, lambda b, t: (0,) * a.ndim)
    kern = functools.partial(_odd_mixer_kernel, nseq=nseq, tb=tb, L=L, c=c, heads=heads, dh=dh,
                             gla_pairs=gla_pairs)
    cblk = (None, nseq, heads, dh, dh)
    nblk = (None, nseq, 1, cw)
    mblk = (None, nseq, 1, LANES)
    gblk = (None, nseq, gla_pairs, LANES, dh)
    return pl.pallas_call(
        kern,
        out_shape=(
            jax.ShapeDtypeStruct((n, 2 * cw), BF16),
            jax.ShapeDtypeStruct((batch, heads, dh, dh), F32),
            jax.ShapeDtypeStruct((batch, 1, cw), F32),
            jax.ShapeDtypeStruct((batch, 1, LANES), F32),
            jax.ShapeDtypeStruct((batch, gla_pairs, LANES, dh), F32),
        ),
        grid=(batch // nseq, nt),
        in_specs=[
            pl.BlockSpec((R, e), lambda b, t: (b * nt + t, 0)),
            pl.BlockSpec(cblk, lambda b, t: (layer_j, b, 0, 0, 0)),
            pl.BlockSpec(nblk, lambda b, t: (layer_j, b, 0, 0)),
            pl.BlockSpec(mblk, lambda b, t: (layer_j, b, 0, 0)),
            pl.BlockSpec(gblk, lambda b, t: (layer_j, b, 0, 0, 0)),
            const(bias_small), const(mnorm), const(wa2_pad), const(gla_ba), const(gnorm),
        ],
        out_specs=(
            pl.BlockSpec((R, 2 * cw), lambda b, t: (b * nt + t, 0)),
            pl.BlockSpec(cblk[1:], lambda b, t: (b, 0, 0, 0)),
            pl.BlockSpec(nblk[1:], lambda b, t: (b, 0, 0)),
            pl.BlockSpec(mblk[1:], lambda b, t: (b, 0, 0)),
            pl.BlockSpec(gblk[1:], lambda b, t: (b, 0, 0, 0)),
        ),
        scratch_shapes=[
            pltpu.VMEM((R, 2 * cw), F32),
            pltpu.VMEM((R, LANES), F32),
            pltpu.VMEM((R, gk), F32), pltpu.VMEM((R, gk), F32),
            pltpu.VMEM((R, gk), F32), pltpu.VMEM((R, gk), F32),
        ],
        compiler_params=pltpu.CompilerParams(
            dimension_semantics=("parallel", "arbitrary"), vmem_limit_bytes=48 * MIB),
        name="odd_mixer",
    )(z, c_all, n_all, m_all, g_all, bias_small, mnorm, wa2_pad, gla_ba, gnorm)


def _block_diag(w):
    nb, ci, di = w.shape
    eye = jnp.eye(nb, dtype=w.dtype)
    return (eye[:, None, :, None] * w[:, :, None, :]).reshape(nb * ci, nb * di)


def kernel(x_prompt, x_sample, state_hgrn, state_lru_h, state_lru_conv, state_mlstm_C, state_mlstm_n, state_mlstm_m, state_gla, norm_mix, norm_ffn, norm_final, w_in_even, hgrn_lb, hgrn_norm, lru_conv_w, lru_conv_b, lru_wa, lru_ba, lru_wx, lru_bx, lru_a, w_out_even, w_in_odd, mlstm_bi, mlstm_bf, mlstm_norm, gla_wa2, gla_ba, gla_norm, w_out_odd, ffn_w_up, ffn_w_down):
    batch, seq, d_model = x_prompt.shape
    dec_batch, dec_seq, _ = x_sample.shape
    depth = norm_mix.shape[0]
    n_even, _, a_heads, a_dk, _ = state_hgrn.shape
    n_odd, _, c_heads, c_dh, _ = state_mlstm_C.shape
    b_width = state_lru_h.shape[-1]
    a_width = a_heads * a_dk
    c_width = c_heads * c_dh
    d_heads, d_dk, d_dv = state_gla.shape[2:]
    gk = d_heads * d_dk
    d_width = d_heads * d_dv
    d_rank = gla_wa2.shape[1]
    gla_pairs = gk // LANES

    w_even = w_in_even.astype(BF16)
    main = 4 * c_width
    small_w = 2 * c_heads
    pad = (-(main + 2 * gk + 2 * d_width + LANES)) % (2 * LANES)
    w_odd = jnp.concatenate([
        w_in_odd[:, :, :main],
        w_in_odd[:, :, main + small_w:main + small_w + 2 * gk + 2 * d_width],
        w_in_odd[:, :, main:main + small_w],
        w_in_odd[:, :, main + small_w + 2 * gk + 2 * d_width:],
        jnp.zeros((n_odd, d_model, LANES - small_w - d_rank + pad), F32),
    ], axis=-1).astype(BF16)
    w_out_e = w_out_even.astype(BF16)
    w_out_o = w_out_odd.astype(BF16)
    w_up = ffn_w_up.astype(BF16)
    w_dn = ffn_w_down.astype(BF16)
    wa_dense = jax.vmap(_block_diag)(lru_wa).astype(BF16)
    wx_dense = jax.vmap(_block_diag)(lru_wx).astype(BF16)
    bias_small = jnp.concatenate(
        [mlstm_bi, mlstm_bf, jnp.zeros((n_odd, LANES - small_w), F32)], axis=-1)[:, None, :]
    wa2_pad = jnp.concatenate([
        jnp.zeros((n_odd, small_w, gk), F32), gla_wa2,
        jnp.zeros((n_odd, LANES - small_w - d_rank, gk), F32)], axis=1).astype(BF16)

    row = lambda a: a[None, :]
    zeros = lambda *s: jnp.zeros(s, F32)
    groups = [
        dict(batch=batch, seq=seq, nseq=1, tb=256, L=math.gcd(seq, CHUNK), c=16, pos0=0,
             x=x_prompt.reshape(batch * seq, d_model),
             hgrn=zeros(n_even, batch, a_heads, a_dk, a_dk),
             lru_h=zeros(n_even, batch, 1, b_width),
             lru_conv=zeros(n_even, batch, CONV_K - 1, b_width),
             mC=zeros(n_odd, batch, c_heads, c_dh, c_dh),
             mn=zeros(n_odd, batch, 1, c_width),
             mm=zeros(n_odd, batch, 1, LANES),
             gla=zeros(n_odd, batch, gla_pairs, LANES, d_dv)),
        dict(batch=dec_batch, seq=dec_seq, nseq=GROUP // dec_seq, tb=dec_seq, L=math.gcd(dec_seq, CHUNK),
             c=math.gcd(dec_seq, CHUNK), pos0=PAST_LEN,
             x=x_sample.reshape(dec_batch * dec_seq, d_model),
             hgrn=state_hgrn,
             lru_h=state_lru_h[:, :, None, :],
             lru_conv=state_lru_conv,
             mC=state_mlstm_C,
             mn=state_mlstm_n.reshape(n_odd, dec_batch, 1, c_width),
             mm=jnp.pad(state_mlstm_m, ((0, 0), (0, 0), (0, LANES - c_heads)))[:, :, None, :],
             gla=state_gla.reshape(n_odd, dec_batch, gla_pairs, LANES, d_dv)),
    ]

    results = []
    for grp in groups:
        x = grp["x"]
        n = x.shape[0]
        tm = min(1024, n)
        outs = dict(hgrn=[], lru_h=[], lru_conv=[], mC=[], mn=[], mm=[], gla=[])
        for l in range(depth):
            j = l // 2
            if l % 2 == 0:
                z = _norm_matmul(x, row(norm_mix[l]), w_even[j], tm, 1024)
                params = (hgrn_lb, row(hgrn_norm[j]), lru_conv_w[j], row(lru_conv_b[j]), wa_dense[j],
                          row(lru_ba[j]), wx_dense[j], row(lru_bx[j]), row(lru_a[j]))
                mix, s, h, cb = _even_mixer(z, grp["hgrn"], grp["lru_h"], grp["lru_conv"], j, grp["batch"],
                                            grp["seq"], grp["nseq"], grp["tb"], grp["L"], grp["c"], grp["pos0"], params)
                outs["hgrn"].append(s)
                outs["lru_h"].append(h[:, 0, :])
                outs["lru_conv"].append(cb)
                w_out = w_out_e[j]
            else:
                z = _norm_matmul(x, row(norm_mix[l]), w_odd[j], tm, w_odd.shape[-1] // 3)
                params = (bias_small[j], row(mlstm_norm[j]), wa2_pad[j], row(gla_ba[j]), row(gla_norm[j]))
                mix, cm, nn, mm, sg = _odd_mixer(z, grp["mC"], grp["mn"], grp["mm"], grp["gla"], j, grp["batch"],
                                                 grp["seq"], grp["nseq"], grp["tb"], grp["L"], grp["c"], params)
                outs["mC"].append(cm)
                outs["mn"].append(nn.reshape(grp["batch"], c_heads, c_dh))
                outs["mm"].append(mm[:, 0, :c_heads])
                outs["gla"].append(sg.reshape(grp["batch"], d_heads, d_dk, d_dv))
                w_out = w_out_o[j]
            x = _proj_ffn(x, mix, w_out, row(norm_ffn[l]), w_up[l], w_dn[l], row(norm_final),
                          final=(l == depth - 1))
        results.append((x.reshape(grp["batch"], grp["seq"], d_model),
                        {k: jnp.stack(v) for k, v in outs.items()}))

    (y_p, st_p), (y_s, st_s) = results
    return (y_p, y_s, st_p["hgrn"], st_s["hgrn"], st_p["lru_h"], st_s["lru_h"], st_p["lru_conv"], st_s["lru_conv"],
            st_p["mC"], st_s["mC"], st_p["mn"], st_s["mn"], st_p["mm"], st_s["mm"], st_p["gla"], st_s["gla"])
```

```python
import functools
import math

import jax
import jax.numpy as jnp
import numpy as np
from jax import lax
from jax.experimental import pallas as pl
from jax.experimental.pallas import tpu as pltpu

F32 = jnp.float32
BF16 = jnp.bfloat16
EPS = 1e-6
LANES = 128
GROUP = 64
CHUNK = 64
CONV_K = 4
LRU_C = 8.0
GLA_TAU = 16.0
PAST_LEN = 16384
MIB = 1024 * 1024
NEG_INF = float("-inf")


def _log1p_exp_neg_abs(x):
    return jnp.log(1.0 + jnp.exp(-jnp.abs(x)))


def _log_sigmoid(x):
    return jnp.minimum(x, 0.0) - _log1p_exp_neg_abs(x)


def _softplus(x):
    return jnp.maximum(x, 0.0) + _log1p_exp_neg_abs(x)


def _neg_expm1(y):
    return -jnp.tanh(0.5 * y) * (jnp.exp(y) + 1.0)


def _silu(x):
    return x * jax.nn.sigmoid(x)


def _gelu_tanh(x):
    return 0.5 * x * (1.0 + jnp.tanh(math.sqrt(2.0 / math.pi) * (x + 0.044715 * (x * x * x))))


def _rms(x):
    return x * lax.rsqrt(jnp.mean(x * x, axis=-1, keepdims=True) + EPS)


def _dot(a, b):
    return jnp.dot(a, b, preferred_element_type=F32)


def _dot_nt(a, b):
    return lax.dot_general(a, b, (((1,), (1,)), ((), ())), preferred_element_type=F32)


def _dot_tn(a, b):
    return lax.dot_general(a, b, (((0,), (0,)), ((), ())), preferred_element_type=F32)


def _level_tables(L):
    t = np.arange(GROUP)[:, None]
    s = np.arange(GROUP)[None, :]
    widths = []
    w = L // 2
    while w >= 1:
        widths.append(w)
        w //= 2
    mats, pmasks = [], []
    for w in widths:
        same = (t // (2 * w)) == (s // (2 * w))
        low_t, low_s = (t % (2 * w)) >= w, (s % (2 * w)) >= w
        mats.append((same & low_t & low_s & (s <= t)) | (same & ~low_t & ~low_s & (s > t)))
        pmasks.append(same & low_t & ~low_s)
    seg = (t // L) == (s // L)
    mats.append(seg & (s <= t))
    mats.append(seg & (s > t))
    pmasks.append(t == s)
    m_all = np.concatenate(mats, axis=0).astype(np.float32)
    m3 = np.concatenate([m_all, m_all, m_all], axis=1)
    return jnp.asarray(m3, BF16), jnp.asarray(np.stack(pmasks).astype(np.float32)), widths


def _level_sums(m3, g):
    g_hi = g.astype(BF16)
    r1 = g - g_hi.astype(F32)
    g_mid = r1.astype(BF16)
    g_lo = (r1 - g_mid.astype(F32)).astype(BF16)
    return _dot(m3, jnp.concatenate([g_hi, g_mid, g_lo], axis=0))


def _gla_tile(qsc, k, vs, masks, e_ref, lanes, pm_ref, lowers, L, s_ref, seq_ids, slot):
    nlev = len(lowers)

    def sel(a, m):
        return a if m is None else jnp.where(m, a, 0.0)

    scores = [None] * len(vs)
    for lv in range(nlev):
        y = jnp.where(lowers[lv], qsc, k) * e_ref[GROUP * lv:GROUP * (lv + 1), lanes]
        yb = y.astype(BF16)
        for u, m in enumerate(masks):
            yu = yb if m is None else jnp.where(m, y, 0.0).astype(BF16)
            part = _dot_nt(yu, yb) * pm_ref[lv]
            scores[u] = part if scores[u] is None else scores[u] + part
    qk = qsc * k
    vbs = [v.astype(BF16) for v in vs]
    intra = []
    for u, m in enumerate(masks):
        diag = pm_ref[nlev] * jnp.sum(sel(qk, m), axis=-1, keepdims=True)
        p = diag if scores[u] is None else scores[u] + diag
        intra.append(_dot(p.astype(BF16), vbs[u]))
    eb = e_ref[GROUP * nlev:GROUP * (nlev + 1), lanes]
    qe = qsc * eb
    kd = k * e_ref[GROUP * (nlev + 1):GROUP * (nlev + 2), lanes]
    pieces = [[] for _ in vs]
    for j in range(GROUP // L):
        sl = slice(L * j, L * (j + 1))
        state = s_ref[seq_ids[j], slot]
        sb = state.astype(BF16)
        upd = None
        for u, m in enumerate(masks):
            pieces[u].append(intra[u][sl] + _dot(sel(qe[sl], m).astype(BF16), sb))
            t = _dot_tn(sel(kd[sl], m).astype(BF16), vbs[u][sl])
            upd = t if upd is None else upd + t
        decay_col = jnp.transpose(jnp.broadcast_to(eb[L * j + L - 1:L * j + L, :], (8, LANES)))[:, 0:1]
        s_ref[seq_ids[j], slot] = decay_col * state + upd
    return [p[0] if len(p) == 1 else jnp.concatenate(p, axis=0) for p in pieces]


def _norm_matmul_kernel(x_ref, g_ref, w_ref, o_ref, xn_ref):
    @pl.when(pl.program_id(1) == 0)
    def _():
        xn_ref[...] = (_rms(x_ref[...]) * g_ref[...]).astype(BF16)

    o_ref[...] = _dot(xn_ref[...], w_ref[...])


def _norm_matmul(x, g, w, tm, tn):
    n, d = x.shape
    e = w.shape[1]
    return pl.pallas_call(
        _norm_matmul_kernel,
        out_shape=jax.ShapeDtypeStruct((n, e), F32),
        grid=(n // tm, e // tn),
        in_specs=[
            pl.BlockSpec((tm, d), lambda i, j: (i, 0)),
            pl.BlockSpec((1, d), lambda i, j: (0, 0)),
            pl.BlockSpec((d, tn), lambda i, j: (0, j)),
        ],
        out_specs=pl.BlockSpec((tm, tn), lambda i, j: (i, j)),
        scratch_shapes=[pltpu.VMEM((tm, d), BF16)],
        compiler_params=pltpu.CompilerParams(
            dimension_semantics=("parallel", "arbitrary"), vmem_limit_bytes=48 * MIB),
        name="norm_in_proj",
    )(x, g, w)


def _proj_ffn_kernel(x_ref, mix_ref, wo_ref, g_ref, wup_ref, wdn_ref, gf_ref, o_ref, *, d_ff, tf, final):
    xnew = x_ref[...] + _dot(mix_ref[...], wo_ref[...])
    h2 = (_rms(xnew) * g_ref[...]).astype(BF16)
    o_ref[...] = xnew
    for f in range(d_ff // tf):
        gate = _dot(h2, wup_ref[:, f * tf:(f + 1) * tf])
        val = _dot(h2, wup_ref[:, d_ff + f * tf:d_ff + (f + 1) * tf])
        act = (_silu(gate) * val).astype(BF16)
        o_ref[...] += _dot(act, wdn_ref[f * tf:(f + 1) * tf, :])
    if final:
        o_ref[...] = _rms(o_ref[...]) * gf_ref[...]


def _proj_ffn(x, mix, wo, g, wup, wdn, gf, final, tm=512, tf=256):
    n, d = x.shape
    d_ff = wdn.shape[0]
    const = lambda shape: pl.BlockSpec(shape, lambda i: (0, 0), pipeline_mode=pl.Buffered(1))
    return pl.pallas_call(
        functools.partial(_proj_ffn_kernel, d_ff=d_ff, tf=tf, final=final),
        out_shape=jax.ShapeDtypeStruct((n, d), F32),
        grid=(n // tm,),
        in_specs=[
            pl.BlockSpec((tm, d), lambda i: (i, 0)),
            pl.BlockSpec((tm, d), lambda i: (i, 0)),
            const((d, d)),
            const((1, d)),
            const((d, 2 * d_ff)),
            const((d_ff, d)),
            const((1, d)),
        ],
        out_specs=pl.BlockSpec((tm, d), lambda i: (i, 0)),
        compiler_params=pltpu.CompilerParams(
            dimension_semantics=("parallel",), vmem_limit_bytes=52 * MIB),
        name="out_proj_ffn",
    )(x, mix, wo, g, wup, wdn, gf)


def _lower_masks(widths):
    row = lax.broadcasted_iota(jnp.int32, (GROUP, LANES), 0)
    return [(row % (2 * w)) >= w for w in widths]


def _even_mixer_kernel(z_ref, s_in, h_in, c_in, m3_ref, pm_ref, lb_ref, hnorm_ref, cw_ref, cb_ref, wa_ref,
                       ba_ref, wx_ref, bx_ref, la_ref,
                       mix_ref, s_out, h_out, c_out,
                       mixf, e_s, xpad,
                       *, layer_j, nseq, tb, seq, L, widths, pos0, heads, width):
    ti = pl.program_id(1)
    R = nseq * tb
    dk = width // heads

    @pl.when(ti == 0)
    def _():
        s_out[...] = s_in[...]
        h_out[...] = h_in[...]
        c_out[...] = c_in[...]

    hl = lb_ref[...]
    e = jnp.exp(hl - jnp.max(hl, axis=0, keepdims=True))
    sm = e / jnp.sum(e, axis=0, keepdims=True)
    lb = jnp.zeros((1, width), F32)
    for i in range(1, layer_j + 1):
        lb = lb + sm[i:i + 1, :]
    log_lb = jnp.log(lb)
    lowers = _lower_masks(widths)

    def tile_body(tidx, carry):
        r0 = pl.multiple_of(tidx * GROUP, GROUP)
        rows = pl.ds(r0, GROUP)
        seq_ids = [(r0 + L * j) // tb for j in range(GROUP // L)]
        x = z_ref[rows, width:2 * width]
        y = log_lb - x
        logf = _log_sigmoid(x) + jnp.maximum(y, 0.0) + _log1p_exp_neg_abs(y)
        key = 1.0 - jnp.exp(logf)
        qsc = z_ref[rows, 0:width] * (dk ** -0.5)
        e_s[...] = jnp.exp(_level_sums(m3_ref[...], logf))
        for h in range(heads):
            lanes = slice(dk * h, dk * (h + 1))
            v = z_ref[rows, 2 * width + dk * h:2 * width + dk * (h + 1)]
            outs = _gla_tile(qsc[:, lanes], key[:, lanes], [v], [None], e_s, lanes, pm_ref, lowers, L,
                             s_out, seq_ids, h)
            gate = z_ref[rows, 3 * width + dk * h:3 * width + dk * (h + 1)]
            mixf[rows, lanes] = _rms(outs[0]) * hnorm_ref[:, lanes] * _silu(gate)
        return carry

    lax.fori_loop(0, R // GROUP, tile_body, 0)

    xoff, yoff = 4 * width, 5 * width
    xpad[:, 8 - (CONV_K - 1):8, :] = c_out[...]
    xpad[:, 8:, :] = z_ref[:, xoff:xoff + width].reshape(nseq, tb, width)
    xc = cb_ref[...].reshape(1, 1, width) + jnp.zeros((nseq, tb, width), F32)
    for j in range(CONV_K):
        lo = 8 - (CONV_K - 1) + j
        xc = xc + xpad[:, lo:lo + tb, :] * cw_ref[j:j + 1, :].reshape(1, 1, width)
    c_out[...] = xpad[:, tb + 8 - (CONV_K - 1):tb + 8, :]
    xc = xc.reshape(R, width)
    xcb = xc.astype(BF16)
    rg = jax.nn.sigmoid(_dot(xcb, wa_ref[...]) + ba_ref[...])
    ig = jax.nn.sigmoid(_dot(xcb, wx_ref[...]) + bx_ref[...])
    log_a = (-LRU_C) * rg * _softplus(-la_ref[...])
    a = jnp.exp(log_a)
    mult = jnp.sqrt(_neg_expm1(2.0 * log_a))
    row_in_seq = lax.broadcasted_iota(jnp.int32, (R, width), 0) % tb
    if pos0 <= 0 < pos0 + seq:
        mult = jnp.where(row_in_seq + ti * tb + pos0 == 0, 1.0, mult)
    hh = mult * ig * xc
    aa = a
    shift = 1
    while shift < tb:
        m = row_in_seq >= shift
        hh_new = jnp.where(m, aa * pltpu.roll(hh, shift, 0) + hh, hh)
        aa = jnp.where(m, aa * pltpu.roll(aa, shift, 0), aa)
        hh = hh_new
        shift *= 2
    hseq = hh.reshape(nseq, tb, width) + aa.reshape(nseq, tb, width) * h_out[...]
    h_out[...] = hseq[:, tb - 1:tb, :]
    mixf[:, width:2 * width] = hseq.reshape(R, width) * _gelu_tanh(z_ref[:, yoff:yoff + width])
    mix_ref[...] = mixf[...].astype(BF16)


def _even_mixer(z, s_all, h_all, c_all, layer_j, batch, seq, nseq, tb, L, pos0, params):
    (hgrn_lb, hgrn_norm, conv_w, conv_b, wa, ba, wx, bx, lru_a) = params
    n, e = z.shape
    heads, dk = s_all.shape[2], s_all.shape[3]
    width = heads * dk
    R = nseq * tb
    nt = seq // tb
    m3, pm, widths = _level_tables(L)
    const = lambda a: pl.BlockSpec(a.shape, lambda b, t: (0,) * a.ndim)
    kern = functools.partial(_even_mixer_kernel, layer_j=layer_j, nseq=nseq, tb=tb, seq=seq, L=L, widths=widths,
                             pos0=pos0, heads=heads, width=width)
    sblk = (None, nseq, heads, dk, dk)
    hblk = (None, nseq, 1, width)
    cblk = (None, nseq, CONV_K - 1, width)
    return pl.pallas_call(
        kern,
        out_shape=(
            jax.ShapeDtypeStruct((n, 2 * width), BF16),
            jax.ShapeDtypeStruct((batch, heads, dk, dk), F32),
            jax.ShapeDtypeStruct((batch, 1, width), F32),
            jax.ShapeDtypeStruct((batch, CONV_K - 1, width), F32),
        ),
        grid=(batch // nseq, nt),
        in_specs=[
            pl.BlockSpec((R, e), lambda b, t: (b * nt + t, 0)),
            pl.BlockSpec(sblk, lambda b, t: (layer_j, b, 0, 0, 0)),
            pl.BlockSpec(hblk, lambda b, t: (layer_j, b, 0, 0)),
            pl.BlockSpec(cblk, lambda b, t: (layer_j, b, 0, 0)),
            const(m3), const(pm),
            const(hgrn_lb), const(hgrn_norm), const(conv_w), const(conv_b),
            const(wa), const(ba), const(wx), const(bx), const(lru_a),
        ],
        out_specs=(
            pl.BlockSpec((R, 2 * width), lambda b, t: (b * nt + t, 0)),
            pl.BlockSpec(sblk[1:], lambda b, t: (b, 0, 0, 0)),
            pl.BlockSpec(hblk[1:], lambda b, t: (b, 0, 0)),
            pl.BlockSpec(cblk[1:], lambda b, t: (b, 0, 0)),
        ),
        scratch_shapes=[
            pltpu.VMEM((R, 2 * width), F32),
            pltpu.VMEM((m3.shape[0], width), F32),
            pltpu.VMEM((nseq, tb + 8, width), F32),
        ],
        compiler_params=pltpu.CompilerParams(
            dimension_semantics=("parallel", "arbitrary"), vmem_limit_bytes=48 * MIB),
        name="even_mixer",
    )(z, s_all, h_all, c_all, m3, pm, hgrn_lb, hgrn_norm, conv_w, conv_b, wa, ba, wx, bx, lru_a)


def _odd_mixer_kernel(z_ref, c_in, n_in, m_in, g_in, m3_ref, pm_ref, bias_ref, mnorm_ref, wa2_ref, gba_ref,
                      gnorm_ref,
                      mix_ref, c_out, n_out, m_out, g_out,
                      mixf, e_s,
                      *, nseq, tb, L, widths, heads, dh, gla_pairs):
    ti = pl.program_id(1)
    R = nseq * tb
    cw = heads * dh
    gk = gla_pairs * LANES
    q_off, k_off, v_off, o_off = 0, cw, 2 * cw, 3 * cw
    gq_off = 4 * cw
    gk_off = gq_off + gk
    gv_off = gk_off + gk
    gg_off = gv_off + cw
    small_off = gg_off + cw
    gla_dk = gk // heads
    nlev = len(widths)

    @pl.when(ti == 0)
    def _():
        c_out[...] = c_in[...]
        n_out[...] = n_in[...]
        m_out[...] = m_in[...]
        g_out[...] = g_in[...]

    lane = lax.broadcasted_iota(jnp.int32, (1, LANES), 1)
    is_f = (lane >= heads) & (lane < 2 * heads)
    tril = (lax.broadcasted_iota(jnp.int32, (L, L), 0) >= lax.broadcasted_iota(jnp.int32, (L, L), 1))
    half = LANES // 2
    gla_masks = [lane < half, lane >= half]
    lowers = _lower_masks(widths)

    def tile_body(tidx, carry):
        r0 = pl.multiple_of(tidx * GROUP, GROUP)
        rows = pl.ds(r0, GROUP)
        seq_ids = [(r0 + L * j) // tb for j in range(GROUP // L)]
        small = z_ref[rows, small_off:small_off + LANES]
        g0 = small + bias_ref[...]
        pre = _dot(small.astype(BF16), wa2_ref[...]) + gba_ref[...]
        la = _log_sigmoid(pre) * (1.0 / GLA_TAU)
        sums = _level_sums(m3_ref[...], jnp.concatenate([la, _log_sigmoid(g0)], axis=-1))
        e_s[...] = jnp.exp(sums[:, 0:gk])
        gates = jnp.where(is_f, sums[GROUP * nlev:GROUP * (nlev + 1), gk:gk + LANES], g0)

        gate_t = [jnp.transpose(gates[L * j:L * (j + 1)]) for j in range(GROUP // L)]
        for h in range(heads):
            lanes = slice(dh * h, dh * (h + 1))
            q_t = z_ref[rows, q_off + dh * h:q_off + dh * (h + 1)]
            k_t = z_ref[rows, k_off + dh * h:k_off + dh * (h + 1)] * (dh ** -0.5)
            v_t = z_ref[rows, v_off + dh * h:v_off + dh * (h + 1)]
            cells = []
            for j in range(GROUP // L):
                sl = slice(L * j, L * (j + 1))
                si = seq_ids[j]
                gt, gtt = gates[sl], gate_t[j]
                i_col, b_col = gt[:, h:h + 1], gt[:, heads + h:heads + h + 1]
                i_row, b_row = gtt[h:h + 1, :], gtt[heads + h:heads + h + 1, :]
                m_vec = m_out[si]
                m_prev = m_vec[:, h:h + 1]
                q, k, v = q_t[sl], k_t[sl], v_t[sl]
                qb, kb, vb = q.astype(BF16), k.astype(BF16), v.astype(BF16)
                cmat = c_out[si, h]
                nrow = n_out[si, :, lanes]
                dlog = jnp.where(tril, b_col - b_row + i_row, NEG_INF)
                inter = b_col + m_prev
                m_t = jnp.maximum(inter, jnp.max(dlog, axis=-1, keepdims=True))
                w_intra = jnp.exp(dlog - m_t)
                w_inter = jnp.exp(inter - m_t)
                qk = _dot_nt(qb, kb) * w_intra
                num = _dot(qk.astype(BF16), vb) + w_inter * _dot(qb, cmat.astype(BF16))
                den = jnp.sum(qk, axis=-1, keepdims=True) + w_inter * jnp.sum(q * nrow, axis=-1, keepdims=True)
                cells.append(num / jnp.maximum(jnp.abs(den), jnp.exp(-m_t)))
                b_last = b_col[L - 1:L, :]
                m_new = jnp.maximum(b_last + m_prev, jnp.max(b_last - b_row + i_row, axis=-1, keepdims=True))
                w_s = jnp.exp(b_last - b_col + i_col - m_new)
                decay = jnp.exp(b_last + m_prev - m_new)
                kw = k * w_s
                c_out[si, h] = decay * cmat + _dot_tn(kw.astype(BF16), vb)
                n_out[si, :, lanes] = decay * nrow + jnp.sum(kw, axis=0, keepdims=True)
                m_out[si] = jnp.where(lane == h, m_new, m_vec)
            hcell = cells[0] if len(cells) == 1 else jnp.concatenate(cells, axis=0)
            ogate = z_ref[rows, o_off + dh * h:o_off + dh * (h + 1)]
            mixf[rows, lanes] = _rms(hcell) * mnorm_ref[:, lanes] * jax.nn.sigmoid(ogate)

        for p in range(gla_pairs):
            lanes = slice(LANES * p, LANES * (p + 1))
            hs = [2 * p, 2 * p + 1]
            vs = [z_ref[rows, gv_off + dh * h:gv_off + dh * (h + 1)] for h in hs]
            qsc = z_ref[rows, gq_off + LANES * p:gq_off + LANES * (p + 1)] * (gla_dk ** -0.5)
            kk = z_ref[rows, gk_off + LANES * p:gk_off + LANES * (p + 1)]
            outs = _gla_tile(qsc, kk, vs, gla_masks, e_s, lanes, pm_ref, lowers, L, g_out, seq_ids, p)
            for h, o in zip(hs, outs):
                vl = slice(dh * h, dh * (h + 1))
                gg = z_ref[rows, gg_off + dh * h:gg_off + dh * (h + 1)]
                mixf[rows, cw + dh * h:cw + dh * (h + 1)] = _rms(o) * gnorm_ref[:, vl] * _silu(gg)
        return carry

    lax.fori_loop(0, R // GROUP, tile_body, 0)
    mix_ref[...] = mixf[...].astype(BF16)


def _odd_mixer(z, c_all, n_all, m_all, g_all, layer_j, batch, seq, nseq, tb, L, params):
    (bias_small, mnorm, wa2_pad, gla_ba, gnorm) = params
    n, e = z.shape
    heads, dh = c_all.shape[2], c_all.shape[3]
    cw = heads * dh
    gla_pairs = g_all.shape[2]
    gk = gla_pairs * LANES
    R = nseq * tb
    nt = seq // tb
    m3, pm, widths = _level_tables(L)
    const = lambda a: pl.BlockSpec(a.shape, lambda b, t: (0,) * a.ndim)
    kern = functools.partial(_odd_mixer_kernel, nseq=nseq, tb=tb, L=L, widths=widths, heads=heads, dh=dh,
                             gla_pairs=gla_pairs)
    cblk = (None, nseq, heads, dh, dh)
    nblk = (None, nseq, 1, cw)
    mblk = (None, nseq, 1, LANES)
    gblk = (None, nseq, gla_pairs, LANES, dh)
    return pl.pallas_call(
        kern,
        out_shape=(
            jax.ShapeDtypeStruct((n, 2 * cw), BF16),
            jax.ShapeDtypeStruct((batch, heads, dh, dh), F32),
            jax.ShapeDtypeStruct((batch, 1, cw), F32),
            jax.ShapeDtypeStruct((batch, 1, LANES), F32),
            jax.ShapeDtypeStruct((batch, gla_pairs, LANES, dh), F32),
        ),
        grid=(batch // nseq, nt),
        in_specs=[
            pl.BlockSpec((R, e), lambda b, t: (b * nt + t, 0)),
            pl.BlockSpec(cblk, lambda b, t: (layer_j, b, 0, 0, 0)),
            pl.BlockSpec(nblk, lambda b, t: (layer_j, b, 0, 0)),
            pl.BlockSpec(mblk, lambda b, t: (layer_j, b, 0, 0)),
            pl.BlockSpec(gblk, lambda b, t: (layer_j, b, 0, 0, 0)),
            const(m3), const(pm),
            const(bias_small), const(mnorm), const(wa2_pad), const(gla_ba), const(gnorm),
        ],
        out_specs=(
            pl.BlockSpec((R, 2 * cw), lambda b, t: (b * nt + t, 0)),
            pl.BlockSpec(cblk[1:], lambda b, t: (b, 0, 0, 0)),
            pl.BlockSpec(nblk[1:], lambda b, t: (b, 0, 0)),
            pl.BlockSpec(mblk[1:], lambda b, t: (b, 0, 0)),
            pl.BlockSpec(gblk[1:], lambda b, t: (b, 0, 0, 0)),
        ),
        scratch_shapes=[
            pltpu.VMEM((R, 2 * cw), F32),
            pltpu.VMEM((m3.shape[0], gk), F32),
        ],
        compiler_params=pltpu.CompilerParams(
            dimension_semantics=("parallel", "arbitrary"), vmem_limit_bytes=48 * MIB),
        name="odd_mixer",
    )(z, c_all, n_all, m_all, g_all, m3, pm, bias_small, mnorm, wa2_pad, gla_ba, gnorm)


def _block_diag(w):
    nb, ci, di = w.shape
    eye = jnp.eye(nb, dtype=w.dtype)
    return (eye[:, None, :, None] * w[:, :, None, :]).reshape(nb * ci, nb * di)


def kernel(x_prompt, x_sample, state_hgrn, state_lru_h, state_lru_conv, state_mlstm_C, state_mlstm_n, state_mlstm_m, state_gla, norm_mix, norm_ffn, norm_final, w_in_even, hgrn_lb, hgrn_norm, lru_conv_w, lru_conv_b, lru_wa, lru_ba, lru_wx, lru_bx, lru_a, w_out_even, w_in_odd, mlstm_bi, mlstm_bf, mlstm_norm, gla_wa2, gla_ba, gla_norm, w_out_odd, ffn_w_up, ffn_w_down):
    batch, seq, d_model = x_prompt.shape
    dec_batch, dec_seq, _ = x_sample.shape
    depth = norm_mix.shape[0]
    n_even, _, a_heads, a_dk, _ = state_hgrn.shape
    n_odd, _, c_heads, c_dh, _ = state_mlstm_C.shape
    b_width = state_lru_h.shape[-1]
    c_width = c_heads * c_dh
    d_heads, d_dk, d_dv = state_gla.shape[2:]
    gk = d_heads * d_dk
    d_width = d_heads * d_dv
    d_rank = gla_wa2.shape[1]
    gla_pairs = gk // LANES
    assert a_heads * a_dk == b_width and GROUP % dec_seq == 0 and seq % (4 * GROUP) == 0

    w_even = w_in_even.astype(BF16)
    main = 4 * c_width
    small_w = 2 * c_heads
    pad = (-(main + 2 * gk + 2 * d_width + LANES)) % (2 * LANES)
    w_odd = jnp.concatenate([
        w_in_odd[:, :, :main],
        w_in_odd[:, :, main + small_w:main + small_w + 2 * gk + 2 * d_width],
        w_in_odd[:, :, main:main + small_w],
        w_in_odd[:, :, main + small_w + 2 * gk + 2 * d_width:],
        jnp.zeros((n_odd, d_model, LANES - small_w - d_rank + pad), F32),
    ], axis=-1).astype(BF16)
    w_out_e = w_out_even.astype(BF16)
    w_out_o = w_out_odd.astype(BF16)
    w_up = ffn_w_up.astype(BF16)
    w_dn = ffn_w_down.astype(BF16)
    wa_dense = jax.vmap(_block_diag)(lru_wa).astype(BF16)
    wx_dense = jax.vmap(_block_diag)(lru_wx).astype(BF16)
    bias_small = jnp.concatenate(
        [mlstm_bi, mlstm_bf, jnp.zeros((n_odd, LANES - small_w), F32)], axis=-1)[:, None, :]
    wa2_pad = jnp.concatenate([
        jnp.zeros((n_odd, small_w, gk), F32), gla_wa2,
        jnp.zeros((n_odd, LANES - small_w - d_rank, gk), F32)], axis=1).astype(BF16)

    row = lambda a: a[None, :]
    zeros = lambda *s: jnp.zeros(s, F32)
    groups = [
        dict(batch=batch, seq=seq, nseq=1, tb=4 * GROUP, L=math.gcd(seq, CHUNK), pos0=0,
             x=x_prompt.reshape(batch * seq, d_model),
             hgrn=zeros(n_even, batch, a_heads, a_dk, a_dk),
             lru_h=zeros(n_even, batch, 1, b_width),
             lru_conv=zeros(n_even, batch, CONV_K - 1, b_width),
             mC=zeros(n_odd, batch, c_heads, c_dh, c_dh),
             mn=zeros(n_odd, batch, 1, c_width),
             mm=zeros(n_odd, batch, 1, LANES),
             gla=zeros(n_odd, batch, gla_pairs, LANES, d_dv)),
        dict(batch=dec_batch, seq=dec_seq, nseq=GROUP // dec_seq, tb=dec_seq, L=math.gcd(dec_seq, CHUNK),
             pos0=PAST_LEN,
             x=x_sample.reshape(dec_batch * dec_seq, d_model),
             hgrn=state_hgrn,
             lru_h=state_lru_h[:, :, None, :],
             lru_conv=state_lru_conv,
             mC=state_mlstm_C,
             mn=state_mlstm_n.reshape(n_odd, dec_batch, 1, c_width),
             mm=jnp.pad(state_mlstm_m, ((0, 0), (0, 0), (0, LANES - c_heads)))[:, :, None, :],
             gla=state_gla.reshape(n_odd, dec_batch, gla_pairs, LANES, d_dv)),
    ]

    results = []
    for grp in groups:
        x = grp["x"]
        n = x.shape[0]
        tm = min(1024, n)
        outs = dict(hgrn=[], lru_h=[], lru_conv=[], mC=[], mn=[], mm=[], gla=[])
        for l in range(depth):
            j = l // 2
            if l % 2 == 0:
                z = _norm_matmul(x, row(norm_mix[l]), w_even[j], tm, 1024)
                params = (hgrn_lb, row(hgrn_norm[j]), lru_conv_w[j], row(lru_conv_b[j]), wa_dense[j],
                          row(lru_ba[j]), wx_dense[j], row(lru_bx[j]), row(lru_a[j]))
                mix, s, h, cb = _even_mixer(z, grp["hgrn"], grp["lru_h"], grp["lru_conv"], j, grp["batch"],
                                            grp["seq"], grp["nseq"], grp["tb"], grp["L"], grp["pos0"], params)
                outs["hgrn"].append(s)
                outs["lru_h"].append(h[:, 0, :])
                outs["lru_conv"].append(cb)
                w_out = w_out_e[j]
            else:
                z = _norm_matmul(x, row(norm_mix[l]), w_odd[j], tm, w_odd.shape[-1] // 3)
                params = (bias_small[j], row(mlstm_norm[j]), wa2_pad[j], row(gla_ba[j]), row(gla_norm[j]))
                mix, cm, nn, mm, sg = _odd_mixer(z, grp["mC"], grp["mn"], grp["mm"], grp["gla"], j, grp["batch"],
                                                 grp["seq"], grp["nseq"], grp["tb"], grp["L"], params)
                outs["mC"].append(cm)
                outs["mn"].append(nn.reshape(grp["batch"], c_heads, c_dh))
                outs["mm"].append(mm[:, 0, :c_heads])
                outs["gla"].append(sg.reshape(grp["batch"], d_heads, d_dk, d_dv))
                w_out = w_out_o[j]
            x = _proj_ffn(x, mix, w_out, row(norm_ffn[l]), w_up[l], w_dn[l], row(norm_final),
                          final=(l == depth - 1))
        results.append((x.reshape(grp["batch"], grp["seq"], d_model),
                        {k: jnp.stack(v) for k, v in outs.items()}))

    (y_p, st_p), (y_s, st_s) = results
    return (y_p, y_s, st_p["hgrn"], st_s["hgrn"], st_p["lru_h"], st_s["lru_h"], st_p["lru_conv"], st_s["lru_conv"],
            st_p["mC"], st_s["mC"], st_p["mn"], st_s["mn"], st_p["mm"], st_s["mm"], st_p["gla"], st_s["gla"])
```

```python
import functools
import math

import jax
import jax.numpy as jnp
import numpy as np
from jax import lax
from jax.experimental import pallas as pl
from jax.experimental.pallas import tpu as pltpu

F32 = jnp.float32
BF16 = jnp.bfloat16
EPS = 1e-6
LANES = 128
GROUP = 64
CHUNK = 64
CONV_K = 4
LRU_C = 8.0
GLA_TAU = 16.0
PAST_LEN = 16384
MIB = 1024 * 1024
NEG_INF = float("-inf")


def _log1p_exp_neg_abs(x):
    return jnp.log(1.0 + jnp.exp(-jnp.abs(x)))


def _log_sigmoid(x):
    return jnp.minimum(x, 0.0) - _log1p_exp_neg_abs(x)


def _softplus(x):
    return jnp.maximum(x, 0.0) + _log1p_exp_neg_abs(x)


def _neg_expm1(y):
    return -jnp.tanh(0.5 * y) * (jnp.exp(y) + 1.0)


def _silu(x):
    return x * jax.nn.sigmoid(x)


def _gelu_tanh(x):
    return 0.5 * x * (1.0 + jnp.tanh(math.sqrt(2.0 / math.pi) * (x + 0.044715 * (x * x * x))))


def _rms(x):
    return x * lax.rsqrt(jnp.mean(x * x, axis=-1, keepdims=True) + EPS)


def _dot(a, b):
    return jnp.dot(a, b, preferred_element_type=F32)


def _dot_nt(a, b):
    return lax.dot_general(a, b, (((1,), (1,)), ((), ())), preferred_element_type=F32)


def _dot_tn(a, b):
    return lax.dot_general(a, b, (((0,), (0,)), ((), ())), preferred_element_type=F32)


def _layer_spec(shape, layer, grid_rank, **kw):
    zeros = (0,) * len(shape)
    if grid_rank == 1:
        return pl.BlockSpec((None,) + tuple(shape), lambda i: (layer,) + zeros, **kw)
    return pl.BlockSpec((None,) + tuple(shape), lambda i, j: (layer,) + zeros, **kw)


def _level_tables(L):
    t = np.arange(GROUP)[:, None]
    s = np.arange(GROUP)[None, :]
    widths = []
    w = L // 2
    while w >= 1:
        widths.append(w)
        w //= 2
    mats, pmasks = [], []
    for w in widths:
        same = (t // (2 * w)) == (s // (2 * w))
        low_t, low_s = (t % (2 * w)) >= w, (s % (2 * w)) >= w
        mats.append((same & low_t & low_s & (s <= t)) | (same & ~low_t & ~low_s & (s > t)))
        pmasks.append(same & low_t & ~low_s)
    seg = (t // L) == (s // L)
    mats.append(seg & (s <= t))
    mats.append(seg & (s > t))
    pmasks.append(t == s)
    m_all = np.concatenate(mats, axis=0).astype(np.float32)
    m3 = np.concatenate([m_all, m_all, m_all], axis=1)
    pm = np.stack(pmasks).astype(np.float32)
    return jnp.asarray(m3, BF16), jnp.asarray(np.concatenate([pm, pm], axis=-1)), widths


def _level_sums(m3, g):
    g_hi = g.astype(BF16)
    r1 = g - g_hi.astype(F32)
    g_mid = r1.astype(BF16)
    g_lo = (r1 - g_mid.astype(F32)).astype(BF16)
    return _dot(m3, jnp.concatenate([g_hi, g_mid, g_lo], axis=0))


def _lower_masks(widths):
    row = lax.broadcasted_iota(jnp.int32, (GROUP, LANES), 0)
    return [(row % (2 * w)) >= w for w in widths]


def _gla_pair_tile(qs, ks, es, vs, masks, pm_ref, lowers, L, states, put_state):
    nlev = len(lowers)
    own = len(qs) == 2
    zero = jnp.zeros((GROUP, LANES), F32)
    lo_half = lax.broadcasted_iota(jnp.int32, (1, LANES), 1) < GROUP

    def blk(e, i):
        return e[GROUP * i:GROUP * (i + 1), :]

    def side_by_side(a, b):
        return jnp.concatenate([jnp.concatenate([a, zero], axis=1), jnp.concatenate([zero, b], axis=1)], axis=0)

    scores = None
    for lv in range(nlev):
        ys = [jnp.where(lowers[lv], q, k) * blk(e, lv) for q, k, e in zip(qs, ks, es)]
        if own:
            lhs = jnp.concatenate(ys, axis=1)
            rhs = side_by_side(ys[0], ys[1])
        else:
            lhs = ys[0]
            rhs = jnp.concatenate([jnp.where(m, ys[0], 0.0) for m in masks], axis=0)
        part = _dot_nt(lhs.astype(BF16), rhs.astype(BF16)) * pm_ref[lv]
        scores = part if scores is None else scores + part
    if own:
        d = [jnp.sum(q * k, axis=-1, keepdims=True) for q, k in zip(qs, ks)]
    else:
        qk = qs[0] * ks[0]
        d = [jnp.sum(jnp.where(m, qk, 0.0), axis=-1, keepdims=True) for m in masks]
    diag = pm_ref[nlev] * jnp.where(lo_half, d[0], d[1])
    p = diag if scores is None else scores + diag
    intra = _dot(p.astype(BF16), side_by_side(vs[0], vs[1]).astype(BF16))
    vbs = [v.astype(BF16) for v in vs]
    ebs = [blk(e, nlev) for e in es]
    qes = [q * eb for q, eb in zip(qs, ebs)]
    kds = [k * blk(e, nlev + 1) for k, e in zip(ks, es)]
    pieces = [[], []]
    for j in range(GROUP // L):
        sl = slice(L * j, L * (j + 1))
        sbs = [s.astype(BF16) for s in states[j]]
        upds = [None] * len(qs)
        for u in range(2):
            i = u if own else 0
            qe = qes[i][sl] if own else jnp.where(masks[u], qes[0][sl], 0.0)
            kd = kds[i][sl] if own else jnp.where(masks[u], kds[0][sl], 0.0)
            pieces[u].append(intra[sl, LANES * u:LANES * (u + 1)] + _dot(qe.astype(BF16), sbs[i]))
            t = _dot_tn(kd.astype(BF16), vbs[u][sl])
            upds[i] = t if upds[i] is None else upds[i] + t
        for i, upd in enumerate(upds):
            last = ebs[i][L * j + L - 1:L * j + L, :]
            decay_col = jnp.transpose(jnp.broadcast_to(last, (8, LANES)))[:, 0:1]
            put_state(j, i, decay_col * states[j][i] + upd)
    return [p[0] if len(p) == 1 else jnp.concatenate(p, axis=0) for p in pieces]


def _norm_matmul_kernel(x_ref, g_ref, w_ref, o_ref, xn_ref):
    @pl.when(pl.program_id(1) == 0)
    def _():
        xn_ref[...] = (_rms(x_ref[...]) * g_ref[...]).astype(BF16)

    o_ref[...] = _dot(xn_ref[...], w_ref[...])


def _norm_matmul(x, g, w, layer, wlayer, tm, tn):
    n, d = x.shape
    e = w.shape[-1]
    return pl.pallas_call(
        _norm_matmul_kernel,
        out_shape=jax.ShapeDtypeStruct((n, e), F32),
        grid=(n // tm, e // tn),
        in_specs=[
            pl.BlockSpec((tm, d), lambda i, j: (i, 0)),
            _layer_spec((1, d), layer, 2),
            pl.BlockSpec((None, d, tn), lambda i, j: (wlayer, 0, j)),
        ],
        out_specs=pl.BlockSpec((tm, tn), lambda i, j: (i, j)),
        scratch_shapes=[pltpu.VMEM((tm, d), BF16)],
        compiler_params=pltpu.CompilerParams(
            dimension_semantics=("parallel", "arbitrary"), vmem_limit_bytes=48 * MIB),
        name="norm_in_proj",
    )(x, g, w)


def _proj_ffn_kernel(x_ref, mix_ref, wo_ref, g_ref, wup_ref, wdn_ref, gf_ref, o_ref, *, d_ff, tf, final):
    xnew = x_ref[...] + _dot(mix_ref[...], wo_ref[...])
    h2 = (_rms(xnew) * g_ref[...]).astype(BF16)
    o_ref[...] = xnew
    for f in range(d_ff // tf):
        gate = _dot(h2, wup_ref[:, f * tf:(f + 1) * tf])
        val = _dot(h2, wup_ref[:, d_ff + f * tf:d_ff + (f + 1) * tf])
        act = (_silu(gate) * val).astype(BF16)
        o_ref[...] += _dot(act, wdn_ref[f * tf:(f + 1) * tf, :])
    if final:
        o_ref[...] = _rms(o_ref[...]) * gf_ref[...]


def _proj_ffn(x, mix, wo, g, wup, wdn, gf, layer, wolayer, final, tm=512, tf=256):
    n, d = x.shape
    d_ff = wdn.shape[-2]
    once = dict(pipeline_mode=pl.Buffered(1))
    return pl.pallas_call(
        functools.partial(_proj_ffn_kernel, d_ff=d_ff, tf=tf, final=final),
        out_shape=jax.ShapeDtypeStruct((n, d), F32),
        grid=(n // tm,),
        in_specs=[
            pl.BlockSpec((tm, d), lambda i: (i, 0)),
            pl.BlockSpec((tm, d), lambda i: (i, 0)),
            _layer_spec((d, d), wolayer, 1, **once),
            _layer_spec((1, d), layer, 1, **once),
            _layer_spec((d, 2 * d_ff), layer, 1, **once),
            _layer_spec((d_ff, d), layer, 1, **once),
            pl.BlockSpec((1, d), lambda i: (0, 0), **once),
        ],
        out_specs=pl.BlockSpec((tm, d), lambda i: (i, 0)),
        compiler_params=pltpu.CompilerParams(
            dimension_semantics=("parallel",), vmem_limit_bytes=52 * MIB),
        name="out_proj_ffn",
    )(x, mix, wo, g, wup, wdn, gf)


def _even_mixer_kernel(*refs, layer_j, nseq, tb, seq, L, widths, pos0, heads, width, aliased):
    (z_ref, s_in, h_in, c_in, m3_ref, pm_ref, lb_ref, hnorm_ref, cw_ref, cb_ref, wa_ref,
     ba_ref, wx_ref, bx_ref, la_ref) = refs[:15]
    mix_ref, s_out, h_out, c_out, mixf, xpad = refs[15 + aliased:]
    ti = pl.program_id(1)
    R = nseq * tb
    dk = width // heads

    @pl.when(ti == 0)
    def _():
        s_out[...] = s_in[...]
        h_out[...] = h_in[...]
        c_out[...] = c_in[...]

    hl = lb_ref[...]
    e = jnp.exp(hl - jnp.max(hl, axis=0, keepdims=True))
    sm = e / jnp.sum(e, axis=0, keepdims=True)
    lb = jnp.zeros((1, width), F32)
    for i in range(1, layer_j + 1):
        lb = lb + sm[i:i + 1, :]
    log_lb = jnp.log(lb)
    lowers = _lower_masks(widths)

    for tidx in range(R // GROUP):
        r0 = tidx * GROUP
        rows = slice(r0, r0 + GROUP)
        seq_ids = [(r0 + L * j) // tb for j in range(GROUP // L)]
        x = z_ref[rows, width:2 * width]
        y = log_lb - x
        logf = _log_sigmoid(x) + jnp.maximum(y, 0.0) + _log1p_exp_neg_abs(y)
        key = 1.0 - jnp.exp(logf)
        qsc = z_ref[rows, 0:width] * (dk ** -0.5)
        e_all = jnp.exp(_level_sums(m3_ref[...], logf))
        for p in range(heads // 2):
            hs = [2 * p, 2 * p + 1]
            ln = [slice(dk * h, dk * (h + 1)) for h in hs]
            vs = [z_ref[rows, 2 * width + dk * h:2 * width + dk * (h + 1)] for h in hs]
            states = [[s_out[si, h] for h in hs] for si in seq_ids]

            def put_state(j, i, new, hs=hs, seq_ids=seq_ids):
                s_out[seq_ids[j], hs[i]] = new

            outs = _gla_pair_tile([qsc[:, l] for l in ln], [key[:, l] for l in ln], [e_all[:, l] for l in ln],
                                  vs, None, pm_ref, lowers, L, states, put_state)
            for h, l, o in zip(hs, ln, outs):
                gate = z_ref[rows, 3 * width + dk * h:3 * width + dk * (h + 1)]
                mixf[rows, l] = _rms(o) * hnorm_ref[:, l] * _silu(gate)

    xoff, yoff = 4 * width, 5 * width
    xpad[:, 8 - (CONV_K - 1):8, :] = c_out[...]
    xpad[:, 8:, :] = z_ref[:, xoff:xoff + width].reshape(nseq, tb, width)
    xc = cb_ref[...].reshape(1, 1, width) + jnp.zeros((nseq, tb, width), F32)
    for j in range(CONV_K):
        lo = 8 - (CONV_K - 1) + j
        xc = xc + xpad[:, lo:lo + tb, :] * cw_ref[j:j + 1, :].reshape(1, 1, width)
    c_out[...] = xpad[:, tb + 8 - (CONV_K - 1):tb + 8, :]
    xc = xc.reshape(R, width)
    xcb = xc.astype(BF16)
    rg = jax.nn.sigmoid(_dot(xcb, wa_ref[...]) + ba_ref[...])
    ig = jax.nn.sigmoid(_dot(xcb, wx_ref[...]) + bx_ref[...])
    log_a = (-LRU_C) * rg * _softplus(-la_ref[...])
    a = jnp.exp(log_a)
    mult = jnp.sqrt(_neg_expm1(2.0 * log_a))
    row_in_seq = lax.broadcasted_iota(jnp.int32, (R, width), 0) % tb
    if pos0 <= 0 < pos0 + seq:
        mult = jnp.where(row_in_seq + ti * tb + pos0 == 0, 1.0, mult)
    hh = mult * ig * xc
    aa = a
    shift = 1
    while shift < tb:
        m = row_in_seq >= shift
        hh_new = jnp.where(m, aa * pltpu.roll(hh, shift, 0) + hh, hh)
        aa = jnp.where(m, aa * pltpu.roll(aa, shift, 0), aa)
        hh = hh_new
        shift *= 2
    hseq = hh.reshape(nseq, tb, width) + aa.reshape(nseq, tb, width) * h_out[...]
    h_out[...] = hseq[:, tb - 1:tb, :]
    mixf[:, width:2 * width] = hseq.reshape(R, width) * _gelu_tanh(z_ref[:, yoff:yoff + width])
    mix_ref[...] = mixf[...].astype(BF16)


def _even_mixer(z, s_all, h_all, c_all, s_prev, layer_j, batch, seq, nseq, tb, L, pos0, params):
    (hgrn_lb, hgrn_norm, conv_w, conv_b, wa, ba, wx, bx, lru_a) = params
    n, e = z.shape
    n_layers, _, heads, dk, _ = s_all.shape
    width = heads * dk
    R = nseq * tb
    nt = seq // tb
    m3, pm, widths = _level_tables(L)
    const = lambda a: pl.BlockSpec(a.shape, lambda b, t: (0,) * a.ndim)
    lay = lambda a: _layer_spec(a.shape[1:], layer_j, 2)
    aliased = s_prev is not None
    kern = functools.partial(_even_mixer_kernel, layer_j=layer_j, nseq=nseq, tb=tb, seq=seq, L=L, widths=widths,
                             pos0=pos0, heads=heads, width=width, aliased=int(aliased))
    sblk = (None, nseq, heads, dk, dk)
    hblk = (None, nseq, 1, width)
    cblk = (None, nseq, CONV_K - 1, width)
    inputs = [z, s_all, h_all, c_all, m3, pm, hgrn_lb, hgrn_norm, conv_w, conv_b, wa, ba, wx, bx, lru_a]
    in_specs = [
        pl.BlockSpec((R, e), lambda b, t: (b * nt + t, 0)),
        pl.BlockSpec(sblk, lambda b, t: (layer_j, b, 0, 0, 0)),
        pl.BlockSpec(hblk, lambda b, t: (layer_j, b, 0, 0)),
        pl.BlockSpec(cblk, lambda b, t: (layer_j, b, 0, 0)),
        const(m3), const(pm), const(hgrn_lb), lay(hgrn_norm), lay(conv_w), lay(conv_b),
        lay(wa), lay(ba), lay(wx), lay(bx), lay(lru_a),
    ]
    if aliased:
        inputs.append(s_prev)
        in_specs.append(pl.BlockSpec(memory_space=pl.ANY))
    return pl.pallas_call(
        kern,
        out_shape=(
            jax.ShapeDtypeStruct((n, 2 * width), BF16),
            jax.ShapeDtypeStruct((n_layers, batch, heads, dk, dk), F32),
            jax.ShapeDtypeStruct((batch, 1, width), F32),
            jax.ShapeDtypeStruct((batch, CONV_K - 1, width), F32),
        ),
        grid=(batch // nseq, nt),
        in_specs=in_specs,
        out_specs=(
            pl.BlockSpec((R, 2 * width), lambda b, t: (b * nt + t, 0)),
            pl.BlockSpec(sblk, lambda b, t: (layer_j, b, 0, 0, 0)),
            pl.BlockSpec(hblk[1:], lambda b, t: (b, 0, 0)),
            pl.BlockSpec(cblk[1:], lambda b, t: (b, 0, 0)),
        ),
        scratch_shapes=[
            pltpu.VMEM((R, 2 * width), F32),
            pltpu.VMEM((nseq, tb + 8, width), F32),
        ],
        input_output_aliases={len(inputs) - 1: 1} if aliased else {},
        compiler_params=pltpu.CompilerParams(
            dimension_semantics=("parallel", "arbitrary"), vmem_limit_bytes=48 * MIB),
        name="even_mixer",
    )(*inputs)


def _odd_mixer_kernel(*refs, nseq, tb, L, widths, heads, dh, gla_pairs, aliased):
    (z_ref, c_in, n_in, m_in, g_in, m3_ref, pm_ref, bias_ref, mnorm_ref, wa2_ref, gba_ref,
     gnorm_ref) = refs[:12]
    mix_ref, c_out, n_out, m_out, g_out, mixf = refs[12 + 2 * aliased:]
    ti = pl.program_id(1)
    R = nseq * tb
    cw = heads * dh
    gk = gla_pairs * LANES
    q_off, k_off, v_off, o_off = 0, cw, 2 * cw, 3 * cw
    gq_off = 4 * cw
    gk_off = gq_off + gk
    gv_off = gk_off + gk
    gg_off = gv_off + cw
    small_off = gg_off + cw
    gla_dk = gk // heads
    nlev = len(widths)

    @pl.when(ti == 0)
    def _():
        c_out[...] = c_in[...]
        n_out[...] = n_in[...]
        m_out[...] = m_in[...]
        g_out[...] = g_in[...]

    lane = lax.broadcasted_iota(jnp.int32, (1, LANES), 1)
    is_f = (lane >= heads) & (lane < 2 * heads)
    tril = (lax.broadcasted_iota(jnp.int32, (L, L), 0) >= lax.broadcasted_iota(jnp.int32, (L, L), 1))
    half = LANES // 2
    gla_masks = [lane < half, lane >= half]
    lowers = _lower_masks(widths)

    for tidx in range(R // GROUP):
        r0 = tidx * GROUP
        rows = slice(r0, r0 + GROUP)
        seq_ids = [(r0 + L * j) // tb for j in range(GROUP // L)]
        small = z_ref[rows, small_off:small_off + LANES]
        g0 = small + bias_ref[...]
        pre = _dot(small.astype(BF16), wa2_ref[...]) + gba_ref[...]
        la = _log_sigmoid(pre) * (1.0 / GLA_TAU)
        sums = _level_sums(m3_ref[...], jnp.concatenate([la, _log_sigmoid(g0)], axis=-1))
        e_all = jnp.exp(sums[:, 0:gk])
        gates = jnp.where(is_f, sums[GROUP * nlev:GROUP * (nlev + 1), gk:gk + LANES], g0)

        gate_t = [jnp.transpose(gates[L * j:L * (j + 1)]) for j in range(GROUP // L)]
        for h in range(heads):
            lanes = slice(dh * h, dh * (h + 1))
            q_t = z_ref[rows, q_off + dh * h:q_off + dh * (h + 1)]
            k_t = z_ref[rows, k_off + dh * h:k_off + dh * (h + 1)] * (dh ** -0.5)
            v_t = z_ref[rows, v_off + dh * h:v_off + dh * (h + 1)]
            cells = []
            for j in range(GROUP // L):
                sl = slice(L * j, L * (j + 1))
                si = seq_ids[j]
                gt, gtt = gates[sl], gate_t[j]
                i_col, b_col = gt[:, h:h + 1], gt[:, heads + h:heads + h + 1]
                i_row, b_row = gtt[h:h + 1, :], gtt[heads + h:heads + h + 1, :]
                m_vec = m_out[si]
                m_prev = m_vec[:, h:h + 1]
                q, k, v = q_t[sl], k_t[sl], v_t[sl]
                qb, kb, vb = q.astype(BF16), k.astype(BF16), v.astype(BF16)
                cmat = c_out[si, h]
                nrow = n_out[si, :, lanes]
                dlog = jnp.where(tril, b_col - b_row + i_row, NEG_INF)
                inter = b_col + m_prev
                m_t = jnp.maximum(inter, jnp.max(dlog, axis=-1, keepdims=True))
                w_intra = jnp.exp(dlog - m_t)
                w_inter = jnp.exp(inter - m_t)
                qk = _dot_nt(qb, kb) * w_intra
                num = _dot(qk.astype(BF16), vb) + w_inter * _dot(qb, cmat.astype(BF16))
                den = jnp.sum(qk, axis=-1, keepdims=True) + w_inter * jnp.sum(q * nrow, axis=-1, keepdims=True)
                cells.append(num / jnp.maximum(jnp.abs(den), jnp.exp(-m_t)))
                b_last = b_col[L - 1:L, :]
                m_new = jnp.maximum(b_last + m_prev, jnp.max(b_last - b_row + i_row, axis=-1, keepdims=True))
                w_s = jnp.exp(b_last - b_col + i_col - m_new)
                decay = jnp.exp(b_last + m_prev - m_new)
                kw = k * w_s
                c_out[si, h] = decay * cmat + _dot_tn(kw.astype(BF16), vb)
                n_out[si, :, lanes] = decay * nrow + jnp.sum(kw, axis=0, keepdims=True)
                m_out[si] = jnp.where(lane == h, m_new, m_vec)
            hcell = cells[0] if len(cells) == 1 else jnp.concatenate(cells, axis=0)
            ogate = z_ref[rows, o_off + dh * h:o_off + dh * (h + 1)]
            mixf[rows, lanes] = _rms(hcell) * mnorm_ref[:, lanes] * jax.nn.sigmoid(ogate)

        for p in range(gla_pairs):
            lanes = slice(LANES * p, LANES * (p + 1))
            hs = [2 * p, 2 * p + 1]
            vs = [z_ref[rows, gv_off + dh * h:gv_off + dh * (h + 1)] for h in hs]
            qsc = z_ref[rows, gq_off + LANES * p:gq_off + LANES * (p + 1)] * (gla_dk ** -0.5)
            kk = z_ref[rows, gk_off + LANES * p:gk_off + LANES * (p + 1)]
            states = [[g_out[si, p]] for si in seq_ids]

            def put_state(j, i, new, p=p, seq_ids=seq_ids):
                g_out[seq_ids[j], p] = new

            outs = _gla_pair_tile([qsc], [kk], [e_all[:, lanes]], vs, gla_masks, pm_ref, lowers, L, states,
                                  put_state)
            for h, o in zip(hs, outs):
                vl = slice(dh * h, dh * (h + 1))
                gg = z_ref[rows, gg_off + dh * h:gg_off + dh * (h + 1)]
                mixf[rows, cw + dh * h:cw + dh * (h + 1)] = _rms(o) * gnorm_ref[:, vl] * _silu(gg)
    mix_ref[...] = mixf[...].astype(BF16)


def _odd_mixer(z, c_all, n_all, m_all, g_all, c_prev, g_prev, layer_j, batch, seq, nseq, tb, L, params):
    (bias_small, mnorm, wa2_pad, gla_ba, gnorm) = params
    n, e = z.shape
    n_layers, _, heads, dh, _ = c_all.shape
    cw = heads * dh
    gla_pairs = g_all.shape[2]
    R = nseq * tb
    nt = seq // tb
    m3, pm, widths = _level_tables(L)
    const = lambda a: pl.BlockSpec(a.shape, lambda b, t: (0,) * a.ndim)
    lay = lambda a: _layer_spec(a.shape[1:], layer_j, 2)
    aliased = c_prev is not None
    kern = functools.partial(_odd_mixer_kernel, nseq=nseq, tb=tb, L=L, widths=widths, heads=heads, dh=dh,
                             gla_pairs=gla_pairs, aliased=int(aliased))
    cblk = (None, nseq, heads, dh, dh)
    nblk = (None, nseq, 1, cw)
    mblk = (None, nseq, 1, LANES)
    gblk = (None, nseq, gla_pairs, LANES, dh)
    inputs = [z, c_all, n_all, m_all, g_all, m3, pm, bias_small, mnorm, wa2_pad, gla_ba, gnorm]
    in_specs = [
        pl.BlockSpec((R, e), lambda b, t: (b * nt + t, 0)),
        pl.BlockSpec(cblk, lambda b, t: (layer_j, b, 0, 0, 0)),
        pl.BlockSpec(nblk, lambda b, t: (layer_j, b, 0, 0)),
        pl.BlockSpec(mblk, lambda b, t: (layer_j, b, 0, 0)),
        pl.BlockSpec(gblk, lambda b, t: (layer_j, b, 0, 0, 0)),
        const(m3), const(pm), lay(bias_small), lay(mnorm), lay(wa2_pad), lay(gla_ba), lay(gnorm),
    ]
    aliases = {}
    if aliased:
        aliases = {len(inputs): 1, len(inputs) + 1: 4}
        inputs += [c_prev, g_prev]
        in_specs += [pl.BlockSpec(memory_space=pl.ANY)] * 2
    return pl.pallas_call(
        kern,
        out_shape=(
            jax.ShapeDtypeStruct((n, 2 * cw), BF16),
            jax.ShapeDtypeStruct((n_layers, batch, heads, dh, dh), F32),
            jax.ShapeDtypeStruct((batch, 1, cw), F32),
            jax.ShapeDtypeStruct((batch, 1, LANES), F32),
            jax.ShapeDtypeStruct((n_layers, batch, gla_pairs, LANES, dh), F32),
        ),
        grid=(batch // nseq, nt),
        in_specs=in_specs,
        out_specs=(
            pl.BlockSpec((R, 2 * cw), lambda b, t: (b * nt + t, 0)),
            pl.BlockSpec(cblk, lambda b, t: (layer_j, b, 0, 0, 0)),
            pl.BlockSpec(nblk[1:], lambda b, t: (b, 0, 0)),
            pl.BlockSpec(mblk[1:], lambda b, t: (b, 0, 0)),
            pl.BlockSpec(gblk, lambda b, t: (layer_j, b, 0, 0, 0)),
        ),
        scratch_shapes=[
            pltpu.VMEM((R, 2 * cw), F32),
        ],
        input_output_aliases=aliases,
        compiler_params=pltpu.CompilerParams(
            dimension_semantics=("parallel", "arbitrary"), vmem_limit_bytes=48 * MIB),
        name="odd_mixer",
    )(*inputs)


def _block_diag(w):
    nb, ci, di = w.shape
    eye = jnp.eye(nb, dtype=w.dtype)
    return (eye[:, None, :, None] * w[:, :, None, :]).reshape(nb * ci, nb * di)


def kernel(x_prompt, x_sample, state_hgrn, state_lru_h, state_lru_conv, state_mlstm_C, state_mlstm_n, state_mlstm_m, state_gla, norm_mix, norm_ffn, norm_final, w_in_even, hgrn_lb, hgrn_norm, lru_conv_w, lru_conv_b, lru_wa, lru_ba, lru_wx, lru_bx, lru_a, w_out_even, w_in_odd, mlstm_bi, mlstm_bf, mlstm_norm, gla_wa2, gla_ba, gla_norm, w_out_odd, ffn_w_up, ffn_w_down):
    batch, seq, d_model = x_prompt.shape
    dec_batch, dec_seq, _ = x_sample.shape
    depth = norm_mix.shape[0]
    n_even, _, a_heads, a_dk, _ = state_hgrn.shape
    n_odd, _, c_heads, c_dh, _ = state_mlstm_C.shape
    b_width = state_lru_h.shape[-1]
    c_width = c_heads * c_dh
    d_heads, d_dk, d_dv = state_gla.shape[2:]
    gk = d_heads * d_dk
    d_width = d_heads * d_dv
    d_rank = gla_wa2.shape[1]
    gla_pairs = gk // LANES
    assert a_heads * a_dk == b_width and GROUP % dec_seq == 0 and seq % (4 * GROUP) == 0

    w_even = w_in_even.astype(BF16)
    main = 4 * c_width
    small_w = 2 * c_heads
    pad = (-(main + 2 * gk + 2 * d_width + LANES)) % (2 * LANES)
    w_odd = jnp.concatenate([
        w_in_odd[:, :, :main],
        w_in_odd[:, :, main + small_w:main + small_w + 2 * gk + 2 * d_width],
        w_in_odd[:, :, main:main + small_w],
        w_in_odd[:, :, main + small_w + 2 * gk + 2 * d_width:],
        jnp.zeros((n_odd, d_model, LANES - small_w - d_rank + pad), F32),
    ], axis=-1).astype(BF16)
    w_out_e = w_out_even.astype(BF16)
    w_out_o = w_out_odd.astype(BF16)
    w_up = ffn_w_up.astype(BF16)
    w_dn = ffn_w_down.astype(BF16)
    wa_dense = jax.vmap(_block_diag)(lru_wa).astype(BF16)
    wx_dense = jax.vmap(_block_diag)(lru_wx).astype(BF16)
    bias_small = jnp.concatenate(
        [mlstm_bi, mlstm_bf, jnp.zeros((n_odd, LANES - small_w), F32)], axis=-1)[:, None, :]
    wa2_pad = jnp.concatenate([
        jnp.zeros((n_odd, small_w, gk), F32), gla_wa2,
        jnp.zeros((n_odd, LANES - small_w - d_rank, gk), F32)], axis=1).astype(BF16)
    rows3 = lambda a: a[:, None, :]
    norm_mix3, norm_ffn3 = rows3(norm_mix), rows3(norm_ffn)
    even_params = (hgrn_lb, rows3(hgrn_norm), lru_conv_w, rows3(lru_conv_b), wa_dense, rows3(lru_ba),
                   wx_dense, rows3(lru_bx), rows3(lru_a))
    odd_params = (bias_small, rows3(mlstm_norm), wa2_pad, rows3(gla_ba), rows3(gla_norm))

    zeros = lambda *s: jnp.zeros(s, F32)
    groups = [
        dict(batch=batch, seq=seq, nseq=1, tb=4 * GROUP, L=math.gcd(seq, CHUNK), pos0=0,
             x=x_prompt.reshape(batch * seq, d_model),
             hgrn=zeros(n_even, batch, a_heads, a_dk, a_dk),
             lru_h=zeros(n_even, batch, 1, b_width),
             lru_conv=zeros(n_even, batch, CONV_K - 1, b_width),
             mC=zeros(n_odd, batch, c_heads, c_dh, c_dh),
             mn=zeros(n_odd, batch, 1, c_width),
             mm=zeros(n_odd, batch, 1, LANES),
             gla=zeros(n_odd, batch, gla_pairs, LANES, d_dv)),
        dict(batch=dec_batch, seq=dec_seq, nseq=GROUP // dec_seq, tb=dec_seq, L=math.gcd(dec_seq, CHUNK),
             pos0=PAST_LEN,
             x=x_sample.reshape(dec_batch * dec_seq, d_model),
             hgrn=state_hgrn,
             lru_h=state_lru_h[:, :, None, :],
             lru_conv=state_lru_conv,
             mC=state_mlstm_C,
             mn=state_mlstm_n.reshape(n_odd, dec_batch, 1, c_width),
             mm=jnp.pad(state_mlstm_m, ((0, 0), (0, 0), (0, LANES - c_heads)))[:, :, None, :],
             gla=state_gla.reshape(n_odd, dec_batch, gla_pairs, LANES, d_dv)),
    ]

    results = []
    for grp in groups:
        x = grp["x"]
        n = x.shape[0]
        tm = min(1024, n)
        bsz = grp["batch"]
        hgrn_out = mc_out = gla_out = None
        small = dict(lru_h=[], lru_conv=[], mn=[], mm=[])
        for l in range(depth):
            j = l // 2
            if l % 2 == 0:
                z = _norm_matmul(x, norm_mix3, w_even, l, j, tm, 1024)
                mix, hgrn_out, h, cb = _even_mixer(z, grp["hgrn"], grp["lru_h"], grp["lru_conv"], hgrn_out, j, bsz,
                                                   grp["seq"], grp["nseq"], grp["tb"], grp["L"], grp["pos0"],
                                                   even_params)
                small["lru_h"].append(h[:, 0, :])
                small["lru_conv"].append(cb)
                w_out = w_out_e
            else:
                z = _norm_matmul(x, norm_mix3, w_odd, l, j, tm, w_odd.shape[-1] // 3)
                mix, mc_out, nn, mm, gla_out = _odd_mixer(z, grp["mC"], grp["mn"], grp["mm"], grp["gla"], mc_out,
                                                          gla_out, j, bsz, grp["seq"], grp["nseq"], grp["tb"],
                                                          grp["L"], odd_params)
                small["mn"].append(nn.reshape(bsz, c_heads, c_dh))
                small["mm"].append(mm[:, 0, :c_heads])
                w_out = w_out_o
            x = _proj_ffn(x, mix, w_out, norm_ffn3, w_up, w_dn, norm_final[None, :], l, j,
                          final=(l == depth - 1))
        st = {k: jnp.stack(v) for k, v in small.items()}
        st["hgrn"] = hgrn_out
        st["mC"] = mc_out
        st["gla"] = gla_out.reshape(n_odd, bsz, d_heads, d_dk, d_dv)
        results.append((x.reshape(bsz, grp["seq"], d_model), st))

    (y_p, st_p), (y_s, st_s) = results
    return (y_p, y_s, st_p["hgrn"], st_s["hgrn"], st_p["lru_h"], st_s["lru_h"], st_p["lru_conv"], st_s["lru_conv"],
            st_p["mC"], st_s["mC"], st_p["mn"], st_s["mn"], st_p["mm"], st_s["mm"], st_p["gla"], st_s["gla"])
```

```python
import functools
import math

import jax
import jax.numpy as jnp
import numpy as np
from jax import lax
from jax.experimental import pallas as pl
from jax.experimental.pallas import tpu as pltpu

F32 = jnp.float32
BF16 = jnp.bfloat16
EPS = 1e-6
LANES = 128
GROUP = 64
CHUNK = 64
CONV_K = 4
LRU_C = 8.0
GLA_TAU = 16.0
PAST_LEN = 16384
MIB = 1024 * 1024
NEG_INF = float("-inf")


def _log1p_exp_neg_abs(x):
    return jnp.log(1.0 + jnp.exp(-jnp.abs(x)))


def _log_sigmoid(x):
    return jnp.minimum(x, 0.0) - _log1p_exp_neg_abs(x)


def _softplus(x):
    return jnp.maximum(x, 0.0) + _log1p_exp_neg_abs(x)


def _neg_expm1(y):
    return -jnp.tanh(0.5 * y) * (jnp.exp(y) + 1.0)


def _silu(x):
    return x * jax.nn.sigmoid(x)


def _gelu_tanh(x):
    return 0.5 * x * (1.0 + jnp.tanh(math.sqrt(2.0 / math.pi) * (x + 0.044715 * (x * x * x))))


def _rms(x):
    return x * lax.rsqrt(jnp.mean(x * x, axis=-1, keepdims=True) + EPS)


def _dot(a, b):
    return jnp.dot(a, b, preferred_element_type=F32)


def _dot_nt(a, b):
    return lax.dot_general(a, b, (((1,), (1,)), ((), ())), preferred_element_type=F32)


def _dot_tn(a, b):
    return lax.dot_general(a, b, (((0,), (0,)), ((), ())), preferred_element_type=F32)


def _layer_spec(shape, layer, grid_rank, **kw):
    zeros = (0,) * len(shape)
    if grid_rank == 1:
        return pl.BlockSpec((None,) + tuple(shape), lambda i: (layer,) + zeros, **kw)
    return pl.BlockSpec((None,) + tuple(shape), lambda i, j: (layer,) + zeros, **kw)


def _level_tables(L):
    t = np.arange(GROUP)[:, None]
    s = np.arange(GROUP)[None, :]
    widths = []
    w = L // 2
    while w >= 1:
        widths.append(w)
        w //= 2
    mats, pmasks = [], []
    for w in widths:
        same = (t // (2 * w)) == (s // (2 * w))
        low_t, low_s = (t % (2 * w)) >= w, (s % (2 * w)) >= w
        mats.append((same & low_t & low_s & (s <= t)) | (same & ~low_t & ~low_s & (s > t)))
        pmasks.append(same & low_t & ~low_s)
    seg = (t // L) == (s // L)
    mats.append(seg & (s <= t))
    mats.append(seg & (s > t))
    pmasks.append(t == s)
    m_all = np.concatenate(mats, axis=0).astype(np.float32)
    m3 = np.concatenate([m_all, m_all, m_all], axis=1)
    pm = np.stack(pmasks).astype(np.float32)
    return jnp.asarray(m3, BF16), jnp.asarray(np.concatenate([pm, pm, pm, pm], axis=-1)), widths


def _level_sums(m3, g):
    g_hi = g.astype(BF16)
    r1 = g - g_hi.astype(F32)
    g_mid = r1.astype(BF16)
    g_lo = (r1 - g_mid.astype(F32)).astype(BF16)
    return _dot(m3, jnp.concatenate([g_hi, g_mid, g_lo], axis=0))


def _lower_masks(widths):
    row = lax.broadcasted_iota(jnp.int32, (GROUP, LANES), 0)
    return [(row % (2 * w)) >= w for w in widths]


def _split_dot(x, table3):
    x_hi = x.astype(BF16)
    r1 = x - x_hi.astype(F32)
    x_mid = r1.astype(BF16)
    x_lo = (r1 - x_mid.astype(F32)).astype(BF16)
    return _dot(jnp.concatenate([x_hi, x_mid, x_lo], axis=1), table3)


def _expand_table(heads, dh):
    c = np.arange(LANES)[:, None]
    col = np.arange(heads * dh)[None, :] // dh
    t = np.concatenate([c == heads + col, c == col], axis=1).astype(np.float32)
    return jnp.asarray(np.concatenate([t, t, t], axis=0), BF16)


def _block_rows(xs, width):
    zero = jnp.zeros((xs[0].shape[0], width), F32)
    return jnp.concatenate(
        [jnp.concatenate([x if i == h else zero for i in range(len(xs))], axis=1) for h, x in enumerate(xs)], axis=0)


def _pack_heads(cols, lo_half):
    return jnp.concatenate([jnp.where(lo_half, cols[2 * p], cols[2 * p + 1]) for p in range(len(cols) // 2)], axis=1)


def _gla_heads_tile(qs, ks, es, vs, group_of, masks, pm_ref, lowers, L, states, put_state):
    nlev = len(lowers)
    nh, ng = len(vs), len(qs)
    lo_half = lax.broadcasted_iota(jnp.int32, (1, LANES), 1) < GROUP

    def blk(e, i):
        return e[GROUP * i:GROUP * (i + 1), :]

    def own(a, h):
        return a if masks[h] is None else jnp.where(masks[h], a, 0.0)

    def rhs_rows(ys):
        zero = jnp.zeros((GROUP, LANES), F32)
        return jnp.concatenate(
            [jnp.concatenate([own(ys[i], h) if i == group_of[h] else zero for i in range(ng)], axis=1)
             for h in range(nh)], axis=0)

    scores = None
    for lv in range(nlev):
        ys = [jnp.where(lowers[lv], q, k) * blk(e, lv) for q, k, e in zip(qs, ks, es)]
        part = _dot_nt(jnp.concatenate(ys, axis=1).astype(BF16), rhs_rows(ys).astype(BF16)) * pm_ref[lv]
        scores = part if scores is None else scores + part
    qks = [q * k for q, k in zip(qs, ks)]
    d = [jnp.sum(own(qks[group_of[h]], h), axis=-1, keepdims=True) for h in range(nh)]
    diag = pm_ref[nlev] * _pack_heads(d, lo_half)
    p = diag if scores is None else scores + diag
    intra = _dot(p.astype(BF16), _block_rows(vs, LANES).astype(BF16))
    vbs = [v.astype(BF16) for v in vs]
    ebs = [blk(e, nlev) for e in es]
    qes = [q * eb for q, eb in zip(qs, ebs)]
    kds = [k * blk(e, nlev + 1) for k, e in zip(ks, es)]
    pieces = [[] for _ in range(nh)]
    for j in range(GROUP // L):
        sl = slice(L * j, L * (j + 1))
        sbs = [s.astype(BF16) for s in states[j]]
        upds = [None] * ng
        for h in range(nh):
            i = group_of[h]
            pieces[h].append(intra[sl, LANES * h:LANES * (h + 1)] + _dot(own(qes[i][sl], h).astype(BF16), sbs[i]))
            t = _dot_tn(own(kds[i][sl], h).astype(BF16), vbs[h][sl])
            upds[i] = t if upds[i] is None else upds[i] + t
        for i, upd in enumerate(upds):
            last = ebs[i][L * j + L - 1:L * j + L, :]
            decay_col = jnp.transpose(jnp.broadcast_to(last, (8, LANES)))[:, 0:1]
            put_state(j, i, decay_col * states[j][i] + upd)
    return [p[0] if len(p) == 1 else jnp.concatenate(p, axis=0) for p in pieces]


def _norm_matmul_kernel(x_ref, g_ref, w_ref, o_ref, xn_ref):
    @pl.when(pl.program_id(1) == 0)
    def _():
        xn_ref[...] = (_rms(x_ref[...]) * g_ref[...]).astype(BF16)

    o_ref[...] = _dot(xn_ref[...], w_ref[...])


def _norm_matmul(x, g, w, layer, wlayer, tm, tn):
    n, d = x.shape
    e = w.shape[-1]
    return pl.pallas_call(
        _norm_matmul_kernel,
        out_shape=jax.ShapeDtypeStruct((n, e), F32),
        grid=(n // tm, e // tn),
        in_specs=[
            pl.BlockSpec((tm, d), lambda i, j: (i, 0)),
            _layer_spec((1, d), layer, 2),
            pl.BlockSpec((None, d, tn), lambda i, j: (wlayer, 0, j)),
        ],
        out_specs=pl.BlockSpec((tm, tn), lambda i, j: (i, j)),
        scratch_shapes=[pltpu.VMEM((tm, d), BF16)],
        compiler_params=pltpu.CompilerParams(
            dimension_semantics=("parallel", "arbitrary"), vmem_limit_bytes=48 * MIB),
        name="norm_in_proj",
    )(x, g, w)


def _proj_ffn_kernel(x_ref, mix_ref, wo_ref, g_ref, wup_ref, wdn_ref, gf_ref, o_ref, *, d_ff, tf, final):
    xnew = x_ref[...] + _dot(mix_ref[...], wo_ref[...])
    h2 = (_rms(xnew) * g_ref[...]).astype(BF16)
    o_ref[...] = xnew
    for f in range(d_ff // tf):
        gate = _dot(h2, wup_ref[:, f * tf:(f + 1) * tf])
        val = _dot(h2, wup_ref[:, d_ff + f * tf:d_ff + (f + 1) * tf])
        act = (_silu(gate) * val).astype(BF16)
        o_ref[...] += _dot(act, wdn_ref[f * tf:(f + 1) * tf, :])
    if final:
        o_ref[...] = _rms(o_ref[...]) * gf_ref[...]


def _proj_ffn(x, mix, wo, g, wup, wdn, gf, layer, wolayer, final, tm=512, tf=256):
    n, d = x.shape
    d_ff = wdn.shape[-2]
    once = dict(pipeline_mode=pl.Buffered(1))
    return pl.pallas_call(
        functools.partial(_proj_ffn_kernel, d_ff=d_ff, tf=tf, final=final),
        out_shape=jax.ShapeDtypeStruct((n, d), F32),
        grid=(n // tm,),
        in_specs=[
            pl.BlockSpec((tm, d), lambda i: (i, 0)),
            pl.BlockSpec((tm, d), lambda i: (i, 0)),
            _layer_spec((d, d), wolayer, 1, **once),
            _layer_spec((1, d), layer, 1, **once),
            _layer_spec((d, 2 * d_ff), layer, 1, **once),
            _layer_spec((d_ff, d), layer, 1, **once),
            pl.BlockSpec((1, d), lambda i: (0, 0), **once),
        ],
        out_specs=pl.BlockSpec((tm, d), lambda i: (i, 0)),
        compiler_params=pltpu.CompilerParams(
            dimension_semantics=("parallel",), vmem_limit_bytes=52 * MIB),
        name="out_proj_ffn",
    )(x, mix, wo, g, wup, wdn, gf)


def _even_mixer_kernel(*refs, layer_j, nseq, tb, seq, L, widths, pos0, heads, width, aliased):
    (z_ref, s_in, h_in, c_in, m3_ref, pm_ref, lb_ref, hnorm_ref, cw_ref, cb_ref, wa_ref,
     ba_ref, wx_ref, bx_ref, la_ref) = refs[:15]
    mix_ref, s_out, h_out, c_out, mixf, xpad = refs[15 + aliased:]
    ti = pl.program_id(1)
    R = nseq * tb
    dk = width // heads

    @pl.when(ti == 0)
    def _():
        s_out[...] = s_in[...]
        h_out[...] = h_in[...]
        c_out[...] = c_in[...]

    hl = lb_ref[...]
    e = jnp.exp(hl - jnp.max(hl, axis=0, keepdims=True))
    sm = e / jnp.sum(e, axis=0, keepdims=True)
    lb = jnp.zeros((1, width), F32)
    for i in range(1, layer_j + 1):
        lb = lb + sm[i:i + 1, :]
    log_lb = jnp.log(lb)
    lowers = _lower_masks(widths)

    for tidx in range(R // GROUP):
        r0 = tidx * GROUP
        rows = slice(r0, r0 + GROUP)
        seq_ids = [(r0 + L * j) // tb for j in range(GROUP // L)]
        x = z_ref[rows, width:2 * width]
        y = log_lb - x
        logf = _log_sigmoid(x) + jnp.maximum(y, 0.0) + _log1p_exp_neg_abs(y)
        key = 1.0 - jnp.exp(logf)
        qsc = z_ref[rows, 0:width] * (dk ** -0.5)
        e_all = jnp.exp(_level_sums(m3_ref[...], logf))
        ln = [slice(dk * h, dk * (h + 1)) for h in range(heads)]
        vs = [z_ref[rows, 2 * width + dk * h:2 * width + dk * (h + 1)] for h in range(heads)]
        states = [[s_out[si, h] for h in range(heads)] for si in seq_ids]

        def put_state(j, i, new, seq_ids=seq_ids):
            s_out[seq_ids[j], i] = new

        outs = _gla_heads_tile([qsc[:, l] for l in ln], [key[:, l] for l in ln], [e_all[:, l] for l in ln], vs,
                               list(range(heads)), [None] * heads, pm_ref, lowers, L, states, put_state)
        for h, (l, o) in enumerate(zip(ln, outs)):
            gate = z_ref[rows, 3 * width + dk * h:3 * width + dk * (h + 1)]
            mixf[rows, l] = _rms(o) * hnorm_ref[:, l] * _silu(gate)

    xoff, yoff = 4 * width, 5 * width
    xpad[:, 8 - (CONV_K - 1):8, :] = c_out[...]
    xpad[:, 8:, :] = z_ref[:, xoff:xoff + width].reshape(nseq, tb, width)
    xc = cb_ref[...].reshape(1, 1, width) + jnp.zeros((nseq, tb, width), F32)
    for j in range(CONV_K):
        lo = 8 - (CONV_K - 1) + j
        xc = xc + xpad[:, lo:lo + tb, :] * cw_ref[j:j + 1, :].reshape(1, 1, width)
    c_out[...] = xpad[:, tb + 8 - (CONV_K - 1):tb + 8, :]
    xc = xc.reshape(R, width)
    xcb = xc.astype(BF16)
    rg = jax.nn.sigmoid(_dot(xcb, wa_ref[...]) + ba_ref[...])
    ig = jax.nn.sigmoid(_dot(xcb, wx_ref[...]) + bx_ref[...])
    log_a = (-LRU_C) * rg * _softplus(-la_ref[...])
    a = jnp.exp(log_a)
    mult = jnp.sqrt(_neg_expm1(2.0 * log_a))
    row_in_seq = lax.broadcasted_iota(jnp.int32, (R, width), 0) % tb
    if pos0 <= 0 < pos0 + seq:
        mult = jnp.where(row_in_seq + ti * tb + pos0 == 0, 1.0, mult)
    hh = mult * ig * xc
    aa = a
    shift = 1
    while shift < tb:
        m = row_in_seq >= shift
        hh_new = jnp.where(m, aa * pltpu.roll(hh, shift, 0) + hh, hh)
        aa = jnp.where(m, aa * pltpu.roll(aa, shift, 0), aa)
        hh = hh_new
        shift *= 2
    hseq = hh.reshape(nseq, tb, width) + aa.reshape(nseq, tb, width) * h_out[...]
    h_out[...] = hseq[:, tb - 1:tb, :]
    mixf[:, width:2 * width] = hseq.reshape(R, width) * _gelu_tanh(z_ref[:, yoff:yoff + width])
    mix_ref[...] = mixf[...].astype(BF16)


def _even_mixer(z, s_all, h_all, c_all, s_prev, layer_j, batch, seq, nseq, tb, L, pos0, params):
    (hgrn_lb, hgrn_norm, conv_w, conv_b, wa, ba, wx, bx, lru_a) = params
    n, e = z.shape
    n_layers, _, heads, dk, _ = s_all.shape
    width = heads * dk
    R = nseq * tb
    nt = seq // tb
    m3, pm, widths = _level_tables(L)
    const = lambda a: pl.BlockSpec(a.shape, lambda b, t: (0,) * a.ndim)
    lay = lambda a: _layer_spec(a.shape[1:], layer_j, 2)
    aliased = s_prev is not None
    kern = functools.partial(_even_mixer_kernel, layer_j=layer_j, nseq=nseq, tb=tb, seq=seq, L=L, widths=widths,
                             pos0=pos0, heads=heads, width=width, aliased=int(aliased))
    sblk = (None, nseq, heads, dk, dk)
    hblk = (None, nseq, 1, width)
    cblk = (None, nseq, CONV_K - 1, width)
    inputs = [z, s_all, h_all, c_all, m3, pm, hgrn_lb, hgrn_norm, conv_w, conv_b, wa, ba, wx, bx, lru_a]
    in_specs = [
        pl.BlockSpec((R, e), lambda b, t: (b * nt + t, 0)),
        pl.BlockSpec(sblk, lambda b, t: (layer_j, b, 0, 0, 0)),
        pl.BlockSpec(hblk, lambda b, t: (layer_j, b, 0, 0)),
        pl.BlockSpec(cblk, lambda b, t: (layer_j, b, 0, 0)),
        const(m3), const(pm), const(hgrn_lb), lay(hgrn_norm), lay(conv_w), lay(conv_b),
        lay(wa), lay(ba), lay(wx), lay(bx), lay(lru_a),
    ]
    if aliased:
        inputs.append(s_prev)
        in_specs.append(pl.BlockSpec(memory_space=pl.ANY))
    return pl.pallas_call(
        kern,
        out_shape=(
            jax.ShapeDtypeStruct((n, 2 * width), BF16),
            jax.ShapeDtypeStruct((n_layers, batch, heads, dk, dk), F32),
            jax.ShapeDtypeStruct((batch, 1, width), F32),
            jax.ShapeDtypeStruct((batch, CONV_K - 1, width), F32),
        ),
        grid=(batch // nseq, nt),
        in_specs=in_specs,
        out_specs=(
            pl.BlockSpec((R, 2 * width), lambda b, t: (b * nt + t, 0)),
            pl.BlockSpec(sblk, lambda b, t: (layer_j, b, 0, 0, 0)),
            pl.BlockSpec(hblk[1:], lambda b, t: (b, 0, 0)),
            pl.BlockSpec(cblk[1:], lambda b, t: (b, 0, 0)),
        ),
        scratch_shapes=[
            pltpu.VMEM((R, 2 * width), F32),
            pltpu.VMEM((nseq, tb + 8, width), F32),
        ],
        input_output_aliases={len(inputs) - 1: 1} if aliased else {},
        compiler_params=pltpu.CompilerParams(
            dimension_semantics=("parallel", "arbitrary"), vmem_limit_bytes=48 * MIB),
        name="even_mixer",
    )(*inputs)


def _odd_mixer_kernel(*refs, nseq, tb, L, widths, heads, dh, gla_pairs, aliased):
    (z_ref, c_in, n_in, m_in, g_in, m3_ref, pm_ref, ex3_ref, bias_ref, mnorm_ref, wa2_ref, gba_ref,
     gnorm_ref) = refs[:13]
    mix_ref, c_out, n_out, m_out, g_out, mixf = refs[13 + 2 * aliased:]
    ti = pl.program_id(1)
    R = nseq * tb
    cw = heads * dh
    gk = gla_pairs * LANES
    q_off, k_off, v_off, o_off = 0, cw, 2 * cw, 3 * cw
    gq_off = 4 * cw
    gk_off = gq_off + gk
    gv_off = gk_off + gk
    gg_off = gv_off + cw
    small_off = gg_off + cw
    gla_dk = gk // heads
    nlev = len(widths)

    @pl.when(ti == 0)
    def _():
        c_out[...] = c_in[...]
        n_out[...] = n_in[...]
        m_out[...] = m_in[...]
        g_out[...] = g_in[...]

    lane = lax.broadcasted_iota(jnp.int32, (1, LANES), 1)
    is_f = (lane >= heads) & (lane < 2 * heads)
    lo_half = lane < LANES // 2
    hi_half = lane >= LANES // 2
    trow = lax.broadcasted_iota(jnp.int32, (GROUP, heads * GROUP), 0)
    tcol = lax.broadcasted_iota(jnp.int32, (GROUP, heads * GROUP), 1) % GROUP
    seg_tril = (tcol <= trow) & ((trow // L) == (tcol // L))
    gla_masks = [lo_half, hi_half]
    pack = functools.partial(_pack_heads, lo_half=lo_half)

    def head_reduce(x, h, fn, fill):
        col = x[:, LANES * (h // 2):LANES * (h // 2 + 1)]
        return fn(jnp.where(lo_half if h % 2 == 0 else hi_half, col, fill), axis=-1, keepdims=True)
    lowers = _lower_masks(widths)

    for tidx in range(R // GROUP):
        r0 = tidx * GROUP
        rows = slice(r0, r0 + GROUP)
        seq_ids = [(r0 + L * j) // tb for j in range(GROUP // L)]
        small = z_ref[rows, small_off:small_off + LANES]
        g0 = small + bias_ref[...]
        pre = _dot(small.astype(BF16), wa2_ref[...]) + gba_ref[...]
        la = _log_sigmoid(pre) * (1.0 / GLA_TAU)
        sums = _level_sums(m3_ref[...], jnp.concatenate([la, _log_sigmoid(g0)], axis=-1))
        e_all = jnp.exp(sums[:, 0:gk])
        gates = jnp.where(is_f, sums[GROUP * nlev:GROUP * (nlev + 1), gk:gk + LANES], g0)

        m_next = [m_out[si] for si in seq_ids]
        ex = _split_dot(gates, ex3_ref[...])
        bcol = [ex[:, dh * h:dh * (h + 1)] for h in range(heads)]
        icol = [ex[:, cw + dh * h:cw + dh * (h + 1)] for h in range(heads)]
        mprev = [[jnp.broadcast_to(mv[:, heads + h:heads + h + 1], (1, dh)) for mv in m_next] for h in range(heads)]
        inter = []
        for h in range(heads):
            parts = [bcol[h][L * j:L * (j + 1)] + mprev[h][j] for j in range(GROUP // L)]
            inter.append(parts[0] if len(parts) == 1 else jnp.concatenate(parts, axis=0))
        gtt = jnp.transpose(gates)
        urow = jnp.concatenate([gtt[h:h + 1, :] - gtt[heads + h:heads + h + 1, :] for h in range(heads)], axis=1)
        dlog = jnp.where(seg_tril, pack(bcol) + urow, NEG_INF)
        m_t = [jnp.maximum(inter[h], head_reduce(dlog, h, jnp.max, NEG_INF)) for h in range(heads)]
        w_intra = jnp.exp(dlog - pack(m_t))
        w_inter = [jnp.exp(inter[h] - m_t[h]) for h in range(heads)]
        q_t = [z_ref[rows, q_off + dh * h:q_off + dh * (h + 1)] for h in range(heads)]
        k_t = [z_ref[rows, k_off + dh * h:k_off + dh * (h + 1)] * (dh ** -0.5) for h in range(heads)]
        v_t = [z_ref[rows, v_off + dh * h:v_off + dh * (h + 1)] for h in range(heads)]
        qk = _dot_nt(jnp.concatenate(q_t, axis=1).astype(BF16), _block_rows(k_t, dh).astype(BF16)) * w_intra
        num_all = _dot(qk.astype(BF16), _block_rows(v_t, dh).astype(BF16))
        for h in range(heads):
            lanes = slice(dh * h, dh * (h + 1))
            qk_sum = head_reduce(qk, h, jnp.sum, 0.0)
            cells = []
            for j in range(GROUP // L):
                sl = slice(L * j, L * (j + 1))
                si = seq_ids[j]
                q, k, vb = q_t[h][sl], k_t[h][sl], v_t[h][sl].astype(BF16)
                cmat = c_out[si, h]
                nrow = n_out[si, :, lanes]
                wi = w_inter[h][sl]
                num = num_all[sl, lanes] + wi * _dot(q.astype(BF16), cmat.astype(BF16))
                den = qk_sum[sl] + wi * jnp.sum(q * nrow, axis=-1, keepdims=True)
                cells.append(num / jnp.maximum(jnp.abs(den), jnp.exp(-m_t[h][sl])))
                b_last = bcol[h][L * j + L - 1:L * j + L, :]
                m_prev = mprev[h][j]
                wlog = b_last - bcol[h][sl] + icol[h][sl]
                m_new = jnp.maximum(b_last + m_prev, jnp.max(wlog, axis=0, keepdims=True))
                w_s = jnp.exp(wlog - m_new)
                decay = jnp.exp(b_last + m_prev - m_new)
                kw = k * w_s
                c_out[si, h] = decay * cmat + _dot_tn(kw.astype(BF16), vb)
                n_out[si, :, lanes] = decay * nrow + jnp.sum(kw, axis=0, keepdims=True)
                m_next[j] = jnp.where(lane == heads + h, m_new, m_next[j])
            hcell = cells[0] if len(cells) == 1 else jnp.concatenate(cells, axis=0)
            ogate = z_ref[rows, o_off + dh * h:o_off + dh * (h + 1)]
            mixf[rows, lanes] = _rms(hcell) * mnorm_ref[:, lanes] * jax.nn.sigmoid(ogate)
        for j, si in enumerate(seq_ids):
            m_out[si] = m_next[j]

        kl = [slice(LANES * p, LANES * (p + 1)) for p in range(gla_pairs)]
        vs = [z_ref[rows, gv_off + dh * h:gv_off + dh * (h + 1)] for h in range(heads)]
        qsc = [z_ref[rows, gq_off + LANES * p:gq_off + LANES * (p + 1)] * (gla_dk ** -0.5) for p in range(gla_pairs)]
        kk = [z_ref[rows, gk_off + LANES * p:gk_off + LANES * (p + 1)] for p in range(gla_pairs)]
        states = [[g_out[si, p] for p in range(gla_pairs)] for si in seq_ids]

        def put_state(j, i, new, seq_ids=seq_ids):
            g_out[seq_ids[j], i] = new

        outs = _gla_heads_tile(qsc, kk, [e_all[:, l] for l in kl], vs, [h // 2 for h in range(heads)],
                               [gla_masks[h % 2] for h in range(heads)], pm_ref, lowers, L, states, put_state)
        for h, o in enumerate(outs):
            vl = slice(dh * h, dh * (h + 1))
            gg = z_ref[rows, gg_off + dh * h:gg_off + dh * (h + 1)]
            mixf[rows, cw + dh * h:cw + dh * (h + 1)] = _rms(o) * gnorm_ref[:, vl] * _silu(gg)
    mix_ref[...] = mixf[...].astype(BF16)


def _odd_mixer(z, c_all, n_all, m_all, g_all, c_prev, g_prev, layer_j, batch, seq, nseq, tb, L, params):
    (bias_small, mnorm, wa2_pad, gla_ba, gnorm) = params
    n, e = z.shape
    n_layers, _, heads, dh, _ = c_all.shape
    cw = heads * dh
    gla_pairs = g_all.shape[2]
    R = nseq * tb
    nt = seq // tb
    m3, pm, widths = _level_tables(L)
    const = lambda a: pl.BlockSpec(a.shape, lambda b, t: (0,) * a.ndim)
    lay = lambda a: _layer_spec(a.shape[1:], layer_j, 2)
    aliased = c_prev is not None
    kern = functools.partial(_odd_mixer_kernel, nseq=nseq, tb=tb, L=L, widths=widths, heads=heads, dh=dh,
                             gla_pairs=gla_pairs, aliased=int(aliased))
    cblk = (None, nseq, heads, dh, dh)
    nblk = (None, nseq, 1, cw)
    mblk = (None, nseq, 1, LANES)
    gblk = (None, nseq, gla_pairs, LANES, dh)
    ex3 = _expand_table(heads, dh)
    inputs = [z, c_all, n_all, m_all, g_all, m3, pm, ex3, bias_small, mnorm, wa2_pad, gla_ba, gnorm]
    in_specs = [
        pl.BlockSpec((R, e), lambda b, t: (b * nt + t, 0)),
        pl.BlockSpec(cblk, lambda b, t: (layer_j, b, 0, 0, 0)),
        pl.BlockSpec(nblk, lambda b, t: (layer_j, b, 0, 0)),
        pl.BlockSpec(mblk, lambda b, t: (layer_j, b, 0, 0)),
        pl.BlockSpec(gblk, lambda b, t: (layer_j, b, 0, 0, 0)),
        const(m3), const(pm), const(ex3), lay(bias_small), lay(mnorm), lay(wa2_pad), lay(gla_ba), lay(gnorm),
    ]
    aliases = {}
    if aliased:
        aliases = {len(inputs): 1, len(inputs) + 1: 4}
        inputs += [c_prev, g_prev]
        in_specs += [pl.BlockSpec(memory_space=pl.ANY)] * 2
    return pl.pallas_call(
        kern,
        out_shape=(
            jax.ShapeDtypeStruct((n, 2 * cw), BF16),
            jax.ShapeDtypeStruct((n_layers, batch, heads, dh, dh), F32),
            jax.ShapeDtypeStruct((batch, 1, cw), F32),
            jax.ShapeDtypeStruct((batch, 1, LANES), F32),
            jax.ShapeDtypeStruct((n_layers, batch, gla_pairs, LANES, dh), F32),
        ),
        grid=(batch // nseq, nt),
        in_specs=in_specs,
        out_specs=(
            pl.BlockSpec((R, 2 * cw), lambda b, t: (b * nt + t, 0)),
            pl.BlockSpec(cblk, lambda b, t: (layer_j, b, 0, 0, 0)),
            pl.BlockSpec(nblk[1:], lambda b, t: (b, 0, 0)),
            pl.BlockSpec(mblk[1:], lambda b, t: (b, 0, 0)),
            pl.BlockSpec(gblk, lambda b, t: (layer_j, b, 0, 0, 0)),
        ),
        scratch_shapes=[
            pltpu.VMEM((R, 2 * cw), F32),
        ],
        input_output_aliases=aliases,
        compiler_params=pltpu.CompilerParams(
            dimension_semantics=("parallel", "arbitrary"), vmem_limit_bytes=48 * MIB),
        name="odd_mixer",
    )(*inputs)


def _block_diag(w):
    nb, ci, di = w.shape
    eye = jnp.eye(nb, dtype=w.dtype)
    return (eye[:, None, :, None] * w[:, :, None, :]).reshape(nb * ci, nb * di)


def kernel(x_prompt, x_sample, state_hgrn, state_lru_h, state_lru_conv, state_mlstm_C, state_mlstm_n, state_mlstm_m, state_gla, norm_mix, norm_ffn, norm_final, w_in_even, hgrn_lb, hgrn_norm, lru_conv_w, lru_conv_b, lru_wa, lru_ba, lru_wx, lru_bx, lru_a, w_out_even, w_in_odd, mlstm_bi, mlstm_bf, mlstm_norm, gla_wa2, gla_ba, gla_norm, w_out_odd, ffn_w_up, ffn_w_down):
    batch, seq, d_model = x_prompt.shape
    dec_batch, dec_seq, _ = x_sample.shape
    depth = norm_mix.shape[0]
    n_even, _, a_heads, a_dk, _ = state_hgrn.shape
    n_odd, _, c_heads, c_dh, _ = state_mlstm_C.shape
    b_width = state_lru_h.shape[-1]
    c_width = c_heads * c_dh
    d_heads, d_dk, d_dv = state_gla.shape[2:]
    gk = d_heads * d_dk
    d_width = d_heads * d_dv
    d_rank = gla_wa2.shape[1]
    gla_pairs = gk // LANES
    assert a_heads * a_dk == b_width and GROUP % dec_seq == 0 and seq % (4 * GROUP) == 0

    w_even = w_in_even.astype(BF16)
    main = 4 * c_width
    small_w = 2 * c_heads
    pad = (-(main + 2 * gk + 2 * d_width + LANES)) % (2 * LANES)
    w_odd = jnp.concatenate([
        w_in_odd[:, :, :main],
        w_in_odd[:, :, main + small_w:main + small_w + 2 * gk + 2 * d_width],
        w_in_odd[:, :, main:main + small_w],
        w_in_odd[:, :, main + small_w + 2 * gk + 2 * d_width:],
        jnp.zeros((n_odd, d_model, LANES - small_w - d_rank + pad), F32),
    ], axis=-1).astype(BF16)
    w_out_e = w_out_even.astype(BF16)
    w_out_o = w_out_odd.astype(BF16)
    w_up = ffn_w_up.astype(BF16)
    w_dn = ffn_w_down.astype(BF16)
    wa_dense = jax.vmap(_block_diag)(lru_wa).astype(BF16)
    wx_dense = jax.vmap(_block_diag)(lru_wx).astype(BF16)
    bias_small = jnp.concatenate(
        [mlstm_bi, mlstm_bf, jnp.zeros((n_odd, LANES - small_w), F32)], axis=-1)[:, None, :]
    wa2_pad = jnp.concatenate([
        jnp.zeros((n_odd, small_w, gk), F32), gla_wa2,
        jnp.zeros((n_odd, LANES - small_w - d_rank, gk), F32)], axis=1).astype(BF16)
    rows3 = lambda a: a[:, None, :]
    norm_mix3, norm_ffn3 = rows3(norm_mix), rows3(norm_ffn)
    even_params = (hgrn_lb, rows3(hgrn_norm), lru_conv_w, rows3(lru_conv_b), wa_dense, rows3(lru_ba),
                   wx_dense, rows3(lru_bx), rows3(lru_a))
    odd_params = (bias_small, rows3(mlstm_norm), wa2_pad, rows3(gla_ba), rows3(gla_norm))

    zeros = lambda *s: jnp.zeros(s, F32)
    groups = [
        dict(batch=batch, seq=seq, nseq=1, tb=4 * GROUP, L=math.gcd(seq, CHUNK), pos0=0,
             x=x_prompt.reshape(batch * seq, d_model),
             hgrn=zeros(n_even, batch, a_heads, a_dk, a_dk),
             lru_h=zeros(n_even, batch, 1, b_width),
             lru_conv=zeros(n_even, batch, CONV_K - 1, b_width),
             mC=zeros(n_odd, batch, c_heads, c_dh, c_dh),
             mn=zeros(n_odd, batch, 1, c_width),
             mm=zeros(n_odd, batch, 1, LANES),
             gla=zeros(n_odd, batch, gla_pairs, LANES, d_dv)),
        dict(batch=dec_batch, seq=dec_seq, nseq=GROUP // dec_seq, tb=dec_seq, L=math.gcd(dec_seq, CHUNK),
             pos0=PAST_LEN,
             x=x_sample.reshape(dec_batch * dec_seq, d_model),
             hgrn=state_hgrn,
             lru_h=state_lru_h[:, :, None, :],
             lru_conv=state_lru_conv,
             mC=state_mlstm_C,
             mn=state_mlstm_n.reshape(n_odd, dec_batch, 1, c_width),
             mm=jnp.pad(state_mlstm_m, ((0, 0), (0, 0), (c_heads, LANES - 2 * c_heads)))[:, :, None, :],
             gla=state_gla.reshape(n_odd, dec_batch, gla_pairs, LANES, d_dv)),
    ]

    results = []
    for grp in groups:
        x = grp["x"]
        n = x.shape[0]
        tm = min(1024, n)
        bsz = grp["batch"]
        hgrn_out = mc_out = gla_out = None
        small = dict(lru_h=[], lru_conv=[], mn=[], mm=[])
        for l in range(depth):
            j = l // 2
            if l % 2 == 0:
                z = _norm_matmul(x, norm_mix3, w_even, l, j, tm, 1024)
                mix, hgrn_out, h, cb = _even_mixer(z, grp["hgrn"], grp["lru_h"], grp["lru_conv"], hgrn_out, j, bsz,
                                                   grp["seq"], grp["nseq"], grp["tb"], grp["L"], grp["pos0"],
                                                   even_params)
                small["lru_h"].append(h[:, 0, :])
                small["lru_conv"].append(cb)
                w_out = w_out_e
            else:
                z = _norm_matmul(x, norm_mix3, w_odd, l, j, tm, w_odd.shape[-1] // 3)
                mix, mc_out, nn, mm, gla_out = _odd_mixer(z, grp["mC"], grp["mn"], grp["mm"], grp["gla"], mc_out,
                                                          gla_out, j, bsz, grp["seq"], grp["nseq"], grp["tb"],
                                                          grp["L"], odd_params)
                small["mn"].append(nn.reshape(bsz, c_heads, c_dh))
                small["mm"].append(mm[:, 0, c_heads:2 * c_heads])
                w_out = w_out_o
            x = _proj_ffn(x, mix, w_out, norm_ffn3, w_up, w_dn, norm_final[None, :], l, j,
                          final=(l == depth - 1))
        st = {k: jnp.stack(v) for k, v in small.items()}
        st["hgrn"] = hgrn_out
        st["mC"] = mc_out
        st["gla"] = gla_out.reshape(n_odd, bsz, d_heads, d_dk, d_dv)
        results.append((x.reshape(bsz, grp["seq"], d_model), st))

    (y_p, st_p), (y_s, st_s) = results
    return (y_p, y_s, st_p["hgrn"], st_s["hgrn"], st_p["lru_h"], st_s["lru_h"], st_p["lru_conv"], st_s["lru_conv"],
            st_p["mC"], st_s["mC"], st_p["mn"], st_s["mn"], st_p["mm"], st_s["mm"], st_p["gla"], st_s["gla"])
```

```python
import functools
import math

import jax
import jax.numpy as jnp
import numpy as np
from jax import lax
from jax.experimental import pallas as pl
from jax.experimental.pallas import tpu as pltpu

F32 = jnp.float32
BF16 = jnp.bfloat16
EPS = 1e-6
LANES = 128
GROUP = 64
CHUNK = 64
CONV_K = 4
LRU_C = 8.0
GLA_TAU = 16.0
PAST_LEN = 16384
MIB = 1024 * 1024
NEG_INF = float("-inf")


def _log1p_exp_neg_abs(x):
    return jnp.log(1.0 + jnp.exp(-jnp.abs(x)))


def _log_sigmoid(x):
    return jnp.minimum(x, 0.0) - _log1p_exp_neg_abs(x)


def _softplus(x):
    return jnp.maximum(x, 0.0) + _log1p_exp_neg_abs(x)


def _neg_expm1(y):
    return -jnp.tanh(0.5 * y) * (jnp.exp(y) + 1.0)


def _silu(x):
    return x * jax.nn.sigmoid(x)


def _gelu_tanh(x):
    return 0.5 * x * (1.0 + jnp.tanh(math.sqrt(2.0 / math.pi) * (x + 0.044715 * (x * x * x))))


def _rms(x):
    return x * lax.rsqrt(jnp.mean(x * x, axis=-1, keepdims=True) + EPS)


def _dot(a, b):
    return jnp.dot(a, b, preferred_element_type=F32)


def _dot_nt(a, b):
    return lax.dot_general(a, b, (((1,), (1,)), ((), ())), preferred_element_type=F32)


def _dot_tn(a, b):
    return lax.dot_general(a, b, (((0,), (0,)), ((), ())), preferred_element_type=F32)


def _layer_spec(shape, layer, grid_rank, **kw):
    zeros = (0,) * len(shape)
    if grid_rank == 1:
        return pl.BlockSpec((None,) + tuple(shape), lambda i: (layer,) + zeros, **kw)
    return pl.BlockSpec((None,) + tuple(shape), lambda i, j: (layer,) + zeros, **kw)


def _level_tables(L):
    t = np.arange(GROUP)[:, None]
    s = np.arange(GROUP)[None, :]
    widths = []
    w = L // 2
    while w >= 1:
        widths.append(w)
        w //= 2
    mats, pmasks = [], []
    for w in widths:
        same = (t // (2 * w)) == (s // (2 * w))
        low_t, low_s = (t % (2 * w)) >= w, (s % (2 * w)) >= w
        mats.append((same & low_t & low_s & (s <= t)) | (same & ~low_t & ~low_s & (s > t)))
        pmasks.append(same & low_t & ~low_s)
    seg = (t // L) == (s // L)
    mats.append(seg & (s <= t))
    mats.append(seg & (s > t))
    pmasks.append(t == s)
    m_all = np.concatenate(mats, axis=0).astype(np.float32)
    m3 = np.concatenate([m_all, m_all, m_all], axis=1)
    pm = np.stack(pmasks).astype(np.float32)
    return jnp.asarray(m3, BF16), jnp.asarray(np.concatenate([pm, pm, pm, pm], axis=-1)), widths


def _level_sums(m3, g):
    g_hi = g.astype(BF16)
    r1 = g - g_hi.astype(F32)
    g_mid = r1.astype(BF16)
    g_lo = (r1 - g_mid.astype(F32)).astype(BF16)
    return _dot(m3, jnp.concatenate([g_hi, g_mid, g_lo], axis=0))


def _lower_masks(widths):
    row = lax.broadcasted_iota(jnp.int32, (GROUP, LANES), 0)
    return [(row % (2 * w)) >= w for w in widths]


def _split_dot(x, table3):
    x_hi = x.astype(BF16)
    r1 = x - x_hi.astype(F32)
    x_mid = r1.astype(BF16)
    x_lo = (r1 - x_mid.astype(F32)).astype(BF16)
    return _dot(jnp.concatenate([x_hi, x_mid, x_lo], axis=1), table3)


def _expand_table(heads, dh):
    c = np.arange(LANES)[:, None]
    col = np.arange(heads * dh)[None, :] // dh
    t = np.concatenate([c == heads + col, c == col], axis=1).astype(np.float32)
    return jnp.asarray(np.concatenate([t, t, t], axis=0), BF16)


def _block_rows(xs, width):
    zero = jnp.zeros((xs[0].shape[0], width), xs[0].dtype)
    return jnp.concatenate(
        [jnp.concatenate([x if i == h else zero for i in range(len(xs))], axis=1) for h, x in enumerate(xs)], axis=0)


def _pack_heads(cols, lo_half):
    return jnp.concatenate([jnp.where(lo_half, cols[2 * p], cols[2 * p + 1]) for p in range(len(cols) // 2)], axis=1)


def _gla_heads_tile(qs, ks, es, vs, group_of, masks, pm_ref, lowers, L, states, put_state):
    nlev = len(lowers)
    nh, ng = len(vs), len(qs)
    lo_half = lax.broadcasted_iota(jnp.int32, (1, LANES), 1) < GROUP

    def blk(e, i):
        return e[GROUP * i:GROUP * (i + 1), :]

    def own(a, h):
        return a if masks[h] is None else jnp.where(masks[h], a, 0.0)

    def rhs_rows(ys):
        zero = jnp.zeros((GROUP, LANES), F32)
        return jnp.concatenate(
            [jnp.concatenate([own(ys[i], h) if i == group_of[h] else zero for i in range(ng)], axis=1)
             for h in range(nh)], axis=0)

    scores = None
    for lv in range(nlev):
        ys = [jnp.where(lowers[lv], q, k) * blk(e, lv) for q, k, e in zip(qs, ks, es)]
        part = _dot_nt(jnp.concatenate(ys, axis=1).astype(BF16), rhs_rows(ys).astype(BF16)) * pm_ref[lv]
        scores = part if scores is None else scores + part
    qks = [q * k for q, k in zip(qs, ks)]
    d = [jnp.sum(own(qks[group_of[h]], h), axis=-1, keepdims=True) for h in range(nh)]
    diag = pm_ref[nlev] * _pack_heads(d, lo_half)
    p = diag if scores is None else scores + diag
    intra = _dot(p.astype(BF16), _block_rows(vs, LANES).astype(BF16))
    vbs = [v.astype(BF16) for v in vs]
    ebs = [blk(e, nlev) for e in es]
    qes = [q * eb for q, eb in zip(qs, ebs)]
    kds = [k * blk(e, nlev + 1) for k, e in zip(ks, es)]
    pieces = [[] for _ in range(nh)]
    for j in range(GROUP // L):
        sl = slice(L * j, L * (j + 1))
        sbs = [s.astype(BF16) for s in states[j]]
        upds = [None] * ng
        for h in range(nh):
            i = group_of[h]
            pieces[h].append(intra[sl, LANES * h:LANES * (h + 1)] + _dot(own(qes[i][sl], h).astype(BF16), sbs[i]))
            t = _dot_tn(own(kds[i][sl], h).astype(BF16), vbs[h][sl])
            upds[i] = t if upds[i] is None else upds[i] + t
        for i, upd in enumerate(upds):
            last = ebs[i][L * j + L - 1:L * j + L, :]
            decay_col = jnp.transpose(jnp.broadcast_to(last, (8, LANES)))[:, 0:1]
            put_state(j, i, decay_col * states[j][i] + upd)
    return [p[0] if len(p) == 1 else jnp.concatenate(p, axis=0) for p in pieces]


def _norm_matmul_kernel(x_ref, g_ref, w_ref, o16_ref, o32_ref, *, e16, tn):
    xn = (_rms(x_ref[...]) * g_ref[...]).astype(BF16)
    for c in range(e16 // tn):
        o16_ref[:, c * tn:(c + 1) * tn] = _dot(xn, w_ref[:, c * tn:(c + 1) * tn]).astype(BF16)
    o32_ref[...] = _dot(xn, w_ref[:, e16:])


def _norm_matmul(x, g, w, layer, wlayer, e16, tm=512, tn=512):
    n, d = x.shape
    e = w.shape[-1]
    return pl.pallas_call(
        functools.partial(_norm_matmul_kernel, e16=e16, tn=tn),
        out_shape=(jax.ShapeDtypeStruct((n, e16), BF16), jax.ShapeDtypeStruct((n, e - e16), F32)),
        grid=(n // tm,),
        in_specs=[
            pl.BlockSpec((tm, d), lambda i: (i, 0)),
            _layer_spec((1, d), layer, 1),
            _layer_spec((d, e), wlayer, 1, pipeline_mode=pl.Buffered(1)),
        ],
        out_specs=(pl.BlockSpec((tm, e16), lambda i: (i, 0)), pl.BlockSpec((tm, e - e16), lambda i: (i, 0))),
        compiler_params=pltpu.CompilerParams(
            dimension_semantics=("parallel",), vmem_limit_bytes=48 * MIB),
        name="norm_in_proj",
    )(x, g, w)


def _proj_ffn_kernel(x_ref, mix_ref, wo_ref, g_ref, wup_ref, wdn_ref, gf_ref, o_ref, *, d_ff, tf, final):
    xnew = x_ref[...] + _dot(mix_ref[...], wo_ref[...])
    h2 = (_rms(xnew) * g_ref[...]).astype(BF16)
    o_ref[...] = xnew
    for f in range(d_ff // tf):
        gate = _dot(h2, wup_ref[:, f * tf:(f + 1) * tf])
        val = _dot(h2, wup_ref[:, d_ff + f * tf:d_ff + (f + 1) * tf])
        act = (_silu(gate) * val).astype(BF16)
        o_ref[...] += _dot(act, wdn_ref[f * tf:(f + 1) * tf, :])
    if final:
        o_ref[...] = _rms(o_ref[...]) * gf_ref[...]


def _proj_ffn(x, mix, wo, g, wup, wdn, gf, layer, wolayer, final, tm=512, tf=256):
    n, d = x.shape
    d_ff = wdn.shape[-2]
    once = dict(pipeline_mode=pl.Buffered(1))
    return pl.pallas_call(
        functools.partial(_proj_ffn_kernel, d_ff=d_ff, tf=tf, final=final),
        out_shape=jax.ShapeDtypeStruct((n, d), F32),
        grid=(n // tm,),
        in_specs=[
            pl.BlockSpec((tm, d), lambda i: (i, 0)),
            pl.BlockSpec((tm, d), lambda i: (i, 0)),
            _layer_spec((d, d), wolayer, 1, **once),
            _layer_spec((1, d), layer, 1, **once),
            _layer_spec((d, 2 * d_ff), layer, 1, **once),
            _layer_spec((d_ff, d), layer, 1, **once),
            pl.BlockSpec((1, d), lambda i: (0, 0), **once),
        ],
        out_specs=pl.BlockSpec((tm, d), lambda i: (i, 0)),
        compiler_params=pltpu.CompilerParams(
            dimension_semantics=("parallel",), vmem_limit_bytes=52 * MIB),
        name="out_proj_ffn",
    )(x, mix, wo, g, wup, wdn, gf)


def _even_mixer_kernel(*refs, layer_j, nseq, tb, seq, L, widths, pos0, heads, width, aliased):
    (z_ref, zg_ref, s_in, h_in, c_in, m3_ref, pm_ref, lb_ref, hnorm_ref, cw_ref, cb_ref, wa_ref,
     ba_ref, wx_ref, bx_ref, la_ref) = refs[:16]
    mix_ref, s_out, h_out, c_out, mixf, xpad = refs[16 + aliased:]
    ti = pl.program_id(1)
    R = nseq * tb
    dk = width // heads

    @pl.when(ti == 0)
    def _():
        s_out[...] = s_in[...]
        h_out[...] = h_in[...]
        c_out[...] = c_in[...]

    hl = lb_ref[...]
    e = jnp.exp(hl - jnp.max(hl, axis=0, keepdims=True))
    sm = e / jnp.sum(e, axis=0, keepdims=True)
    lb = jnp.zeros((1, width), F32)
    for i in range(1, layer_j + 1):
        lb = lb + sm[i:i + 1, :]
    log_lb = jnp.log(lb)
    lowers = _lower_masks(widths)

    for tidx in range(R // GROUP):
        r0 = tidx * GROUP
        rows = slice(r0, r0 + GROUP)
        seq_ids = [(r0 + L * j) // tb for j in range(GROUP // L)]
        x = zg_ref[rows, :]
        y = log_lb - x
        logf = _log_sigmoid(x) + jnp.maximum(y, 0.0) + _log1p_exp_neg_abs(y)
        key = 1.0 - jnp.exp(logf)
        qsc = z_ref[rows, 0:width].astype(F32) * (dk ** -0.5)
        e_all = jnp.exp(_level_sums(m3_ref[...], logf))
        ln = [slice(dk * h, dk * (h + 1)) for h in range(heads)]
        vs = [z_ref[rows, width + dk * h:width + dk * (h + 1)] for h in range(heads)]
        states = [[s_out[si, h] for h in range(heads)] for si in seq_ids]

        def put_state(j, i, new, seq_ids=seq_ids):
            s_out[seq_ids[j], i] = new

        outs = _gla_heads_tile([qsc[:, l] for l in ln], [key[:, l] for l in ln], [e_all[:, l] for l in ln], vs,
                               list(range(heads)), [None] * heads, pm_ref, lowers, L, states, put_state)
        for h, (l, o) in enumerate(zip(ln, outs)):
            gate = z_ref[rows, 2 * width + dk * h:2 * width + dk * (h + 1)].astype(F32)
            mixf[rows, l] = _rms(o) * hnorm_ref[:, l] * _silu(gate)

    xoff, yoff = 3 * width, 4 * width
    xpad[:, 8 - (CONV_K - 1):8, :] = c_out[...]
    xpad[:, 8:, :] = z_ref[:, xoff:xoff + width].astype(F32).reshape(nseq, tb, width)
    xc = cb_ref[...].reshape(1, 1, width) + jnp.zeros((nseq, tb, width), F32)
    for j in range(CONV_K):
        lo = 8 - (CONV_K - 1) + j
        xc = xc + xpad[:, lo:lo + tb, :] * cw_ref[j:j + 1, :].reshape(1, 1, width)
    c_out[...] = xpad[:, tb + 8 - (CONV_K - 1):tb + 8, :]
    xc = xc.reshape(R, width)
    xcb = xc.astype(BF16)
    rg = jax.nn.sigmoid(_dot(xcb, wa_ref[...]) + ba_ref[...])
    ig = jax.nn.sigmoid(_dot(xcb, wx_ref[...]) + bx_ref[...])
    log_a = (-LRU_C) * rg * _softplus(-la_ref[...])
    a = jnp.exp(log_a)
    mult = jnp.sqrt(_neg_expm1(2.0 * log_a))
    row_in_seq = lax.broadcasted_iota(jnp.int32, (R, width), 0) % tb
    if pos0 <= 0 < pos0 + seq:
        mult = jnp.where(row_in_seq + ti * tb + pos0 == 0, 1.0, mult)
    hh = mult * ig * xc
    aa = a
    shift = 1
    while shift < tb:
        m = row_in_seq >= shift
        hh_new = jnp.where(m, aa * pltpu.roll(hh, shift, 0) + hh, hh)
        aa = jnp.where(m, aa * pltpu.roll(aa, shift, 0), aa)
        hh = hh_new
        shift *= 2
    hseq = hh.reshape(nseq, tb, width) + aa.reshape(nseq, tb, width) * h_out[...]
    h_out[...] = hseq[:, tb - 1:tb, :]
    mixf[:, width:2 * width] = hseq.reshape(R, width) * _gelu_tanh(z_ref[:, yoff:yoff + width].astype(F32))
    mix_ref[...] = mixf[...].astype(BF16)


def _even_mixer(z, zg, s_all, h_all, c_all, s_prev, layer_j, batch, seq, nseq, tb, L, pos0, params):
    (hgrn_lb, hgrn_norm, conv_w, conv_b, wa, ba, wx, bx, lru_a) = params
    n, e = z.shape
    n_layers, _, heads, dk, _ = s_all.shape
    width = heads * dk
    R = nseq * tb
    nt = seq // tb
    m3, pm, widths = _level_tables(L)
    const = lambda a: pl.BlockSpec(a.shape, lambda b, t: (0,) * a.ndim)
    lay = lambda a: _layer_spec(a.shape[1:], layer_j, 2)
    aliased = s_prev is not None
    kern = functools.partial(_even_mixer_kernel, layer_j=layer_j, nseq=nseq, tb=tb, seq=seq, L=L, widths=widths,
                             pos0=pos0, heads=heads, width=width, aliased=int(aliased))
    sblk = (None, nseq, heads, dk, dk)
    hblk = (None, nseq, 1, width)
    cblk = (None, nseq, CONV_K - 1, width)
    inputs = [z, zg, s_all, h_all, c_all, m3, pm, hgrn_lb, hgrn_norm, conv_w, conv_b, wa, ba, wx, bx, lru_a]
    in_specs = [
        pl.BlockSpec((R, e), lambda b, t: (b * nt + t, 0)),
        pl.BlockSpec((R, zg.shape[1]), lambda b, t: (b * nt + t, 0)),
        pl.BlockSpec(sblk, lambda b, t: (layer_j, b, 0, 0, 0)),
        pl.BlockSpec(hblk, lambda b, t: (layer_j, b, 0, 0)),
        pl.BlockSpec(cblk, lambda b, t: (layer_j, b, 0, 0)),
        const(m3), const(pm), const(hgrn_lb), lay(hgrn_norm), lay(conv_w), lay(conv_b),
        lay(wa), lay(ba), lay(wx), lay(bx), lay(lru_a),
    ]
    if aliased:
        inputs.append(s_prev)
        in_specs.append(pl.BlockSpec(memory_space=pl.ANY))
    return pl.pallas_call(
        kern,
        out_shape=(
            jax.ShapeDtypeStruct((n, 2 * width), BF16),
            jax.ShapeDtypeStruct((n_layers, batch, heads, dk, dk), F32),
            jax.ShapeDtypeStruct((batch, 1, width), F32),
            jax.ShapeDtypeStruct((batch, CONV_K - 1, width), F32),
        ),
        grid=(batch // nseq, nt),
        in_specs=in_specs,
        out_specs=(
            pl.BlockSpec((R, 2 * width), lambda b, t: (b * nt + t, 0)),
            pl.BlockSpec(sblk, lambda b, t: (layer_j, b, 0, 0, 0)),
            pl.BlockSpec(hblk[1:], lambda b, t: (b, 0, 0)),
            pl.BlockSpec(cblk[1:], lambda b, t: (b, 0, 0)),
        ),
        scratch_shapes=[
            pltpu.VMEM((R, 2 * width), F32),
            pltpu.VMEM((nseq, tb + 8, width), F32),
        ],
        input_output_aliases={len(inputs) - 1: 1} if aliased else {},
        compiler_params=pltpu.CompilerParams(
            dimension_semantics=("parallel", "arbitrary"), vmem_limit_bytes=48 * MIB),
        name="even_mixer",
    )(*inputs)


def _odd_mixer_kernel(*refs, nseq, tb, L, widths, heads, dh, gla_pairs, aliased):
    (z_ref, zg_ref, c_in, n_in, m_in, g_in, m3_ref, pm_ref, ex3_ref, bias_ref, mnorm_ref, wa2_ref, gba_ref,
     gnorm_ref) = refs[:14]
    mix_ref, c_out, n_out, m_out, g_out, mixf = refs[14 + 2 * aliased:]
    ti = pl.program_id(1)
    R = nseq * tb
    cw = heads * dh
    gk = gla_pairs * LANES
    q_off, k_off, v_off, o_off = 0, cw, 2 * cw, 3 * cw
    gq_off = 4 * cw
    gk_off = gq_off + gk
    gv_off = gk_off + gk
    gg_off = gv_off + cw
    gla_dk = gk // heads
    nlev = len(widths)

    @pl.when(ti == 0)
    def _():
        c_out[...] = c_in[...]
        n_out[...] = n_in[...]
        m_out[...] = m_in[...]
        g_out[...] = g_in[...]

    lane = lax.broadcasted_iota(jnp.int32, (1, LANES), 1)
    is_f = (lane >= heads) & (lane < 2 * heads)
    lo_half = lane < LANES // 2
    hi_half = lane >= LANES // 2
    trow = lax.broadcasted_iota(jnp.int32, (GROUP, heads * GROUP), 0)
    tcol = lax.broadcasted_iota(jnp.int32, (GROUP, heads * GROUP), 1) % GROUP
    seg_tril = (tcol <= trow) & ((trow // L) == (tcol // L))
    gla_masks = [lo_half, hi_half]
    pack = functools.partial(_pack_heads, lo_half=lo_half)

    def head_reduce(x, h, fn, fill):
        col = x[:, LANES * (h // 2):LANES * (h // 2 + 1)]
        return fn(jnp.where(lo_half if h % 2 == 0 else hi_half, col, fill), axis=-1, keepdims=True)
    lowers = _lower_masks(widths)

    for tidx in range(R // GROUP):
        r0 = tidx * GROUP
        rows = slice(r0, r0 + GROUP)
        seq_ids = [(r0 + L * j) // tb for j in range(GROUP // L)]
        small = zg_ref[rows, :]
        g0 = small + bias_ref[...]
        pre = _dot(small.astype(BF16), wa2_ref[...]) + gba_ref[...]
        la = _log_sigmoid(pre) * (1.0 / GLA_TAU)
        sums = _level_sums(m3_ref[...], jnp.concatenate([la, _log_sigmoid(g0)], axis=-1))
        e_all = jnp.exp(sums[:, 0:gk])
        gates = jnp.where(is_f, sums[GROUP * nlev:GROUP * (nlev + 1), gk:gk + LANES], g0)

        m_next = [m_out[si] for si in seq_ids]
        ex = _split_dot(gates, ex3_ref[...])
        bcol = [ex[:, dh * h:dh * (h + 1)] for h in range(heads)]
        icol = [ex[:, cw + dh * h:cw + dh * (h + 1)] for h in range(heads)]
        mprev = [[jnp.broadcast_to(mv[:, heads + h:heads + h + 1], (1, dh)) for mv in m_next] for h in range(heads)]
        inter = []
        for h in range(heads):
            parts = [bcol[h][L * j:L * (j + 1)] + mprev[h][j] for j in range(GROUP // L)]
            inter.append(parts[0] if len(parts) == 1 else jnp.concatenate(parts, axis=0))
        gtt = jnp.transpose(gates)
        urow = jnp.concatenate([gtt[h:h + 1, :] - gtt[heads + h:heads + h + 1, :] for h in range(heads)], axis=1)
        dlog = jnp.where(seg_tril, pack(bcol) + urow, NEG_INF)
        m_t = [jnp.maximum(inter[h], head_reduce(dlog, h, jnp.max, NEG_INF)) for h in range(heads)]
        w_intra = jnp.exp(dlog - pack(m_t))
        w_inter = [jnp.exp(inter[h] - m_t[h]) for h in range(heads)]
        q_t = [z_ref[rows, q_off + dh * h:q_off + dh * (h + 1)].astype(F32) for h in range(heads)]
        k_t = [z_ref[rows, k_off + dh * h:k_off + dh * (h + 1)].astype(F32) * (dh ** -0.5) for h in range(heads)]
        v_t = [z_ref[rows, v_off + dh * h:v_off + dh * (h + 1)] for h in range(heads)]
        qk = _dot_nt(jnp.concatenate(q_t, axis=1).astype(BF16), _block_rows(k_t, dh).astype(BF16)) * w_intra
        num_all = _dot(qk.astype(BF16), _block_rows(v_t, dh).astype(BF16))
        for h in range(heads):
            lanes = slice(dh * h, dh * (h + 1))
            qk_sum = head_reduce(qk, h, jnp.sum, 0.0)
            cells = []
            for j in range(GROUP // L):
                sl = slice(L * j, L * (j + 1))
                si = seq_ids[j]
                q, k, vb = q_t[h][sl], k_t[h][sl], v_t[h][sl].astype(BF16)
                cmat = c_out[si, h]
                nrow = n_out[si, :, lanes]
                wi = w_inter[h][sl]
                num = num_all[sl, lanes] + wi * _dot(q.astype(BF16), cmat.astype(BF16))
                den = qk_sum[sl] + wi * jnp.sum(q * nrow, axis=-1, keepdims=True)
                cells.append(num / jnp.maximum(jnp.abs(den), jnp.exp(-m_t[h][sl])))
                b_last = bcol[h][L * j + L - 1:L * j + L, :]
                m_prev = mprev[h][j]
                wlog = b_last - bcol[h][sl] + icol[h][sl]
                m_new = jnp.maximum(b_last + m_prev, jnp.max(wlog, axis=0, keepdims=True))
                w_s = jnp.exp(wlog - m_new)
                decay = jnp.exp(b_last + m_prev - m_new)
                kw = k * w_s
                c_out[si, h] = decay * cmat + _dot_tn(kw.astype(BF16), vb)
                n_out[si, :, lanes] = decay * nrow + jnp.sum(kw, axis=0, keepdims=True)
                m_next[j] = jnp.where(lane == heads + h, m_new, m_next[j])
            hcell = cells[0] if len(cells) == 1 else jnp.concatenate(cells, axis=0)
            ogate = z_ref[rows, o_off + dh * h:o_off + dh * (h + 1)].astype(F32)
            mixf[rows, lanes] = _rms(hcell) * mnorm_ref[:, lanes] * jax.nn.sigmoid(ogate)
        for j, si in enumerate(seq_ids):
            m_out[si] = m_next[j]

        kl = [slice(LANES * p, LANES * (p + 1)) for p in range(gla_pairs)]
        vs = [z_ref[rows, gv_off + dh * h:gv_off + dh * (h + 1)] for h in range(heads)]
        qsc = [z_ref[rows, gq_off + LANES * p:gq_off + LANES * (p + 1)].astype(F32) * (gla_dk ** -0.5)
               for p in range(gla_pairs)]
        kk = [z_ref[rows, gk_off + LANES * p:gk_off + LANES * (p + 1)].astype(F32) for p in range(gla_pairs)]
        states = [[g_out[si, p] for p in range(gla_pairs)] for si in seq_ids]

        def put_state(j, i, new, seq_ids=seq_ids):
            g_out[seq_ids[j], i] = new

        outs = _gla_heads_tile(qsc, kk, [e_all[:, l] for l in kl], vs, [h // 2 for h in range(heads)],
                               [gla_masks[h % 2] for h in range(heads)], pm_ref, lowers, L, states, put_state)
        for h, o in enumerate(outs):
            vl = slice(dh * h, dh * (h + 1))
            gg = z_ref[rows, gg_off + dh * h:gg_off + dh * (h + 1)].astype(F32)
            mixf[rows, cw + dh * h:cw + dh * (h + 1)] = _rms(o) * gnorm_ref[:, vl] * _silu(gg)
    mix_ref[...] = mixf[...].astype(BF16)


def _odd_mixer(z, zg, c_all, n_all, m_all, g_all, c_prev, g_prev, layer_j, batch, seq, nseq, tb, L, params):
    (bias_small, mnorm, wa2_pad, gla_ba, gnorm) = params
    n, e = z.shape
    n_layers, _, heads, dh, _ = c_all.shape
    cw = heads * dh
    gla_pairs = g_all.shape[2]
    R = nseq * tb
    nt = seq // tb
    m3, pm, widths = _level_tables(L)
    const = lambda a: pl.BlockSpec(a.shape, lambda b, t: (0,) * a.ndim)
    lay = lambda a: _layer_spec(a.shape[1:], layer_j, 2)
    aliased = c_prev is not None
    kern = functools.partial(_odd_mixer_kernel, nseq=nseq, tb=tb, L=L, widths=widths, heads=heads, dh=dh,
                             gla_pairs=gla_pairs, aliased=int(aliased))
    cblk = (None, nseq, heads, dh, dh)
    nblk = (None, nseq, 1, cw)
    mblk = (None, nseq, 1, LANES)
    gblk = (None, nseq, gla_pairs, LANES, dh)
    ex3 = _expand_table(heads, dh)
    inputs = [z, zg, c_all, n_all, m_all, g_all, m3, pm, ex3, bias_small, mnorm, wa2_pad, gla_ba, gnorm]
    in_specs = [
        pl.BlockSpec((R, e), lambda b, t: (b * nt + t, 0)),
        pl.BlockSpec((R, zg.shape[1]), lambda b, t: (b * nt + t, 0)),
        pl.BlockSpec(cblk, lambda b, t: (layer_j, b, 0, 0, 0)),
        pl.BlockSpec(nblk, lambda b, t: (layer_j, b, 0, 0)),
        pl.BlockSpec(mblk, lambda b, t: (layer_j, b, 0, 0)),
        pl.BlockSpec(gblk, lambda b, t: (layer_j, b, 0, 0, 0)),
        const(m3), const(pm), const(ex3), lay(bias_small), lay(mnorm), lay(wa2_pad), lay(gla_ba), lay(gnorm),
    ]
    aliases = {}
    if aliased:
        aliases = {len(inputs): 1, len(inputs) + 1: 4}
        inputs += [c_prev, g_prev]
        in_specs += [pl.BlockSpec(memory_space=pl.ANY)] * 2
    return pl.pallas_call(
        kern,
        out_shape=(
            jax.ShapeDtypeStruct((n, 2 * cw), BF16),
            jax.ShapeDtypeStruct((n_layers, batch, heads, dh, dh), F32),
            jax.ShapeDtypeStruct((batch, 1, cw), F32),
            jax.ShapeDtypeStruct((batch, 1, LANES), F32),
            jax.ShapeDtypeStruct((n_layers, batch, gla_pairs, LANES, dh), F32),
        ),
        grid=(batch // nseq, nt),
        in_specs=in_specs,
        out_specs=(
            pl.BlockSpec((R, 2 * cw), lambda b, t: (b * nt + t, 0)),
            pl.BlockSpec(cblk, lambda b, t: (layer_j, b, 0, 0, 0)),
            pl.BlockSpec(nblk[1:], lambda b, t: (b, 0, 0)),
            pl.BlockSpec(mblk[1:], lambda b, t: (b, 0, 0)),
            pl.BlockSpec(gblk, lambda b, t: (layer_j, b, 0, 0, 0)),
        ),
        scratch_shapes=[
            pltpu.VMEM((R, 2 * cw), F32),
        ],
        input_output_aliases=aliases,
        compiler_params=pltpu.CompilerParams(
            dimension_semantics=("parallel", "arbitrary"), vmem_limit_bytes=48 * MIB),
        name="odd_mixer",
    )(*inputs)


def _block_diag(w):
    nb, ci, di = w.shape
    eye = jnp.eye(nb, dtype=w.dtype)
    return (eye[:, None, :, None] * w[:, :, None, :]).reshape(nb * ci, nb * di)


def kernel(x_prompt, x_sample, state_hgrn, state_lru_h, state_lru_conv, state_mlstm_C, state_mlstm_n, state_mlstm_m, state_gla, norm_mix, norm_ffn, norm_final, w_in_even, hgrn_lb, hgrn_norm, lru_conv_w, lru_conv_b, lru_wa, lru_ba, lru_wx, lru_bx, lru_a, w_out_even, w_in_odd, mlstm_bi, mlstm_bf, mlstm_norm, gla_wa2, gla_ba, gla_norm, w_out_odd, ffn_w_up, ffn_w_down):
    batch, seq, d_model = x_prompt.shape
    dec_batch, dec_seq, _ = x_sample.shape
    depth = norm_mix.shape[0]
    n_even, _, a_heads, a_dk, _ = state_hgrn.shape
    n_odd, _, c_heads, c_dh, _ = state_mlstm_C.shape
    b_width = state_lru_h.shape[-1]
    c_width = c_heads * c_dh
    d_heads, d_dk, d_dv = state_gla.shape[2:]
    gk = d_heads * d_dk
    d_width = d_heads * d_dv
    d_rank = gla_wa2.shape[1]
    gla_pairs = gk // LANES
    assert a_heads * a_dk == b_width and GROUP % dec_seq == 0 and seq % (4 * GROUP) == 0

    a_width = a_heads * a_dk
    w_even = jnp.concatenate([w_in_even[:, :, :a_width], w_in_even[:, :, 2 * a_width:],
                              w_in_even[:, :, a_width:2 * a_width]], axis=-1).astype(BF16)
    even_e16 = w_even.shape[-1] - a_width
    main = 4 * c_width
    small_w = 2 * c_heads
    w_odd = jnp.concatenate([
        w_in_odd[:, :, :main],
        w_in_odd[:, :, main + small_w:main + small_w + 2 * gk + 2 * d_width],
        w_in_odd[:, :, main:main + small_w],
        w_in_odd[:, :, main + small_w + 2 * gk + 2 * d_width:],
        jnp.zeros((n_odd, d_model, LANES - small_w - d_rank), F32),
    ], axis=-1).astype(BF16)
    odd_e16 = w_odd.shape[-1] - LANES
    w_out_e = w_out_even.astype(BF16)
    w_out_o = w_out_odd.astype(BF16)
    w_up = ffn_w_up.astype(BF16)
    w_dn = ffn_w_down.astype(BF16)
    wa_dense = jax.vmap(_block_diag)(lru_wa).astype(BF16)
    wx_dense = jax.vmap(_block_diag)(lru_wx).astype(BF16)
    bias_small = jnp.concatenate(
        [mlstm_bi, mlstm_bf, jnp.zeros((n_odd, LANES - small_w), F32)], axis=-1)[:, None, :]
    wa2_pad = jnp.concatenate([
        jnp.zeros((n_odd, small_w, gk), F32), gla_wa2,
        jnp.zeros((n_odd, LANES - small_w - d_rank, gk), F32)], axis=1).astype(BF16)
    rows3 = lambda a: a[:, None, :]
    norm_mix3, norm_ffn3 = rows3(norm_mix), rows3(norm_ffn)
    even_params = (hgrn_lb, rows3(hgrn_norm), lru_conv_w, rows3(lru_conv_b), wa_dense, rows3(lru_ba),
                   wx_dense, rows3(lru_bx), rows3(lru_a))
    odd_params = (bias_small, rows3(mlstm_norm), wa2_pad, rows3(gla_ba), rows3(gla_norm))

    zeros = lambda *s: jnp.zeros(s, F32)
    groups = [
        dict(batch=batch, seq=seq, nseq=1, tb=4 * GROUP, L=math.gcd(seq, CHUNK), pos0=0,
             x=x_prompt.reshape(batch * seq, d_model),
             hgrn=zeros(n_even, batch, a_heads, a_dk, a_dk),
             lru_h=zeros(n_even, batch, 1, b_width),
             lru_conv=zeros(n_even, batch, CONV_K - 1, b_width),
             mC=zeros(n_odd, batch, c_heads, c_dh, c_dh),
             mn=zeros(n_odd, batch, 1, c_width),
             mm=zeros(n_odd, batch, 1, LANES),
             gla=zeros(n_odd, batch, gla_pairs, LANES, d_dv)),
        dict(batch=dec_batch, seq=dec_seq, nseq=GROUP // dec_seq, tb=dec_seq, L=math.gcd(dec_seq, CHUNK),
             pos0=PAST_LEN,
             x=x_sample.reshape(dec_batch * dec_seq, d_model),
             hgrn=state_hgrn,
             lru_h=state_lru_h[:, :, None, :],
             lru_conv=state_lru_conv,
             mC=state_mlstm_C,
             mn=state_mlstm_n.reshape(n_odd, dec_batch, 1, c_width),
             mm=jnp.pad(state_mlstm_m, ((0, 0), (0, 0), (c_heads, LANES - 2 * c_heads)))[:, :, None, :],
             gla=state_gla.reshape(n_odd, dec_batch, gla_pairs, LANES, d_dv)),
    ]

    results = []
    for grp in groups:
        x = grp["x"]
        bsz = grp["batch"]
        hgrn_out = mc_out = gla_out = None
        small = dict(lru_h=[], lru_conv=[], mn=[], mm=[])
        for l in range(depth):
            j = l // 2
            if l % 2 == 0:
                z, zg = _norm_matmul(x, norm_mix3, w_even, l, j, even_e16)
                mix, hgrn_out, h, cb = _even_mixer(z, zg, grp["hgrn"], grp["lru_h"], grp["lru_conv"], hgrn_out, j, bsz,
                                                   grp["seq"], grp["nseq"], grp["tb"], grp["L"], grp["pos0"],
                                                   even_params)
                small["lru_h"].append(h[:, 0, :])
                small["lru_conv"].append(cb)
                w_out = w_out_e
            else:
                z, zg = _norm_matmul(x, norm_mix3, w_odd, l, j, odd_e16)
                mix, mc_out, nn, mm, gla_out = _odd_mixer(z, zg, grp["mC"], grp["mn"], grp["mm"], grp["gla"], mc_out,
                                                          gla_out, j, bsz, grp["seq"], grp["nseq"], grp["tb"],
                                                          grp["L"], odd_params)
                small["mn"].append(nn.reshape(bsz, c_heads, c_dh))
                small["mm"].append(mm[:, 0, c_heads:2 * c_heads])
                w_out = w_out_o
            x = _proj_ffn(x, mix, w_out, norm_ffn3, w_up, w_dn, norm_final[None, :], l, j,
                          final=(l == depth - 1))
        st = {k: jnp.stack(v) for k, v in small.items()}
        st["hgrn"] = hgrn_out
        st["mC"] = mc_out
        st["gla"] = gla_out.reshape(n_odd, bsz, d_heads, d_dk, d_dv)
        results.append((x.reshape(bsz, grp["seq"], d_model), st))

    (y_p, st_p), (y_s, st_s) = results
    return (y_p, y_s, st_p["hgrn"], st_s["hgrn"], st_p["lru_h"], st_s["lru_h"], st_p["lru_conv"], st_s["lru_conv"],
            st_p["mC"], st_s["mC"], st_p["mn"], st_s["mn"], st_p["mm"], st_s["mm"], st_p["gla"], st_s["gla"])
```

```python
import functools
import math

import jax
import jax.numpy as jnp
import numpy as np
from jax import lax
from jax.experimental import pallas as pl
from jax.experimental.pallas import tpu as pltpu

F32 = jnp.float32
BF16 = jnp.bfloat16
EPS = 1e-6
LANES = 128
GROUP = 64
CHUNK = 64
CONV_K = 4
LRU_C = 8.0
GLA_TAU = 16.0
PAST_LEN = 16384
MIB = 1024 * 1024
NEG_INF = float("-inf")


def _log1p_exp_neg_abs(x):
    return jnp.log(1.0 + jnp.exp(-jnp.abs(x)))


def _log_sigmoid(x):
    return jnp.minimum(x, 0.0) - _log1p_exp_neg_abs(x)


def _softplus(x):
    return jnp.maximum(x, 0.0) + _log1p_exp_neg_abs(x)


def _silu(x):
    return x * jax.nn.sigmoid(x)


def _gelu_tanh(x):
    return 0.5 * x * (1.0 + jnp.tanh(math.sqrt(2.0 / math.pi) * (x + 0.044715 * (x * x * x))))


def _rms(x):
    return x * lax.rsqrt(jnp.mean(x * x, axis=-1, keepdims=True) + EPS)


def _dot(a, b):
    return jnp.dot(a, b, preferred_element_type=F32)


def _dot_nt(a, b):
    return lax.dot_general(a, b, (((1,), (1,)), ((), ())), preferred_element_type=F32)


def _dot_tn(a, b):
    return lax.dot_general(a, b, (((0,), (0,)), ((), ())), preferred_element_type=F32)


def _layer_spec(shape, layer, grid_rank, **kw):
    zeros = (0,) * len(shape)
    if grid_rank == 1:
        return pl.BlockSpec((None,) + tuple(shape), lambda i: (layer,) + zeros, **kw)
    return pl.BlockSpec((None,) + tuple(shape), lambda i, j: (layer,) + zeros, **kw)


def _level_tables(L):
    t = np.arange(GROUP)[:, None]
    s = np.arange(GROUP)[None, :]
    widths = []
    w = L // 2
    while w >= 1:
        widths.append(w)
        w //= 2
    mats, pmasks = [], []
    for w in widths:
        same = (t // (2 * w)) == (s // (2 * w))
        low_t, low_s = (t % (2 * w)) >= w, (s % (2 * w)) >= w
        mats.append((same & low_t & low_s & (s <= t)) | (same & ~low_t & ~low_s & (s > t)))
        pmasks.append(same & low_t & ~low_s)
    seg = (t // L) == (s // L)
    mats.append(seg & (s <= t))
    mats.append(seg & (s > t))
    pmasks.append(t == s)
    m_all = np.concatenate(mats, axis=0).astype(np.float32)
    m3 = np.concatenate([m_all, m_all, m_all], axis=1)
    pm = np.stack(pmasks).astype(np.float32)
    return jnp.asarray(m3, BF16), jnp.asarray(np.concatenate([pm, pm, pm, pm], axis=-1)), widths


def _level_sums(m3, g):
    g_hi = g.astype(BF16)
    r1 = g - g_hi.astype(F32)
    g_mid = r1.astype(BF16)
    g_lo = (r1 - g_mid.astype(F32)).astype(BF16)
    return _dot(m3, jnp.concatenate([g_hi, g_mid, g_lo], axis=0))


def _lower_masks(widths):
    row = lax.broadcasted_iota(jnp.int32, (GROUP, LANES), 0)
    return [(row % (2 * w)) >= w for w in widths]


def _split_dot(x, table3):
    x_hi = x.astype(BF16)
    r1 = x - x_hi.astype(F32)
    x_mid = r1.astype(BF16)
    x_lo = (r1 - x_mid.astype(F32)).astype(BF16)
    return _dot(jnp.concatenate([x_hi, x_mid, x_lo], axis=1), table3)


def _expand_table(heads, dh):
    c = np.arange(LANES)[:, None]
    col = np.arange(heads * dh)[None, :] // dh
    t = np.concatenate([c == heads + col, c == col], axis=1).astype(np.float32)
    return jnp.asarray(np.concatenate([t, t, t], axis=0), BF16)


def _block_rows(xs, width):
    zero = jnp.zeros((xs[0].shape[0], width), xs[0].dtype)
    return jnp.concatenate(
        [jnp.concatenate([x if i == h else zero for i in range(len(xs))], axis=1) for h, x in enumerate(xs)], axis=0)


def _pack_heads(cols, lo_half):
    return jnp.concatenate([jnp.where(lo_half, cols[2 * p], cols[2 * p + 1]) for p in range(len(cols) // 2)], axis=1)


def _gla_heads_tile(qs, ks, es, vs, group_of, masks, pm_ref, lowers, L, states, put_state):
    nlev = len(lowers)
    nh, ng = len(vs), len(qs)
    lo_half = lax.broadcasted_iota(jnp.int32, (1, LANES), 1) < GROUP

    def blk(e, i):
        return e[GROUP * i:GROUP * (i + 1), :]

    def own(a, h):
        return a if masks[h] is None else jnp.where(masks[h], a, 0.0)

    def rhs_rows(ys):
        zero = jnp.zeros((GROUP, LANES), F32)
        return jnp.concatenate(
            [jnp.concatenate([own(ys[i], h) if i == group_of[h] else zero for i in range(ng)], axis=1)
             for h in range(nh)], axis=0)

    scores = None
    for lv in range(nlev):
        ys = [jnp.where(lowers[lv], q, k) * blk(e, lv) for q, k, e in zip(qs, ks, es)]
        part = _dot_nt(jnp.concatenate(ys, axis=1).astype(BF16), rhs_rows(ys).astype(BF16)) * pm_ref[lv]
        scores = part if scores is None else scores + part
    qks = [q * k for q, k in zip(qs, ks)]
    d = [jnp.sum(own(qks[group_of[h]], h), axis=-1, keepdims=True) for h in range(nh)]
    diag = pm_ref[nlev] * _pack_heads(d, lo_half)
    p = diag if scores is None else scores + diag
    intra = _dot(p.astype(BF16), _block_rows(vs, LANES).astype(BF16))
    vbs = [v.astype(BF16) for v in vs]
    ebs = [blk(e, nlev) for e in es]
    qes = [q * eb for q, eb in zip(qs, ebs)]
    kds = [k * blk(e, nlev + 1) for k, e in zip(ks, es)]
    pieces = [[] for _ in range(nh)]
    for j in range(GROUP // L):
        sl = slice(L * j, L * (j + 1))
        sbs = [s.astype(BF16) for s in states[j]]
        upds = [None] * ng
        for h in range(nh):
            i = group_of[h]
            pieces[h].append(intra[sl, LANES * h:LANES * (h + 1)] + _dot(own(qes[i][sl], h).astype(BF16), sbs[i]))
            t = _dot_tn(own(kds[i][sl], h).astype(BF16), vbs[h][sl])
            upds[i] = t if upds[i] is None else upds[i] + t
        for i, upd in enumerate(upds):
            last = ebs[i][L * j + L - 1:L * j + L, :]
            decay_col = jnp.transpose(jnp.broadcast_to(last, (8, LANES)))[:, 0:1]
            put_state(j, i, decay_col * states[j][i] + upd)
    return [p[0] if len(p) == 1 else jnp.concatenate(p, axis=0) for p in pieces]


def _norm_matmul_kernel(x_ref, g_ref, w_ref, o16_ref, o32_ref, *, e16, tn):
    def put16(cols, val):
        o16_ref[:, cols] = val

    def put32(val):
        o32_ref[...] = val

    _project_rows(x_ref, g_ref, w_ref, put16, put32, e16, tn)


def _norm_matmul(x, g, w, layer, wlayer, e16, tm=512, tn=512):
    n, d = x.shape
    e = w.shape[-1]
    return pl.pallas_call(
        functools.partial(_norm_matmul_kernel, e16=e16, tn=tn),
        out_shape=(jax.ShapeDtypeStruct((n, e16), BF16), jax.ShapeDtypeStruct((n, e - e16), F32)),
        grid=(n // tm,),
        in_specs=[
            pl.BlockSpec((tm, d), lambda i: (i, 0)),
            _layer_spec((1, d), layer, 1),
            _layer_spec((d, e), wlayer, 1, pipeline_mode=pl.Buffered(1)),
        ],
        out_specs=(pl.BlockSpec((tm, e16), lambda i: (i, 0)), pl.BlockSpec((tm, e - e16), lambda i: (i, 0))),
        compiler_params=pltpu.CompilerParams(
            dimension_semantics=("parallel",), vmem_limit_bytes=48 * MIB),
        name="norm_in_proj",
    )(x, g, w)


def _project_rows(x_ref, g_ref, w_ref, put16, put32, e16, tn=512):
    xn = (_rms(x_ref[...]) * g_ref[...]).astype(BF16)
    for c in range(e16 // tn):
        put16(slice(c * tn, (c + 1) * tn), _dot(xn, w_ref[:, c * tn:(c + 1) * tn]).astype(BF16))
    put32(_dot(xn, w_ref[:, e16:]))


def _z_source(head, bufs, e16):
    if bufs is None:
        return head[0], head[1], lambda: None
    x_first, x_next, g_ref, w_ref = head
    zs, zgs = bufs
    step = pl.program_id(0) * pl.num_programs(1) + pl.program_id(1)
    slot = step % 2

    def project(x_ref, dst):
        def put16(cols, val):
            zs[dst, :, cols] = val

        def put32(val):
            zgs[dst] = val

        _project_rows(x_ref, g_ref, w_ref, put16, put32, e16)

    @pl.when(step == 0)
    def _():
        project(x_first, 0)

    return zs.at[slot], zgs.at[slot], lambda: project(x_next, 1 - slot)


def _z_inputs(src, R, nt, nblocks):
    if src[0] == "z":
        _, z, zg = src
        specs = [pl.BlockSpec((R, z.shape[1]), lambda b, t: (b * nt + t, 0)),
                 pl.BlockSpec((R, zg.shape[1]), lambda b, t: (b * nt + t, 0))]
        return [z, zg], specs, [], z.shape[0], None
    _, x, g3, layer, w, wlayer, e16 = src
    d, e = x.shape[1], w.shape[-1]
    specs = [pl.BlockSpec((R, d), lambda b, t: (0, 0)),
             pl.BlockSpec((R, d), lambda b, t: (jnp.minimum(b * nt + t + 1, nblocks - 1), 0)),
             _layer_spec((1, d), layer, 2),
             _layer_spec((d, e), wlayer, 2, pipeline_mode=pl.Buffered(1))]
    scratch = [pltpu.VMEM((2, R, e16), BF16), pltpu.VMEM((2, R, e - e16), F32)]
    return [x, x, g3, w], specs, scratch, x.shape[0], e16


def _proj_ffn_kernel(x_ref, mix_ref, wo_ref, g_ref, wup_ref, wdn_ref, gf_ref, o_ref, *, d_ff, tf, final):
    xnew = x_ref[...] + _dot(mix_ref[...], wo_ref[...])
    h2 = (_rms(xnew) * g_ref[...]).astype(BF16)
    o_ref[...] = xnew
    for f in range(d_ff // tf):
        gate = _dot(h2, wup_ref[:, f * tf:(f + 1) * tf])
        val = _dot(h2, wup_ref[:, d_ff + f * tf:d_ff + (f + 1) * tf])
        act = (_silu(gate) * val).astype(BF16)
        o_ref[...] += _dot(act, wdn_ref[f * tf:(f + 1) * tf, :])
    if final:
        o_ref[...] = _rms(o_ref[...]) * gf_ref[...]


def _proj_ffn(x, mix, wo, g, wup, wdn, gf, layer, wolayer, final, tm=512, tf=256):
    n, d = x.shape
    d_ff = wdn.shape[-2]
    once = dict(pipeline_mode=pl.Buffered(1))
    return pl.pallas_call(
        functools.partial(_proj_ffn_kernel, d_ff=d_ff, tf=tf, final=final),
        out_shape=jax.ShapeDtypeStruct((n, d), F32),
        grid=(n // tm,),
        in_specs=[
            pl.BlockSpec((tm, d), lambda i: (i, 0)),
            pl.BlockSpec((tm, d), lambda i: (i, 0)),
            _layer_spec((d, d), wolayer, 1, **once),
            _layer_spec((1, d), layer, 1, **once),
            _layer_spec((d, 2 * d_ff), layer, 1, **once),
            _layer_spec((d_ff, d), layer, 1, **once),
            pl.BlockSpec((1, d), lambda i: (0, 0), **once),
        ],
        out_specs=pl.BlockSpec((tm, d), lambda i: (i, 0)),
        compiler_params=pltpu.CompilerParams(
            dimension_semantics=("parallel",), vmem_limit_bytes=52 * MIB),
        name="out_proj_ffn",
    )(x, mix, wo, g, wup, wdn, gf)


def _even_mixer_kernel(*refs, layer_j, nseq, tb, seq, L, widths, pos0, heads, width, aliased, e16):
    nhead = 2 if e16 is None else 4
    (s_in, h_in, c_in, m3_ref, pm_ref, lb_ref, hnorm_ref, cw_ref, cb_ref, wa_ref,
     ba_ref, wx_ref, bx_ref, la_ref) = refs[nhead:nhead + 14]
    mix_ref, s_out, h_out, c_out, mixf, xpad = refs[nhead + 14 + aliased:nhead + 20 + aliased]
    z_ref, zg_ref, project_next = _z_source(refs[:nhead], refs[nhead + 20 + aliased:] or None, e16)
    ti = pl.program_id(1)
    R = nseq * tb
    dk = width // heads

    @pl.when(ti == 0)
    def _():
        s_out[...] = s_in[...]
        h_out[...] = h_in[...]
        c_out[...] = c_in[...]

    hl = lb_ref[...]
    e = jnp.exp(hl - jnp.max(hl, axis=0, keepdims=True))
    sm = e / jnp.sum(e, axis=0, keepdims=True)
    lb = jnp.zeros((1, width), F32)
    for i in range(1, layer_j + 1):
        lb = lb + sm[i:i + 1, :]
    log_lb = jnp.log(lb)
    lowers = _lower_masks(widths)

    for tidx in range(R // GROUP):
        r0 = tidx * GROUP
        rows = slice(r0, r0 + GROUP)
        seq_ids = [(r0 + L * j) // tb for j in range(GROUP // L)]
        x = zg_ref[rows, :]
        y = log_lb - x
        logf = _log_sigmoid(x) + jnp.maximum(y, 0.0) + _log1p_exp_neg_abs(y)
        key = 1.0 - jnp.exp(logf)
        qsc = z_ref[rows, 0:width].astype(F32) * (dk ** -0.5)
        e_all = jnp.exp(_level_sums(m3_ref[...], logf))
        ln = [slice(dk * h, dk * (h + 1)) for h in range(heads)]
        vs = [z_ref[rows, width + dk * h:width + dk * (h + 1)] for h in range(heads)]
        states = [[s_out[si, h] for h in range(heads)] for si in seq_ids]

        def put_state(j, i, new, seq_ids=seq_ids):
            s_out[seq_ids[j], i] = new

        outs = _gla_heads_tile([qsc[:, l] for l in ln], [key[:, l] for l in ln], [e_all[:, l] for l in ln], vs,
                               list(range(heads)), [None] * heads, pm_ref, lowers, L, states, put_state)
        for h, (l, o) in enumerate(zip(ln, outs)):
            gate = z_ref[rows, 2 * width + dk * h:2 * width + dk * (h + 1)].astype(F32)
            mixf[rows, l] = _rms(o) * hnorm_ref[:, l] * _silu(gate)

    xoff, yoff = 3 * width, 4 * width
    xpad[:, 8 - (CONV_K - 1):8, :] = c_out[...]
    xpad[:, 8:, :] = z_ref[:, xoff:xoff + width].astype(F32).reshape(nseq, tb, width)
    xc = cb_ref[...].reshape(1, 1, width) + jnp.zeros((nseq, tb, width), F32)
    for j in range(CONV_K):
        lo = 8 - (CONV_K - 1) + j
        xc = xc + xpad[:, lo:lo + tb, :] * cw_ref[j:j + 1, :].reshape(1, 1, width)
    c_out[...] = xpad[:, tb + 8 - (CONV_K - 1):tb + 8, :]
    xc = xc.reshape(R, width)
    xcb = xc.astype(BF16)
    rg = jax.nn.sigmoid(_dot(xcb, wa_ref[...]) + ba_ref[...])
    ig = jax.nn.sigmoid(_dot(xcb, wx_ref[...]) + bx_ref[...])
    log_a = (-LRU_C) * rg * _softplus(-la_ref[...])
    a = jnp.exp(log_a)
    mult = jnp.sqrt(1.0 - a * a)
    row_in_seq = lax.broadcasted_iota(jnp.int32, (R, width), 0) % tb
    if pos0 <= 0 < pos0 + seq:
        mult = jnp.where(row_in_seq + ti * tb + pos0 == 0, 1.0, mult)
    hh = mult * ig * xc
    aa = a
    shift = 1
    while shift < tb:
        m = row_in_seq >= shift
        hh_new = jnp.where(m, aa * pltpu.roll(hh, shift, 0) + hh, hh)
        aa = jnp.where(m, aa * pltpu.roll(aa, shift, 0), aa)
        hh = hh_new
        shift *= 2
    hseq = hh.reshape(nseq, tb, width) + aa.reshape(nseq, tb, width) * h_out[...]
    h_out[...] = hseq[:, tb - 1:tb, :]
    mixf[:, width:2 * width] = hseq.reshape(R, width) * _gelu_tanh(z_ref[:, yoff:yoff + width].astype(F32))
    mix_ref[...] = mixf[...].astype(BF16)
    project_next()


def _even_mixer(src, s_all, h_all, c_all, s_prev, layer_j, batch, seq, nseq, tb, L, pos0, params):
    (hgrn_lb, hgrn_norm, conv_w, conv_b, wa, ba, wx, bx, lru_a) = params
    n_layers, _, heads, dk, _ = s_all.shape
    width = heads * dk
    R = nseq * tb
    nt = seq // tb
    z_in, z_specs, z_scratch, n, e16 = _z_inputs(src, R, nt, (batch // nseq) * nt)
    m3, pm, widths = _level_tables(L)
    const = lambda a: pl.BlockSpec(a.shape, lambda b, t: (0,) * a.ndim)
    lay = lambda a: _layer_spec(a.shape[1:], layer_j, 2)
    aliased = s_prev is not None
    kern = functools.partial(_even_mixer_kernel, layer_j=layer_j, nseq=nseq, tb=tb, seq=seq, L=L, widths=widths,
                             pos0=pos0, heads=heads, width=width, aliased=int(aliased), e16=e16)
    sblk = (None, nseq, heads, dk, dk)
    hblk = (None, nseq, 1, width)
    cblk = (None, nseq, CONV_K - 1, width)
    inputs = z_in + [s_all, h_all, c_all, m3, pm, hgrn_lb, hgrn_norm, conv_w, conv_b, wa, ba, wx, bx, lru_a]
    in_specs = z_specs + [
        pl.BlockSpec(sblk, lambda b, t: (layer_j, b, 0, 0, 0)),
        pl.BlockSpec(hblk, lambda b, t: (layer_j, b, 0, 0)),
        pl.BlockSpec(cblk, lambda b, t: (layer_j, b, 0, 0)),
        const(m3), const(pm), const(hgrn_lb), lay(hgrn_norm), lay(conv_w), lay(conv_b),
        lay(wa), lay(ba), lay(wx), lay(bx), lay(lru_a),
    ]
    if aliased:
        inputs.append(s_prev)
        in_specs.append(pl.BlockSpec(memory_space=pl.ANY))
    return pl.pallas_call(
        kern,
        out_shape=(
            jax.ShapeDtypeStruct((n, 2 * width), BF16),
            jax.ShapeDtypeStruct((n_layers, batch, heads, dk, dk), F32),
            jax.ShapeDtypeStruct((batch, 1, width), F32),
            jax.ShapeDtypeStruct((batch, CONV_K - 1, width), F32),
        ),
        grid=(batch // nseq, nt),
        in_specs=in_specs,
        out_specs=(
            pl.BlockSpec((R, 2 * width), lambda b, t: (b * nt + t, 0)),
            pl.BlockSpec(sblk, lambda b, t: (layer_j, b, 0, 0, 0)),
            pl.BlockSpec(hblk[1:], lambda b, t: (b, 0, 0)),
            pl.BlockSpec(cblk[1:], lambda b, t: (b, 0, 0)),
        ),
        scratch_shapes=[
            pltpu.VMEM((R, 2 * width), F32),
            pltpu.VMEM((nseq, tb + 8, width), F32),
        ] + z_scratch,
        input_output_aliases={len(inputs) - 1: 1} if aliased else {},
        compiler_params=pltpu.CompilerParams(
            dimension_semantics=("arbitrary", "arbitrary"), vmem_limit_bytes=48 * MIB),
        name="even_mixer",
    )(*inputs)


def _odd_mixer_kernel(*refs, nseq, tb, L, widths, heads, dh, gla_pairs, aliased, e16):
    nhead = 2 if e16 is None else 4
    (c_in, n_in, m_in, g_in, m3_ref, pm_ref, ex3_ref, bias_ref, mnorm_ref, wa2_ref, gba_ref,
     gnorm_ref) = refs[nhead:nhead + 12]
    nout = nhead + 12 + 2 * aliased
    mix_ref, c_out, n_out, m_out, g_out, mixf = refs[nout:nout + 6]
    z_ref, zg_ref, project_next = _z_source(refs[:nhead], refs[nout + 6:] or None, e16)
    ti = pl.program_id(1)
    R = nseq * tb
    cw = heads * dh
    gk = gla_pairs * LANES
    q_off, k_off, v_off, o_off = 0, cw, 2 * cw, 3 * cw
    gq_off = 4 * cw
    gk_off = gq_off + gk
    gv_off = gk_off + gk
    gg_off = gv_off + cw
    gla_dk = gk // heads
    nlev = len(widths)

    @pl.when(ti == 0)
    def _():
        c_out[...] = c_in[...]
        n_out[...] = n_in[...]
        m_out[...] = m_in[...]
        g_out[...] = g_in[...]

    lane = lax.broadcasted_iota(jnp.int32, (1, LANES), 1)
    is_f = (lane >= heads) & (lane < 2 * heads)
    lo_half = lane < LANES // 2
    hi_half = lane >= LANES // 2
    trow = lax.broadcasted_iota(jnp.int32, (GROUP, heads * GROUP), 0)
    tcol = lax.broadcasted_iota(jnp.int32, (GROUP, heads * GROUP), 1) % GROUP
    seg_tril = (tcol <= trow) & ((trow // L) == (tcol // L))
    gla_masks = [lo_half, hi_half]
    pack = functools.partial(_pack_heads, lo_half=lo_half)

    def head_reduce(x, h, fn, fill):
        col = x[:, LANES * (h // 2):LANES * (h // 2 + 1)]
        return fn(jnp.where(lo_half if h % 2 == 0 else hi_half, col, fill), axis=-1, keepdims=True)
    lowers = _lower_masks(widths)

    for tidx in range(R // GROUP):
        r0 = tidx * GROUP
        rows = slice(r0, r0 + GROUP)
        seq_ids = [(r0 + L * j) // tb for j in range(GROUP // L)]
        small = zg_ref[rows, :]
        g0 = small + bias_ref[...]
        pre = _dot(small.astype(BF16), wa2_ref[...]) + gba_ref[...]
        la = _log_sigmoid(pre) * (1.0 / GLA_TAU)
        sums = _level_sums(m3_ref[...], jnp.concatenate([la, _log_sigmoid(g0)], axis=-1))
        e_all = jnp.exp(sums[:, 0:gk])
        gates = jnp.where(is_f, sums[GROUP * nlev:GROUP * (nlev + 1), gk:gk + LANES], g0)

        m_next = [m_out[si] for si in seq_ids]
        ex = _split_dot(gates, ex3_ref[...])
        bcol = [ex[:, dh * h:dh * (h + 1)] for h in range(heads)]
        icol = [ex[:, cw + dh * h:cw + dh * (h + 1)] for h in range(heads)]
        mprev = [[jnp.broadcast_to(mv[:, heads + h:heads + h + 1], (1, dh)) for mv in m_next] for h in range(heads)]
        inter = []
        for h in range(heads):
            parts = [bcol[h][L * j:L * (j + 1)] + mprev[h][j] for j in range(GROUP // L)]
            inter.append(parts[0] if len(parts) == 1 else jnp.concatenate(parts, axis=0))
        gtt = jnp.transpose(gates)
        urow = jnp.concatenate([gtt[h:h + 1, :] - gtt[heads + h:heads + h + 1, :] for h in range(heads)], axis=1)
        dlog = jnp.where(seg_tril, pack(bcol) + urow, NEG_INF)
        m_t = [jnp.maximum(inter[h], head_reduce(dlog, h, jnp.max, NEG_INF)) for h in range(heads)]
        w_intra = jnp.exp(dlog - pack(m_t))
        w_inter = [jnp.exp(inter[h] - m_t[h]) for h in range(heads)]
        q_t = [z_ref[rows, q_off + dh * h:q_off + dh * (h + 1)].astype(F32) for h in range(heads)]
        k_t = [z_ref[rows, k_off + dh * h:k_off + dh * (h + 1)].astype(F32) * (dh ** -0.5) for h in range(heads)]
        v_t = [z_ref[rows, v_off + dh * h:v_off + dh * (h + 1)] for h in range(heads)]
        qk = _dot_nt(jnp.concatenate(q_t, axis=1).astype(BF16), _block_rows(k_t, dh).astype(BF16)) * w_intra
        num_all = _dot(qk.astype(BF16), _block_rows(v_t, dh).astype(BF16))
        for h in range(heads):
            lanes = slice(dh * h, dh * (h + 1))
            qk_sum = head_reduce(qk, h, jnp.sum, 0.0)
            cells = []
            for j in range(GROUP // L):
                sl = slice(L * j, L * (j + 1))
                si = seq_ids[j]
                q, k, vb = q_t[h][sl], k_t[h][sl], v_t[h][sl].astype(BF16)
                cmat = c_out[si, h]
                nrow = n_out[si, :, lanes]
                wi = w_inter[h][sl]
                num = num_all[sl, lanes] + wi * _dot(q.astype(BF16), cmat.astype(BF16))
                den = qk_sum[sl] + wi * jnp.sum(q * nrow, axis=-1, keepdims=True)
                cells.append(num / jnp.maximum(jnp.abs(den), jnp.exp(-m_t[h][sl])))
                b_last = bcol[h][L * j + L - 1:L * j + L, :]
                m_prev = mprev[h][j]
                wlog = b_last - bcol[h][sl] + icol[h][sl]
                m_new = jnp.maximum(b_last + m_prev, jnp.max(wlog, axis=0, keepdims=True))
                w_s = jnp.exp(wlog - m_new)
                decay = jnp.exp(b_last + m_prev - m_new)
                kw = k * w_s
                c_out[si, h] = decay * cmat + _dot_tn(kw.astype(BF16), vb)
                n_out[si, :, lanes] = decay * nrow + jnp.sum(kw, axis=0, keepdims=True)
                m_next[j] = jnp.where(lane == heads + h, m_new, m_next[j])
            hcell = cells[0] if len(cells) == 1 else jnp.concatenate(cells, axis=0)
            ogate = z_ref[rows, o_off + dh * h:o_off + dh * (h + 1)].astype(F32)
            mixf[rows, lanes] = _rms(hcell) * mnorm_ref[:, lanes] * jax.nn.sigmoid(ogate)
        for j, si in enumerate(seq_ids):
            m_out[si] = m_next[j]

        kl = [slice(LANES * p, LANES * (p + 1)) for p in range(gla_pairs)]
        vs = [z_ref[rows, gv_off + dh * h:gv_off + dh * (h + 1)] for h in range(heads)]
        qsc = [z_ref[rows, gq_off + LANES * p:gq_off + LANES * (p + 1)].astype(F32) * (gla_dk ** -0.5)
               for p in range(gla_pairs)]
        kk = [z_ref[rows, gk_off + LANES * p:gk_off + LANES * (p + 1)].astype(F32) for p in range(gla_pairs)]
        states = [[g_out[si, p] for p in range(gla_pairs)] for si in seq_ids]

        def put_state(j, i, new, seq_ids=seq_ids):
            g_out[seq_ids[j], i] = new

        outs = _gla_heads_tile(qsc, kk, [e_all[:, l] for l in kl], vs, [h // 2 for h in range(heads)],
                               [gla_masks[h % 2] for h in range(heads)], pm_ref, lowers, L, states, put_state)
        for h, o in enumerate(outs):
            vl = slice(dh * h, dh * (h + 1))
            gg = z_ref[rows, gg_off + dh * h:gg_off + dh * (h + 1)].astype(F32)
            mixf[rows, cw + dh * h:cw + dh * (h + 1)] = _rms(o) * gnorm_ref[:, vl] * _silu(gg)
    mix_ref[...] = mixf[...].astype(BF16)
    project_next()


def _odd_mixer(src, c_all, n_all, m_all, g_all, c_prev, g_prev, layer_j, batch, seq, nseq, tb, L, params):
    (bias_small, mnorm, wa2_pad, gla_ba, gnorm) = params
    n_layers, _, heads, dh, _ = c_all.shape
    cw = heads * dh
    gla_pairs = g_all.shape[2]
    R = nseq * tb
    nt = seq // tb
    z_in, z_specs, z_scratch, n, e16 = _z_inputs(src, R, nt, (batch // nseq) * nt)
    m3, pm, widths = _level_tables(L)
    const = lambda a: pl.BlockSpec(a.shape, lambda b, t: (0,) * a.ndim)
    lay = lambda a: _layer_spec(a.shape[1:], layer_j, 2)
    aliased = c_prev is not None
    kern = functools.partial(_odd_mixer_kernel, nseq=nseq, tb=tb, L=L, widths=widths, heads=heads, dh=dh,
                             gla_pairs=gla_pairs, aliased=int(aliased), e16=e16)
    cblk = (None, nseq, heads, dh, dh)
    nblk = (None, nseq, 1, cw)
    mblk = (None, nseq, 1, LANES)
    gblk = (None, nseq, gla_pairs, LANES, dh)
    ex3 = _expand_table(heads, dh)
    inputs = z_in + [c_all, n_all, m_all, g_all, m3, pm, ex3, bias_small, mnorm, wa2_pad, gla_ba, gnorm]
    in_specs = z_specs + [
        pl.BlockSpec(cblk, lambda b, t: (layer_j, b, 0, 0, 0)),
        pl.BlockSpec(nblk, lambda b, t: (layer_j, b, 0, 0)),
        pl.BlockSpec(mblk, lambda b, t: (layer_j, b, 0, 0)),
        pl.BlockSpec(gblk, lambda b, t: (layer_j, b, 0, 0, 0)),
        const(m3), const(pm), const(ex3), lay(bias_small), lay(mnorm), lay(wa2_pad), lay(gla_ba), lay(gnorm),
    ]
    aliases = {}
    if aliased:
        aliases = {len(inputs): 1, len(inputs) + 1: 4}
        inputs += [c_prev, g_prev]
        in_specs += [pl.BlockSpec(memory_space=pl.ANY)] * 2
    return pl.pallas_call(
        kern,
        out_shape=(
            jax.ShapeDtypeStruct((n, 2 * cw), BF16),
            jax.ShapeDtypeStruct((n_layers, batch, heads, dh, dh), F32),
            jax.ShapeDtypeStruct((batch, 1, cw), F32),
            jax.ShapeDtypeStruct((batch, 1, LANES), F32),
            jax.ShapeDtypeStruct((n_layers, batch, gla_pairs, LANES, dh), F32),
        ),
        grid=(batch // nseq, nt),
        in_specs=in_specs,
        out_specs=(
            pl.BlockSpec((R, 2 * cw), lambda b, t: (b * nt + t, 0)),
            pl.BlockSpec(cblk, lambda b, t: (layer_j, b, 0, 0, 0)),
            pl.BlockSpec(nblk[1:], lambda b, t: (b, 0, 0)),
            pl.BlockSpec(mblk[1:], lambda b, t: (b, 0, 0)),
            pl.BlockSpec(gblk, lambda b, t: (layer_j, b, 0, 0, 0)),
        ),
        scratch_shapes=[
            pltpu.VMEM((R, 2 * cw), F32),
        ] + z_scratch,
        input_output_aliases=aliases,
        compiler_params=pltpu.CompilerParams(
            dimension_semantics=("arbitrary", "arbitrary"), vmem_limit_bytes=48 * MIB),
        name="odd_mixer",
    )(*inputs)


def _block_diag(w):
    nb, ci, di = w.shape
    eye = jnp.eye(nb, dtype=w.dtype)
    return (eye[:, None, :, None] * w[:, :, None, :]).reshape(nb * ci, nb * di)


def kernel(x_prompt, x_sample, state_hgrn, state_lru_h, state_lru_conv, state_mlstm_C, state_mlstm_n, state_mlstm_m, state_gla, norm_mix, norm_ffn, norm_final, w_in_even, hgrn_lb, hgrn_norm, lru_conv_w, lru_conv_b, lru_wa, lru_ba, lru_wx, lru_bx, lru_a, w_out_even, w_in_odd, mlstm_bi, mlstm_bf, mlstm_norm, gla_wa2, gla_ba, gla_norm, w_out_odd, ffn_w_up, ffn_w_down):
    batch, seq, d_model = x_prompt.shape
    dec_batch, dec_seq, _ = x_sample.shape
    depth = norm_mix.shape[0]
    n_even, _, a_heads, a_dk, _ = state_hgrn.shape
    n_odd, _, c_heads, c_dh, _ = state_mlstm_C.shape
    b_width = state_lru_h.shape[-1]
    c_width = c_heads * c_dh
    d_heads, d_dk, d_dv = state_gla.shape[2:]
    gk = d_heads * d_dk
    d_width = d_heads * d_dv
    d_rank = gla_wa2.shape[1]
    gla_pairs = gk // LANES
    assert a_heads * a_dk == b_width and GROUP % dec_seq == 0 and seq % (4 * GROUP) == 0

    a_width = a_heads * a_dk
    w_even = jnp.concatenate([w_in_even[:, :, :a_width], w_in_even[:, :, 2 * a_width:],
                              w_in_even[:, :, a_width:2 * a_width]], axis=-1).astype(BF16)
    even_e16 = w_even.shape[-1] - a_width
    main = 4 * c_width
    small_w = 2 * c_heads
    w_odd = jnp.concatenate([
        w_in_odd[:, :, :main],
        w_in_odd[:, :, main + small_w:main + small_w + 2 * gk + 2 * d_width],
        w_in_odd[:, :, main:main + small_w],
        w_in_odd[:, :, main + small_w + 2 * gk + 2 * d_width:],
        jnp.zeros((n_odd, d_model, LANES - small_w - d_rank), F32),
    ], axis=-1).astype(BF16)
    odd_e16 = w_odd.shape[-1] - LANES
    w_out_e = w_out_even.astype(BF16)
    w_out_o = w_out_odd.astype(BF16)
    w_up = ffn_w_up.astype(BF16)
    w_dn = ffn_w_down.astype(BF16)
    wa_dense = jax.vmap(_block_diag)(lru_wa).astype(BF16)
    wx_dense = jax.vmap(_block_diag)(lru_wx).astype(BF16)
    bias_small = jnp.concatenate(
        [mlstm_bi, mlstm_bf, jnp.zeros((n_odd, LANES - small_w), F32)], axis=-1)[:, None, :]
    wa2_pad = jnp.concatenate([
        jnp.zeros((n_odd, small_w, gk), F32), gla_wa2,
        jnp.zeros((n_odd, LANES - small_w - d_rank, gk), F32)], axis=1).astype(BF16)
    rows3 = lambda a: a[:, None, :]
    norm_mix3, norm_ffn3 = rows3(norm_mix), rows3(norm_ffn)
    even_params = (hgrn_lb, rows3(hgrn_norm), lru_conv_w, rows3(lru_conv_b), wa_dense, rows3(lru_ba),
                   wx_dense, rows3(lru_bx), rows3(lru_a))
    odd_params = (bias_small, rows3(mlstm_norm), wa2_pad, rows3(gla_ba), rows3(gla_norm))

    zeros = lambda *s: jnp.zeros(s, F32)
    groups = [
        dict(batch=batch, seq=seq, nseq=1, tb=4 * GROUP, L=math.gcd(seq, CHUNK), pos0=0, fuse_proj=True,
             x=x_prompt.reshape(batch * seq, d_model),
             hgrn=zeros(n_even, batch, a_heads, a_dk, a_dk),
             lru_h=zeros(n_even, batch, 1, b_width),
             lru_conv=zeros(n_even, batch, CONV_K - 1, b_width),
             mC=zeros(n_odd, batch, c_heads, c_dh, c_dh),
             mn=zeros(n_odd, batch, 1, c_width),
             mm=zeros(n_odd, batch, 1, LANES),
             gla=zeros(n_odd, batch, gla_pairs, LANES, d_dv)),
        dict(batch=dec_batch, seq=dec_seq, nseq=GROUP // dec_seq, tb=dec_seq, L=math.gcd(dec_seq, CHUNK),
             pos0=PAST_LEN, fuse_proj=False,
             x=x_sample.reshape(dec_batch * dec_seq, d_model),
             hgrn=state_hgrn,
             lru_h=state_lru_h[:, :, None, :],
             lru_conv=state_lru_conv,
             mC=state_mlstm_C,
             mn=state_mlstm_n.reshape(n_odd, dec_batch, 1, c_width),
             mm=jnp.pad(state_mlstm_m, ((0, 0), (0, 0), (c_heads, LANES - 2 * c_heads)))[:, :, None, :],
             gla=state_gla.reshape(n_odd, dec_batch, gla_pairs, LANES, d_dv)),
    ]

    results = []
    for grp in groups:
        x = grp["x"]
        bsz = grp["batch"]
        hgrn_out = mc_out = gla_out = None
        small = dict(lru_h=[], lru_conv=[], mn=[], mm=[])

        def z_src(x, w, l, j, e16, fuse=grp["fuse_proj"]):
            if fuse:
                return ("x", x, norm_mix3, l, w, j, e16)
            return ("z",) + tuple(_norm_matmul(x, norm_mix3, w, l, j, e16))

        for l in range(depth):
            j = l // 2
            if l % 2 == 0:
                src = z_src(x, w_even, l, j, even_e16)
                mix, hgrn_out, h, cb = _even_mixer(src, grp["hgrn"], grp["lru_h"], grp["lru_conv"], hgrn_out, j, bsz,
                                                   grp["seq"], grp["nseq"], grp["tb"], grp["L"], grp["pos0"],
                                                   even_params)
                small["lru_h"].append(h[:, 0, :])
                small["lru_conv"].append(cb)
                w_out = w_out_e
            else:
                src = z_src(x, w_odd, l, j, odd_e16)
                mix, mc_out, nn, mm, gla_out = _odd_mixer(src, grp["mC"], grp["mn"], grp["mm"], grp["gla"], mc_out,
                                                          gla_out, j, bsz, grp["seq"], grp["nseq"], grp["tb"],
                                                          grp["L"], odd_params)
                small["mn"].append(nn.reshape(bsz, c_heads, c_dh))
                small["mm"].append(mm[:, 0, c_heads:2 * c_heads])
                w_out = w_out_o
            x = _proj_ffn(x, mix, w_out, norm_ffn3, w_up, w_dn, norm_final[None, :], l, j,
                          final=(l == depth - 1))
        st = {k: jnp.stack(v) for k, v in small.items()}
        st["hgrn"] = hgrn_out
        st["mC"] = mc_out
        st["gla"] = gla_out.reshape(n_odd, bsz, d_heads, d_dk, d_dv)
        results.append((x.reshape(bsz, grp["seq"], d_model), st))

    (y_p, st_p), (y_s, st_s) = results
    return (y_p, y_s, st_p["hgrn"], st_s["hgrn"], st_p["lru_h"], st_s["lru_h"], st_p["lru_conv"], st_s["lru_conv"],
            st_p["mC"], st_s["mC"], st_p["mn"], st_s["mn"], st_p["mm"], st_s["mm"], st_p["gla"], st_s["gla"])
```

```python
import functools
import math

import jax
import jax.numpy as jnp
import numpy as np
from jax import lax
from jax.experimental import pallas as pl
from jax.experimental.pallas import tpu as pltpu

F32 = jnp.float32
BF16 = jnp.bfloat16
EPS = 1e-6
LANES = 128
GROUP = 64
CHUNK = 64
CONV_K = 4
LRU_C = 8.0
GLA_TAU = 16.0
PAST_LEN = 16384
MIB = 1024 * 1024
V7X_VMEM_BYTES = 64 * MIB
VMEM_LIMIT = 3 * V7X_VMEM_BYTES // 4
VMEM_LIMIT_FFN = 13 * V7X_VMEM_BYTES // 16
NEG_INF = float("-inf")


def _log1p_exp_neg_abs(x):
    return jnp.log(1.0 + jnp.exp(-jnp.abs(x)))


def _log_sigmoid(x):
    return jnp.minimum(x, 0.0) - _log1p_exp_neg_abs(x)


def _softplus(x):
    return jnp.maximum(x, 0.0) + _log1p_exp_neg_abs(x)


def _silu(x):
    return x * jax.nn.sigmoid(x)


def _gelu_tanh(x):
    return 0.5 * x * (1.0 + jnp.tanh(math.sqrt(2.0 / math.pi) * (x + 0.044715 * (x * x * x))))


def _rms(x):
    return x * lax.rsqrt(jnp.mean(x * x, axis=-1, keepdims=True) + EPS)


def _dot(a, b):
    return jnp.dot(a, b, preferred_element_type=F32)


def _dot_nt(a, b):
    return lax.dot_general(a, b, (((1,), (1,)), ((), ())), preferred_element_type=F32)


def _dot_tn(a, b):
    return lax.dot_general(a, b, (((0,), (0,)), ((), ())), preferred_element_type=F32)


def _layer_spec(shape, layer, grid_rank, **kw):
    zeros = (0,) * len(shape)
    if grid_rank == 1:
        return pl.BlockSpec((None,) + tuple(shape), lambda i: (layer,) + zeros, **kw)
    return pl.BlockSpec((None,) + tuple(shape), lambda i, j: (layer,) + zeros, **kw)


def _level_tables(L):
    t = np.arange(GROUP)[:, None]
    s = np.arange(GROUP)[None, :]
    widths = []
    w = L // 2
    while w >= 1:
        widths.append(w)
        w //= 2
    mats, pmasks = [], []
    for w in widths:
        same = (t // (2 * w)) == (s // (2 * w))
        low_t, low_s = (t % (2 * w)) >= w, (s % (2 * w)) >= w
        mats.append((same & low_t & low_s & (s <= t)) | (same & ~low_t & ~low_s & (s > t)))
        pmasks.append(same & low_t & ~low_s)
    seg = (t // L) == (s // L)
    mats.append(seg & (s <= t))
    mats.append(seg & (s > t))
    pmasks.append(t == s)
    m_all = np.concatenate(mats, axis=0).astype(np.float32)
    m3 = np.concatenate([m_all, m_all, m_all], axis=1)
    pm = np.stack(pmasks).astype(np.float32)
    return jnp.asarray(m3, BF16), jnp.asarray(np.concatenate([pm, pm, pm, pm], axis=-1)), widths


def _level_sums(m3, g):
    g_hi = g.astype(BF16)
    r1 = g - g_hi.astype(F32)
    g_mid = r1.astype(BF16)
    g_lo = (r1 - g_mid.astype(F32)).astype(BF16)
    return _dot(m3, jnp.concatenate([g_hi, g_mid, g_lo], axis=0))


def _lower_masks(widths):
    row = lax.broadcasted_iota(jnp.int32, (GROUP, LANES), 0)
    return [(row % (2 * w)) >= w for w in widths]


def _split_dot(x, table3):
    x_hi = x.astype(BF16)
    r1 = x - x_hi.astype(F32)
    x_mid = r1.astype(BF16)
    x_lo = (r1 - x_mid.astype(F32)).astype(BF16)
    return _dot(jnp.concatenate([x_hi, x_mid, x_lo], axis=1), table3)


def _expand_table(heads, dh):
    c = np.arange(LANES)[:, None]
    col = np.arange(heads * dh)[None, :] // dh
    t = np.concatenate([c == heads + col, c == col], axis=1).astype(np.float32)
    return jnp.asarray(np.concatenate([t, t, t], axis=0), BF16)


def _block_rows(xs, width):
    zero = jnp.zeros((xs[0].shape[0], width), xs[0].dtype)
    return jnp.concatenate(
        [jnp.concatenate([x if i == h else zero for i in range(len(xs))], axis=1) for h, x in enumerate(xs)], axis=0)


def _pack_heads(cols, lo_half):
    return jnp.concatenate([jnp.where(lo_half, cols[2 * p], cols[2 * p + 1]) for p in range(len(cols) // 2)], axis=1)


def _gla_heads_tile(qs, ks, es, vs, group_of, masks, pm_ref, lowers, L, states, put_state):
    nlev = len(lowers)
    nh, ng = len(vs), len(qs)
    lo_half = lax.broadcasted_iota(jnp.int32, (1, LANES), 1) < GROUP

    def blk(e, i):
        return e[GROUP * i:GROUP * (i + 1), :]

    def own(a, h):
        return a if masks[h] is None else jnp.where(masks[h], a, 0.0)

    def rhs_rows(ys):
        zero = jnp.zeros((GROUP, LANES), F32)
        return jnp.concatenate(
            [jnp.concatenate([own(ys[i], h) if i == group_of[h] else zero for i in range(ng)], axis=1)
             for h in range(nh)], axis=0)

    scores = None
    for lv in range(nlev):
        ys = [jnp.where(lowers[lv], q, k) * blk(e, lv) for q, k, e in zip(qs, ks, es)]
        part = _dot_nt(jnp.concatenate(ys, axis=1).astype(BF16), rhs_rows(ys).astype(BF16)) * pm_ref[lv]
        scores = part if scores is None else scores + part
    qks = [q * k for q, k in zip(qs, ks)]
    d = [jnp.sum(own(qks[group_of[h]], h), axis=-1, keepdims=True) for h in range(nh)]
    diag = pm_ref[nlev] * _pack_heads(d, lo_half)
    p = diag if scores is None else scores + diag
    intra = _dot(p.astype(BF16), _block_rows(vs, LANES).astype(BF16))
    vbs = [v.astype(BF16) for v in vs]
    ebs = [blk(e, nlev) for e in es]
    qes = [q * eb for q, eb in zip(qs, ebs)]
    kds = [k * blk(e, nlev + 1) for k, e in zip(ks, es)]
    pieces = [[] for _ in range(nh)]
    for j in range(GROUP // L):
        sl = slice(L * j, L * (j + 1))
        sbs = [s.astype(BF16) for s in states[j]]
        upds = [None] * ng
        for h in range(nh):
            i = group_of[h]
            pieces[h].append(intra[sl, LANES * h:LANES * (h + 1)] + _dot(own(qes[i][sl], h).astype(BF16), sbs[i]))
            t = _dot_tn(own(kds[i][sl], h).astype(BF16), vbs[h][sl])
            upds[i] = t if upds[i] is None else upds[i] + t
        for i, upd in enumerate(upds):
            last = ebs[i][L * j + L - 1:L * j + L, :]
            decay_col = jnp.transpose(jnp.broadcast_to(last, (8, LANES)))[:, 0:1]
            put_state(j, i, decay_col * states[j][i] + upd)
    return [p[0] if len(p) == 1 else jnp.concatenate(p, axis=0) for p in pieces]


def _norm_matmul_kernel(x_ref, g_ref, w_ref, o16_ref, o32_ref, *, e16, tn):
    def put16(cols, val):
        o16_ref[:, cols] = val

    def put32(val):
        o32_ref[...] = val

    _project_rows(x_ref, g_ref, w_ref, put16, put32, e16, tn)


def _norm_matmul(x, g, w, layer, wlayer, e16, tm=512, tn=512):
    n, d = x.shape
    e = w.shape[-1]
    return pl.pallas_call(
        functools.partial(_norm_matmul_kernel, e16=e16, tn=tn),
        out_shape=(jax.ShapeDtypeStruct((n, e16), BF16), jax.ShapeDtypeStruct((n, e - e16), F32)),
        grid=(n // tm,),
        in_specs=[
            pl.BlockSpec((tm, d), lambda i: (i, 0)),
            _layer_spec((1, d), layer, 1),
            _layer_spec((d, e), wlayer, 1, pipeline_mode=pl.Buffered(1)),
        ],
        out_specs=(pl.BlockSpec((tm, e16), lambda i: (i, 0)), pl.BlockSpec((tm, e - e16), lambda i: (i, 0))),
        compiler_params=pltpu.CompilerParams(
            dimension_semantics=("parallel",), vmem_limit_bytes=VMEM_LIMIT),
        name="norm_in_proj",
    )(x, g, w)


def _project_rows(x_ref, g_ref, w_ref, put16, put32, e16, tn=512):
    xn = (_rms(x_ref[...]) * g_ref[...]).astype(BF16)
    for c in range(e16 // tn):
        put16(slice(c * tn, (c + 1) * tn), _dot(xn, w_ref[:, c * tn:(c + 1) * tn]).astype(BF16))
    put32(_dot(xn, w_ref[:, e16:]))


def _z_source(head, bufs, e16):
    if bufs is None:
        return head[0], head[1], lambda: None
    x_first, x_next, g_ref, w_ref = head
    zs, zgs = bufs
    step = pl.program_id(0) * pl.num_programs(1) + pl.program_id(1)
    slot = step % 2

    def project(x_ref, dst):
        def put16(cols, val):
            zs[dst, :, cols] = val

        def put32(val):
            zgs[dst] = val

        _project_rows(x_ref, g_ref, w_ref, put16, put32, e16)

    @pl.when(step == 0)
    def _():
        project(x_first, 0)

    return zs.at[slot], zgs.at[slot], lambda: project(x_next, 1 - slot)


def _z_inputs(src, R, nt, nblocks):
    if src[0] == "z":
        _, z, zg = src
        specs = [pl.BlockSpec((R, z.shape[1]), lambda b, t: (b * nt + t, 0)),
                 pl.BlockSpec((R, zg.shape[1]), lambda b, t: (b * nt + t, 0))]
        return [z, zg], specs, [], z.shape[0], None
    _, x, g3, layer, w, wlayer, e16 = src
    d, e = x.shape[1], w.shape[-1]
    specs = [pl.BlockSpec((R, d), lambda b, t: (0, 0)),
             pl.BlockSpec((R, d), lambda b, t: (jnp.minimum(b * nt + t + 1, nblocks - 1), 0)),
             _layer_spec((1, d), layer, 2),
             _layer_spec((d, e), wlayer, 2, pipeline_mode=pl.Buffered(1))]
    scratch = [pltpu.VMEM((2, R, e16), BF16), pltpu.VMEM((2, R, e - e16), F32)]
    return [x, x, g3, w], specs, scratch, x.shape[0], e16


def _proj_ffn_kernel(x_ref, mix_ref, wo_ref, g_ref, wup_ref, wdn_ref, gf_ref, o_ref, *, d_ff, tf, final):
    xnew = x_ref[...] + _dot(mix_ref[...], wo_ref[...])
    h2 = (_rms(xnew) * g_ref[...]).astype(BF16)
    o_ref[...] = xnew
    for f in range(d_ff // tf):
        gate = _dot(h2, wup_ref[:, f * tf:(f + 1) * tf])
        val = _dot(h2, wup_ref[:, d_ff + f * tf:d_ff + (f + 1) * tf])
        act = (_silu(gate) * val).astype(BF16)
        o_ref[...] += _dot(act, wdn_ref[f * tf:(f + 1) * tf, :])
    if final:
        o_ref[...] = _rms(o_ref[...]) * gf_ref[...]


def _proj_ffn(x, mix, wo, g, wup, wdn, gf, layer, wolayer, final, tm=512, tf=256):
    n, d = x.shape
    d_ff = wdn.shape[-2]
    once = dict(pipeline_mode=pl.Buffered(1))
    return pl.pallas_call(
        functools.partial(_proj_ffn_kernel, d_ff=d_ff, tf=tf, final=final),
        out_shape=jax.ShapeDtypeStruct((n, d), F32),
        grid=(n // tm,),
        in_specs=[
            pl.BlockSpec((tm, d), lambda i: (i, 0)),
            pl.BlockSpec((tm, d), lambda i: (i, 0)),
            _layer_spec((d, d), wolayer, 1, **once),
            _layer_spec((1, d), layer, 1, **once),
            _layer_spec((d, 2 * d_ff), layer, 1, **once),
            _layer_spec((d_ff, d), layer, 1, **once),
            pl.BlockSpec((1, d), lambda i: (0, 0), **once),
        ],
        out_specs=pl.BlockSpec((tm, d), lambda i: (i, 0)),
        compiler_params=pltpu.CompilerParams(
            dimension_semantics=("parallel",), vmem_limit_bytes=VMEM_LIMIT_FFN),
        name="out_proj_ffn",
    )(x, mix, wo, g, wup, wdn, gf)


def _even_mixer_kernel(*refs, layer_j, nseq, tb, seq, L, widths, pos0, heads, width, nprev, e16):
    nhead = 2 if e16 is None else 4
    (s_in, h_in, c_in, m3_ref, pm_ref, lb_ref, hnorm_ref, cw_ref, cb_ref, wa_ref,
     ba_ref, wx_ref, bx_ref, la_ref) = refs[nhead:nhead + 14]
    nout = nhead + 14 + nprev
    prev_states = refs[nhead + 14:nout]
    mix_ref, s_stack, h_out, c_out, mixf, xpad = refs[nout:nout + 6]
    z_ref, zg_ref, project_next = _z_source(refs[:nhead], refs[nout + 6:] or None, e16)
    s_out = s_stack.at[layer_j] if nprev else s_stack
    ti = pl.program_id(1)
    R = nseq * tb
    dk = width // heads

    @pl.when(ti == 0)
    def _():
        for i, prev in enumerate(prev_states):
            s_stack[i] = prev[...]
        s_out[...] = s_in[...]
        h_out[...] = h_in[...]
        c_out[...] = c_in[...]

    hl = lb_ref[...]
    e = jnp.exp(hl - jnp.max(hl, axis=0, keepdims=True))
    sm = e / jnp.sum(e, axis=0, keepdims=True)
    lb = jnp.zeros((1, width), F32)
    for i in range(1, layer_j + 1):
        lb = lb + sm[i:i + 1, :]
    log_lb = jnp.log(lb)
    lowers = _lower_masks(widths)

    for tidx in range(R // GROUP):
        r0 = tidx * GROUP
        rows = slice(r0, r0 + GROUP)
        seq_ids = [(r0 + L * j) // tb for j in range(GROUP // L)]
        x = zg_ref[rows, :]
        y = log_lb - x
        logf = _log_sigmoid(x) + jnp.maximum(y, 0.0) + _log1p_exp_neg_abs(y)
        key = 1.0 - jnp.exp(logf)
        qsc = z_ref[rows, 0:width].astype(F32) * (dk ** -0.5)
        e_all = jnp.exp(_level_sums(m3_ref[...], logf))
        ln = [slice(dk * h, dk * (h + 1)) for h in range(heads)]
        vs = [z_ref[rows, width + dk * h:width + dk * (h + 1)] for h in range(heads)]
        states = [[s_out[si, h] for h in range(heads)] for si in seq_ids]

        def put_state(j, i, new, seq_ids=seq_ids):
            s_out[seq_ids[j], i] = new

        outs = _gla_heads_tile([qsc[:, l] for l in ln], [key[:, l] for l in ln], [e_all[:, l] for l in ln], vs,
                               list(range(heads)), [None] * heads, pm_ref, lowers, L, states, put_state)
        for h, (l, o) in enumerate(zip(ln, outs)):
            gate = z_ref[rows, 2 * width + dk * h:2 * width + dk * (h + 1)].astype(F32)
            mixf[rows, l] = _rms(o) * hnorm_ref[:, l] * _silu(gate)

    xoff, yoff = 3 * width, 4 * width
    xpad[:, 8 - (CONV_K - 1):8, :] = c_out[...]
    xpad[:, 8:, :] = z_ref[:, xoff:xoff + width].astype(F32).reshape(nseq, tb, width)
    xc = cb_ref[...].reshape(1, 1, width) + jnp.zeros((nseq, tb, width), F32)
    for j in range(CONV_K):
        lo = 8 - (CONV_K - 1) + j
        xc = xc + xpad[:, lo:lo + tb, :] * cw_ref[j:j + 1, :].reshape(1, 1, width)
    c_out[...] = xpad[:, tb + 8 - (CONV_K - 1):tb + 8, :]
    xc = xc.reshape(R, width)
    xcb = xc.astype(BF16)
    rg = jax.nn.sigmoid(_dot(xcb, wa_ref[...]) + ba_ref[...])
    ig = jax.nn.sigmoid(_dot(xcb, wx_ref[...]) + bx_ref[...])
    log_a = (-LRU_C) * rg * _softplus(-la_ref[...])
    a = jnp.exp(log_a)
    mult = jnp.sqrt(1.0 - a * a)
    row_in_seq = lax.broadcasted_iota(jnp.int32, (R, width), 0) % tb
    if pos0 <= 0 < pos0 + seq:
        mult = jnp.where(row_in_seq + ti * tb + pos0 == 0, 1.0, mult)
    hh = mult * ig * xc
    aa = a
    shift = 1
    while shift < tb:
        m = row_in_seq >= shift
        hh_new = jnp.where(m, aa * pltpu.roll(hh, shift, 0) + hh, hh)
        aa = jnp.where(m, aa * pltpu.roll(aa, shift, 0), aa)
        hh = hh_new
        shift *= 2
    hseq = hh.reshape(nseq, tb, width) + aa.reshape(nseq, tb, width) * h_out[...]
    h_out[...] = hseq[:, tb - 1:tb, :]
    mixf[:, width:2 * width] = hseq.reshape(R, width) * _gelu_tanh(z_ref[:, yoff:yoff + width].astype(F32))
    mix_ref[...] = mixf[...].astype(BF16)
    project_next()


def _state_out(prevs, n_layers, batch, nseq, tail):
    zeros = (0,) * len(tail)
    if prevs:
        return (jax.ShapeDtypeStruct((n_layers, batch) + tail, F32),
                pl.BlockSpec((n_layers, nseq) + tail, lambda b, t: (0, b) + zeros))
    return (jax.ShapeDtypeStruct((batch,) + tail, F32), pl.BlockSpec((nseq,) + tail, lambda b, t: (b,) + zeros))


def _even_mixer(src, s_all, h_all, c_all, s_prevs, layer_j, batch, seq, nseq, tb, L, pos0, params):
    (hgrn_lb, hgrn_norm, conv_w, conv_b, wa, ba, wx, bx, lru_a) = params
    n_layers, _, heads, dk, _ = s_all.shape
    width = heads * dk
    R = nseq * tb
    nt = seq // tb
    z_in, z_specs, z_scratch, n, e16 = _z_inputs(src, R, nt, (batch // nseq) * nt)
    m3, pm, widths = _level_tables(L)
    const = lambda a: pl.BlockSpec(a.shape, lambda b, t: (0,) * a.ndim)
    lay = lambda a: _layer_spec(a.shape[1:], layer_j, 2)
    kern = functools.partial(_even_mixer_kernel, layer_j=layer_j, nseq=nseq, tb=tb, seq=seq, L=L, widths=widths,
                             pos0=pos0, heads=heads, width=width, nprev=len(s_prevs), e16=e16)
    s_shape, s_spec = _state_out(s_prevs, n_layers, batch, nseq, (heads, dk, dk))
    sblk = (None, nseq, heads, dk, dk)
    hblk = (None, nseq, 1, width)
    cblk = (None, nseq, CONV_K - 1, width)
    inputs = z_in + [s_all, h_all, c_all, m3, pm, hgrn_lb, hgrn_norm, conv_w, conv_b, wa, ba, wx, bx, lru_a]
    in_specs = z_specs + [
        pl.BlockSpec(sblk, lambda b, t: (layer_j, b, 0, 0, 0)),
        pl.BlockSpec(hblk, lambda b, t: (layer_j, b, 0, 0)),
        pl.BlockSpec(cblk, lambda b, t: (layer_j, b, 0, 0)),
        const(m3), const(pm), const(hgrn_lb), lay(hgrn_norm), lay(conv_w), lay(conv_b),
        lay(wa), lay(ba), lay(wx), lay(bx), lay(lru_a),
    ]
    inputs += list(s_prevs)
    in_specs += [pl.BlockSpec(sblk[1:], lambda b, t: (b, 0, 0, 0))] * len(s_prevs)
    return pl.pallas_call(
        kern,
        out_shape=(
            jax.ShapeDtypeStruct((n, 2 * width), BF16),
            s_shape,
            jax.ShapeDtypeStruct((batch, 1, width), F32),
            jax.ShapeDtypeStruct((batch, CONV_K - 1, width), F32),
        ),
        grid=(batch // nseq, nt),
        in_specs=in_specs,
        out_specs=(
            pl.BlockSpec((R, 2 * width), lambda b, t: (b * nt + t, 0)),
            s_spec,
            pl.BlockSpec(hblk[1:], lambda b, t: (b, 0, 0)),
            pl.BlockSpec(cblk[1:], lambda b, t: (b, 0, 0)),
        ),
        scratch_shapes=[
            pltpu.VMEM((R, 2 * width), F32),
            pltpu.VMEM((nseq, tb + 8, width), F32),
        ] + z_scratch,
        compiler_params=pltpu.CompilerParams(
            dimension_semantics=("arbitrary", "arbitrary"), vmem_limit_bytes=VMEM_LIMIT),
        name="even_mixer",
    )(*inputs)


def _odd_mixer_kernel(*refs, layer_j, nseq, tb, L, widths, heads, dh, gla_pairs, nprev, e16):
    nhead = 2 if e16 is None else 4
    (c_in, n_in, m_in, g_in, m3_ref, pm_ref, ex3_ref, bias_ref, mnorm_ref, wa2_ref, gba_ref,
     gnorm_ref) = refs[nhead:nhead + 12]
    nout = nhead + 12 + 2 * nprev
    prev_c, prev_g = refs[nhead + 12:nhead + 12 + nprev], refs[nhead + 12 + nprev:nout]
    mix_ref, c_stack, n_out, m_out, g_stack, mixf = refs[nout:nout + 6]
    z_ref, zg_ref, project_next = _z_source(refs[:nhead], refs[nout + 6:] or None, e16)
    c_out = c_stack.at[layer_j] if nprev else c_stack
    g_out = g_stack.at[layer_j] if nprev else g_stack
    ti = pl.program_id(1)
    R = nseq * tb
    cw = heads * dh
    gk = gla_pairs * LANES
    q_off, k_off, v_off, o_off = 0, cw, 2 * cw, 3 * cw
    gq_off = 4 * cw
    gk_off = gq_off + gk
    gv_off = gk_off + gk
    gg_off = gv_off + cw
    gla_dk = gk // heads
    nlev = len(widths)

    @pl.when(ti == 0)
    def _():
        for i in range(nprev):
            c_stack[i] = prev_c[i][...]
            g_stack[i] = prev_g[i][...]
        c_out[...] = c_in[...]
        n_out[...] = n_in[...]
        m_out[...] = m_in[...]
        g_out[...] = g_in[...]

    lane = lax.broadcasted_iota(jnp.int32, (1, LANES), 1)
    is_f = (lane >= heads) & (lane < 2 * heads)
    lo_half = lane < LANES // 2
    hi_half = lane >= LANES // 2
    trow = lax.broadcasted_iota(jnp.int32, (GROUP, heads * GROUP), 0)
    tcol = lax.broadcasted_iota(jnp.int32, (GROUP, heads * GROUP), 1) % GROUP
    seg_tril = (tcol <= trow) & ((trow // L) == (tcol // L))
    gla_masks = [lo_half, hi_half]
    pack = functools.partial(_pack_heads, lo_half=lo_half)

    def head_reduce(x, h, fn, fill):
        col = x[:, LANES * (h // 2):LANES * (h // 2 + 1)]
        return fn(jnp.where(lo_half if h % 2 == 0 else hi_half, col, fill), axis=-1, keepdims=True)
    lowers = _lower_masks(widths)

    for tidx in range(R // GROUP):
        r0 = tidx * GROUP
        rows = slice(r0, r0 + GROUP)
        seq_ids = [(r0 + L * j) // tb for j in range(GROUP // L)]
        small = zg_ref[rows, :]
        g0 = small + bias_ref[...]
        pre = _dot(small.astype(BF16), wa2_ref[...]) + gba_ref[...]
        la = _log_sigmoid(pre) * (1.0 / GLA_TAU)
        sums = _level_sums(m3_ref[...], jnp.concatenate([la, _log_sigmoid(g0)], axis=-1))
        e_all = jnp.exp(sums[:, 0:gk])
        gates = jnp.where(is_f, sums[GROUP * nlev:GROUP * (nlev + 1), gk:gk + LANES], g0)

        kl = [slice(LANES * p, LANES * (p + 1)) for p in range(gla_pairs)]
        vs = [z_ref[rows, gv_off + dh * h:gv_off + dh * (h + 1)] for h in range(heads)]
        qsc = [z_ref[rows, gq_off + LANES * p:gq_off + LANES * (p + 1)].astype(F32) * (gla_dk ** -0.5)
               for p in range(gla_pairs)]
        kk = [z_ref[rows, gk_off + LANES * p:gk_off + LANES * (p + 1)].astype(F32) for p in range(gla_pairs)]
        states = [[g_out[si, p] for p in range(gla_pairs)] for si in seq_ids]

        def put_state(j, i, new, seq_ids=seq_ids):
            g_out[seq_ids[j], i] = new

        outs = _gla_heads_tile(qsc, kk, [e_all[:, l] for l in kl], vs, [h // 2 for h in range(heads)],
                               [gla_masks[h % 2] for h in range(heads)], pm_ref, lowers, L, states, put_state)
        for h, o in enumerate(outs):
            vl = slice(dh * h, dh * (h + 1))
            gg = z_ref[rows, gg_off + dh * h:gg_off + dh * (h + 1)].astype(F32)
            mixf[rows, cw + dh * h:cw + dh * (h + 1)] = _rms(o) * gnorm_ref[:, vl] * _silu(gg)

        m_next = [m_out[si] for si in seq_ids]
        ex = _split_dot(gates, ex3_ref[...])
        bcol = [ex[:, dh * h:dh * (h + 1)] for h in range(heads)]
        icol = [ex[:, cw + dh * h:cw + dh * (h + 1)] for h in range(heads)]
        mprev = [[jnp.broadcast_to(mv[:, heads + h:heads + h + 1], (1, dh)) for mv in m_next] for h in range(heads)]
        inter = []
        for h in range(heads):
            parts = [bcol[h][L * j:L * (j + 1)] + mprev[h][j] for j in range(GROUP // L)]
            inter.append(parts[0] if len(parts) == 1 else jnp.concatenate(parts, axis=0))
        gtt = jnp.transpose(gates)
        urow = jnp.concatenate([gtt[h:h + 1, :] - gtt[heads + h:heads + h + 1, :] for h in range(heads)], axis=1)
        dlog = jnp.where(seg_tril, pack(bcol) + urow, NEG_INF)
        m_t = [jnp.maximum(inter[h], head_reduce(dlog, h, jnp.max, NEG_INF)) for h in range(heads)]
        w_intra = jnp.exp(dlog - pack(m_t))
        w_inter = [jnp.exp(inter[h] - m_t[h]) for h in range(heads)]
        q_t = [z_ref[rows, q_off + dh * h:q_off + dh * (h + 1)].astype(F32) for h in range(heads)]
        k_t = [z_ref[rows, k_off + dh * h:k_off + dh * (h + 1)].astype(F32) * (dh ** -0.5) for h in range(heads)]
        v_t = [z_ref[rows, v_off + dh * h:v_off + dh * (h + 1)] for h in range(heads)]
        qk = _dot_nt(jnp.concatenate(q_t, axis=1).astype(BF16), _block_rows(k_t, dh).astype(BF16)) * w_intra
        num_all = _dot(qk.astype(BF16), _block_rows(v_t, dh).astype(BF16))
        for h in range(heads):
            lanes = slice(dh * h, dh * (h + 1))
            qk_sum = head_reduce(qk, h, jnp.sum, 0.0)
            cells = []
            for j in range(GROUP // L):
                sl = slice(L * j, L * (j + 1))
                si = seq_ids[j]
                q, k, vb = q_t[h][sl], k_t[h][sl], v_t[h][sl].astype(BF16)
                cmat = c_out[si, h]
                nrow = n_out[si, :, lanes]
                wi = w_inter[h][sl]
                num = num_all[sl, lanes] + wi * _dot(q.astype(BF16), cmat.astype(BF16))
                den = qk_sum[sl] + wi * jnp.sum(q * nrow, axis=-1, keepdims=True)
                cells.append(num / jnp.maximum(jnp.abs(den), jnp.exp(-m_t[h][sl])))
                b_last = bcol[h][L * j + L - 1:L * j + L, :]
                m_prev = mprev[h][j]
                wlog = b_last - bcol[h][sl] + icol[h][sl]
                m_new = jnp.maximum(b_last + m_prev, jnp.max(wlog, axis=0, keepdims=True))
                w_s = jnp.exp(wlog - m_new)
                decay = jnp.exp(b_last + m_prev - m_new)
                kw = k * w_s
                c_out[si, h] = decay * cmat + _dot_tn(kw.astype(BF16), vb)
                n_out[si, :, lanes] = decay * nrow + jnp.sum(kw, axis=0, keepdims=True)
                m_next[j] = jnp.where(lane == heads + h, m_new, m_next[j])
            hcell = cells[0] if len(cells) == 1 else jnp.concatenate(cells, axis=0)
            ogate = z_ref[rows, o_off + dh * h:o_off + dh * (h + 1)].astype(F32)
            mixf[rows, lanes] = _rms(hcell) * mnorm_ref[:, lanes] * jax.nn.sigmoid(ogate)
        for j, si in enumerate(seq_ids):
            m_out[si] = m_next[j]
    mix_ref[...] = mixf[...].astype(BF16)
    project_next()


def _odd_mixer(src, c_all, n_all, m_all, g_all, c_prevs, g_prevs, layer_j, batch, seq, nseq, tb, L, params):
    (bias_small, mnorm, wa2_pad, gla_ba, gnorm) = params
    n_layers, _, heads, dh, _ = c_all.shape
    cw = heads * dh
    gla_pairs = g_all.shape[2]
    R = nseq * tb
    nt = seq // tb
    z_in, z_specs, z_scratch, n, e16 = _z_inputs(src, R, nt, (batch // nseq) * nt)
    m3, pm, widths = _level_tables(L)
    const = lambda a: pl.BlockSpec(a.shape, lambda b, t: (0,) * a.ndim)
    lay = lambda a: _layer_spec(a.shape[1:], layer_j, 2)
    kern = functools.partial(_odd_mixer_kernel, layer_j=layer_j, nseq=nseq, tb=tb, L=L, widths=widths, heads=heads,
                             dh=dh, gla_pairs=gla_pairs, nprev=len(c_prevs), e16=e16)
    c_shape, c_spec = _state_out(c_prevs, n_layers, batch, nseq, (heads, dh, dh))
    g_shape, g_spec = _state_out(g_prevs, n_layers, batch, nseq, (gla_pairs, LANES, dh))
    cblk = (None, nseq, heads, dh, dh)
    nblk = (None, nseq, 1, cw)
    mblk = (None, nseq, 1, LANES)
    gblk = (None, nseq, gla_pairs, LANES, dh)
    ex3 = _expand_table(heads, dh)
    inputs = z_in + [c_all, n_all, m_all, g_all, m3, pm, ex3, bias_small, mnorm, wa2_pad, gla_ba, gnorm]
    in_specs = z_specs + [
        pl.BlockSpec(cblk, lambda b, t: (layer_j, b, 0, 0, 0)),
        pl.BlockSpec(nblk, lambda b, t: (layer_j, b, 0, 0)),
        pl.BlockSpec(mblk, lambda b, t: (layer_j, b, 0, 0)),
        pl.BlockSpec(gblk, lambda b, t: (layer_j, b, 0, 0, 0)),
        const(m3), const(pm), const(ex3), lay(bias_small), lay(mnorm), lay(wa2_pad), lay(gla_ba), lay(gnorm),
    ]
    inputs += list(c_prevs) + list(g_prevs)
    in_specs += ([pl.BlockSpec(cblk[1:], lambda b, t: (b, 0, 0, 0))] * len(c_prevs)
                 + [pl.BlockSpec(gblk[1:], lambda b, t: (b, 0, 0, 0))] * len(g_prevs))
    return pl.pallas_call(
        kern,
        out_shape=(
            jax.ShapeDtypeStruct((n, 2 * cw), BF16),
            c_shape,
            jax.ShapeDtypeStruct((batch, 1, cw), F32),
            jax.ShapeDtypeStruct((batch, 1, LANES), F32),
            g_shape,
        ),
        grid=(batch // nseq, nt),
        in_specs=in_specs,
        out_specs=(
            pl.BlockSpec((R, 2 * cw), lambda b, t: (b * nt + t, 0)),
            c_spec,
            pl.BlockSpec(nblk[1:], lambda b, t: (b, 0, 0)),
            pl.BlockSpec(mblk[1:], lambda b, t: (b, 0, 0)),
            g_spec,
        ),
        scratch_shapes=[
            pltpu.VMEM((R, 2 * cw), F32),
        ] + z_scratch,
        compiler_params=pltpu.CompilerParams(
            dimension_semantics=("arbitrary", "arbitrary"), vmem_limit_bytes=VMEM_LIMIT),
        name="odd_mixer",
    )(*inputs)


def _block_diag(w):
    nb, ci, di = w.shape
    eye = jnp.eye(nb, dtype=w.dtype)
    return (eye[:, None, :, None] * w[:, :, None, :]).reshape(nb * ci, nb * di)


def kernel(x_prompt, x_sample, state_hgrn, state_lru_h, state_lru_conv, state_mlstm_C, state_mlstm_n, state_mlstm_m, state_gla, norm_mix, norm_ffn, norm_final, w_in_even, hgrn_lb, hgrn_norm, lru_conv_w, lru_conv_b, lru_wa, lru_ba, lru_wx, lru_bx, lru_a, w_out_even, w_in_odd, mlstm_bi, mlstm_bf, mlstm_norm, gla_wa2, gla_ba, gla_norm, w_out_odd, ffn_w_up, ffn_w_down):
    batch, seq, d_model = x_prompt.shape
    dec_batch, dec_seq, _ = x_sample.shape
    depth = norm_mix.shape[0]
    n_even, _, a_heads, a_dk, _ = state_hgrn.shape
    n_odd, _, c_heads, c_dh, _ = state_mlstm_C.shape
    b_width = state_lru_h.shape[-1]
    c_width = c_heads * c_dh
    d_heads, d_dk, d_dv = state_gla.shape[2:]
    gk = d_heads * d_dk
    d_width = d_heads * d_dv
    d_rank = gla_wa2.shape[1]
    gla_pairs = gk // LANES
    assert a_heads * a_dk == b_width and GROUP % dec_seq == 0 and seq % (4 * GROUP) == 0

    a_width = a_heads * a_dk
    w_even = jnp.concatenate([w_in_even[:, :, :a_width], w_in_even[:, :, 2 * a_width:],
                              w_in_even[:, :, a_width:2 * a_width]], axis=-1).astype(BF16)
    even_e16 = w_even.shape[-1] - a_width
    main = 4 * c_width
    small_w = 2 * c_heads
    w_odd = jnp.concatenate([
        w_in_odd[:, :, :main],
        w_in_odd[:, :, main + small_w:main + small_w + 2 * gk + 2 * d_width],
        w_in_odd[:, :, main:main + small_w],
        w_in_odd[:, :, main + small_w + 2 * gk + 2 * d_width:],
        jnp.zeros((n_odd, d_model, LANES - small_w - d_rank), F32),
    ], axis=-1).astype(BF16)
    odd_e16 = w_odd.shape[-1] - LANES
    w_out_e = w_out_even.astype(BF16)
    w_out_o = w_out_odd.astype(BF16)
    w_up = ffn_w_up.astype(BF16)
    w_dn = ffn_w_down.astype(BF16)
    wa_dense = jax.vmap(_block_diag)(lru_wa).astype(BF16)
    wx_dense = jax.vmap(_block_diag)(lru_wx).astype(BF16)
    bias_small = jnp.concatenate(
        [mlstm_bi, mlstm_bf, jnp.zeros((n_odd, LANES - small_w), F32)], axis=-1)[:, None, :]
    wa2_pad = jnp.concatenate([
        jnp.zeros((n_odd, small_w, gk), F32), gla_wa2,
        jnp.zeros((n_odd, LANES - small_w - d_rank, gk), F32)], axis=1).astype(BF16)
    rows3 = lambda a: a[:, None, :]
    norm_mix3, norm_ffn3 = rows3(norm_mix), rows3(norm_ffn)
    even_params = (hgrn_lb, rows3(hgrn_norm), lru_conv_w, rows3(lru_conv_b), wa_dense, rows3(lru_ba),
                   wx_dense, rows3(lru_bx), rows3(lru_a))
    odd_params = (bias_small, rows3(mlstm_norm), wa2_pad, rows3(gla_ba), rows3(gla_norm))

    zeros = lambda *s: jnp.zeros(s, F32)
    groups = [
        dict(batch=batch, seq=seq, nseq=1, tb=4 * GROUP, L=math.gcd(seq, CHUNK), pos0=0, fuse_proj=True,
             x=x_prompt.reshape(batch * seq, d_model),
             hgrn=zeros(n_even, batch, a_heads, a_dk, a_dk),
             lru_h=zeros(n_even, batch, 1, b_width),
             lru_conv=zeros(n_even, batch, CONV_K - 1, b_width),
             mC=zeros(n_odd, batch, c_heads, c_dh, c_dh),
             mn=zeros(n_odd, batch, 1, c_width),
             mm=zeros(n_odd, batch, 1, LANES),
             gla=zeros(n_odd, batch, gla_pairs, LANES, d_dv)),
        dict(batch=dec_batch, seq=dec_seq, nseq=GROUP // dec_seq, tb=dec_seq, L=math.gcd(dec_seq, CHUNK),
             pos0=PAST_LEN, fuse_proj=False,
             x=x_sample.reshape(dec_batch * dec_seq, d_model),
             hgrn=state_hgrn,
             lru_h=state_lru_h[:, :, None, :],
             lru_conv=state_lru_conv,
             mC=state_mlstm_C,
             mn=state_mlstm_n.reshape(n_odd, dec_batch, 1, c_width),
             mm=jnp.pad(state_mlstm_m, ((0, 0), (0, 0), (c_heads, LANES - 2 * c_heads)))[:, :, None, :],
             gla=state_gla.reshape(n_odd, dec_batch, gla_pairs, LANES, d_dv)),
    ]

    results = []
    for grp in groups:
        x = grp["x"]
        bsz = grp["batch"]
        hgrn_done, mc_done, gla_done = [], [], []
        small = dict(lru_h=[], lru_conv=[], mn=[], mm=[])
        prevs = lambda done, j, n_kind: done if (j == n_kind - 1 and j > 0) else []

        def z_src(x, w, l, j, e16, fuse=grp["fuse_proj"]):
            if fuse:
                return ("x", x, norm_mix3, l, w, j, e16)
            return ("z",) + tuple(_norm_matmul(x, norm_mix3, w, l, j, e16))

        for l in range(depth):
            j = l // 2
            if l % 2 == 0:
                src = z_src(x, w_even, l, j, even_e16)
                mix, s, h, cb = _even_mixer(src, grp["hgrn"], grp["lru_h"], grp["lru_conv"],
                                            prevs(hgrn_done, j, n_even), j, bsz, grp["seq"], grp["nseq"], grp["tb"],
                                            grp["L"], grp["pos0"], even_params)
                hgrn_done.append(s)
                small["lru_h"].append(h[:, 0, :])
                small["lru_conv"].append(cb)
                w_out = w_out_e
            else:
                src = z_src(x, w_odd, l, j, odd_e16)
                mix, cm, nn, mm, sg = _odd_mixer(src, grp["mC"], grp["mn"], grp["mm"], grp["gla"],
                                                 prevs(mc_done, j, n_odd), prevs(gla_done, j, n_odd), j, bsz,
                                                 grp["seq"], grp["nseq"], grp["tb"], grp["L"], odd_params)
                mc_done.append(cm)
                gla_done.append(sg)
                small["mn"].append(nn.reshape(bsz, c_heads, c_dh))
                small["mm"].append(mm[:, 0, c_heads:2 * c_heads])
                w_out = w_out_o
            x = _proj_ffn(x, mix, w_out, norm_ffn3, w_up, w_dn, norm_final[None, :], l, j,
                          final=(l == depth - 1))
        st = {k: jnp.stack(v) for k, v in small.items()}
        stacked = lambda done: done[-1] if len(done) > 1 else done[-1][None]
        st["hgrn"] = stacked(hgrn_done)
        st["mC"] = stacked(mc_done)
        st["gla"] = stacked(gla_done).reshape(n_odd, bsz, d_heads, d_dk, d_dv)
        results.append((x.reshape(bsz, grp["seq"], d_model), st))

    (y_p, st_p), (y_s, st_s) = results
    return (y_p, y_s, st_p["hgrn"], st_s["hgrn"], st_p["lru_h"], st_s["lru_h"], st_p["lru_conv"], st_s["lru_conv"],
            st_p["mC"], st_s["mC"], st_p["mn"], st_s["mn"], st_p["mm"], st_s["mm"], st_p["gla"], st_s["gla"])
```

```python
import functools
import math

import jax
import jax.numpy as jnp
import numpy as np
from jax import lax
from jax.experimental import pallas as pl
from jax.experimental.pallas import tpu as pltpu

F32 = jnp.float32
BF16 = jnp.bfloat16
EPS = 1e-6
LANES = 128
GROUP = 64
PROMPT_TILES = 8
CHUNK = 64
CONV_K = 4
LRU_C = 8.0
GLA_TAU = 16.0
PAST_LEN = 16384
MIB = 1024 * 1024
V7X_VMEM_BYTES = 64 * MIB
VMEM_LIMIT = 3 * V7X_VMEM_BYTES // 4
VMEM_LIMIT_FFN = 13 * V7X_VMEM_BYTES // 16
NEG_INF = float("-inf")


def _log1p_exp_neg_abs(x):
    return jnp.log(1.0 + jnp.exp(-jnp.abs(x)))


def _log_sigmoid(x):
    return jnp.minimum(x, 0.0) - _log1p_exp_neg_abs(x)


def _softplus(x):
    return jnp.maximum(x, 0.0) + _log1p_exp_neg_abs(x)


def _silu(x):
    return x * jax.nn.sigmoid(x)


def _gelu_tanh(x):
    return 0.5 * x * (1.0 + jnp.tanh(math.sqrt(2.0 / math.pi) * (x + 0.044715 * (x * x * x))))


def _rms(x):
    return x * lax.rsqrt(jnp.mean(x * x, axis=-1, keepdims=True) + EPS)


def _dot(a, b):
    return jnp.dot(a, b, preferred_element_type=F32)


def _dot_nt(a, b):
    return lax.dot_general(a, b, (((1,), (1,)), ((), ())), preferred_element_type=F32)


def _dot_tn(a, b):
    return lax.dot_general(a, b, (((0,), (0,)), ((), ())), preferred_element_type=F32)


def _layer_spec(shape, layer, grid_rank, **kw):
    zeros = (0,) * len(shape)
    if grid_rank == 1:
        return pl.BlockSpec((None,) + tuple(shape), lambda i: (layer,) + zeros, **kw)
    return pl.BlockSpec((None,) + tuple(shape), lambda i, j: (layer,) + zeros, **kw)


def _level_tables(L):
    t = np.arange(GROUP)[:, None]
    s = np.arange(GROUP)[None, :]
    widths = []
    w = L // 2
    while w >= 1:
        widths.append(w)
        w //= 2
    mats, pmasks = [], []
    for w in widths:
        same = (t // (2 * w)) == (s // (2 * w))
        low_t, low_s = (t % (2 * w)) >= w, (s % (2 * w)) >= w
        mats.append((same & low_t & low_s & (s <= t)) | (same & ~low_t & ~low_s & (s > t)))
        pmasks.append(same & low_t & ~low_s)
    seg = (t // L) == (s // L)
    mats.append(seg & (s <= t))
    mats.append(seg & (s > t))
    pmasks.append(t == s)
    m_all = np.concatenate(mats, axis=0).astype(np.float32)
    m3 = np.concatenate([m_all, m_all, m_all], axis=1)
    pm = np.stack(pmasks).astype(np.float32)
    return jnp.asarray(m3, BF16), jnp.asarray(np.concatenate([pm, pm, pm, pm], axis=-1)), widths


def _level_sums(m3, g):
    g_hi = g.astype(BF16)
    r1 = g - g_hi.astype(F32)
    g_mid = r1.astype(BF16)
    g_lo = (r1 - g_mid.astype(F32)).astype(BF16)
    return _dot(m3, jnp.concatenate([g_hi, g_mid, g_lo], axis=0))


def _lower_masks(widths):
    row = lax.broadcasted_iota(jnp.int32, (GROUP, LANES), 0)
    return [(row % (2 * w)) >= w for w in widths]


def _split_dot(x, table3):
    x_hi = x.astype(BF16)
    r1 = x - x_hi.astype(F32)
    x_mid = r1.astype(BF16)
    x_lo = (r1 - x_mid.astype(F32)).astype(BF16)
    return _dot(jnp.concatenate([x_hi, x_mid, x_lo], axis=1), table3)


def _expand_table(heads, dh):
    c = np.arange(LANES)[:, None]
    col = np.arange(heads * dh)[None, :] // dh
    t = np.concatenate([c == heads + col, c == col], axis=1).astype(np.float32)
    return jnp.asarray(np.concatenate([t, t, t], axis=0), BF16)


def _block_rows(xs, width):
    zero = jnp.zeros((xs[0].shape[0], width), xs[0].dtype)
    return jnp.concatenate(
        [jnp.concatenate([x if i == h else zero for i in range(len(xs))], axis=1) for h, x in enumerate(xs)], axis=0)


def _pack_heads(cols, lo_half):
    return jnp.concatenate([jnp.where(lo_half, cols[2 * p], cols[2 * p + 1]) for p in range(len(cols) // 2)], axis=1)


def _gla_heads_tile(qs, ks, es, vs, group_of, masks, pm_ref, lowers, L, states, put_state):
    nlev = len(lowers)
    nh, ng = len(vs), len(qs)
    lo_half = lax.broadcasted_iota(jnp.int32, (1, LANES), 1) < GROUP

    def blk(e, i):
        return e[GROUP * i:GROUP * (i + 1), :]

    def own(a, h):
        return a if masks[h] is None else jnp.where(masks[h], a, 0.0)

    def rhs_rows(ys):
        zero = jnp.zeros((GROUP, LANES), F32)
        return jnp.concatenate(
            [jnp.concatenate([own(ys[i], h) if i == group_of[h] else zero for i in range(ng)], axis=1)
             for h in range(nh)], axis=0)

    scores = None
    for lv in range(nlev):
        ys = [jnp.where(lowers[lv], q, k) * blk(e, lv) for q, k, e in zip(qs, ks, es)]
        part = _dot_nt(jnp.concatenate(ys, axis=1).astype(BF16), rhs_rows(ys).astype(BF16)) * pm_ref[lv]
        scores = part if scores is None else scores + part
    qks = [q * k for q, k in zip(qs, ks)]
    d = [jnp.sum(own(qks[group_of[h]], h), axis=-1, keepdims=True) for h in range(nh)]
    diag = pm_ref[nlev] * _pack_heads(d, lo_half)
    p = diag if scores is None else scores + diag
    intra = _dot(p.astype(BF16), _block_rows(vs, LANES).astype(BF16))
    vbs = [v.astype(BF16) for v in vs]
    ebs = [blk(e, nlev) for e in es]
    qes = [q * eb for q, eb in zip(qs, ebs)]
    kds = [k * blk(e, nlev + 1) for k, e in zip(ks, es)]
    pieces = [[] for _ in range(nh)]
    for j in range(GROUP // L):
        sl = slice(L * j, L * (j + 1))
        sbs = [s.astype(BF16) for s in states[j]]
        upds = [None] * ng
        for h in range(nh):
            i = group_of[h]
            pieces[h].append(intra[sl, LANES * h:LANES * (h + 1)] + _dot(own(qes[i][sl], h).astype(BF16), sbs[i]))
            t = _dot_tn(own(kds[i][sl], h).astype(BF16), vbs[h][sl])
            upds[i] = t if upds[i] is None else upds[i] + t
        for i, upd in enumerate(upds):
            last = ebs[i][L * j + L - 1:L * j + L, :]
            decay_col = jnp.transpose(jnp.broadcast_to(last, (8, LANES)))[:, 0:1]
            put_state(j, i, decay_col * states[j][i] + upd)
    return [p[0] if len(p) == 1 else jnp.concatenate(p, axis=0) for p in pieces]


def _norm_matmul_kernel(x_ref, g_ref, w_ref, o16_ref, o32_ref, *, e16, tn):
    def put16(cols, val):
        o16_ref[:, cols] = val

    def put32(val):
        o32_ref[...] = val

    _project_rows(x_ref, g_ref, w_ref, put16, put32, e16, tn)


def _norm_matmul(x, g, w, layer, wlayer, e16, tm=512, tn=512):
    n, d = x.shape
    e = w.shape[-1]
    return pl.pallas_call(
        functools.partial(_norm_matmul_kernel, e16=e16, tn=tn),
        out_shape=(jax.ShapeDtypeStruct((n, e16), BF16), jax.ShapeDtypeStruct((n, e - e16), F32)),
        grid=(n // tm,),
        in_specs=[
            pl.BlockSpec((tm, d), lambda i: (i, 0)),
            _layer_spec((1, d), layer, 1),
            _layer_spec((d, e), wlayer, 1, pipeline_mode=pl.Buffered(1)),
        ],
        out_specs=(pl.BlockSpec((tm, e16), lambda i: (i, 0)), pl.BlockSpec((tm, e - e16), lambda i: (i, 0))),
        compiler_params=pltpu.CompilerParams(
            dimension_semantics=("parallel",), vmem_limit_bytes=VMEM_LIMIT),
        name="norm_in_proj",
    )(x, g, w)


def _project_rows(x_ref, g_ref, w_ref, put16, put32, e16, tn=512):
    xn = (_rms(x_ref[...]) * g_ref[...]).astype(BF16)
    for c in range(e16 // tn):
        put16(slice(c * tn, (c + 1) * tn), _dot(xn, w_ref[:, c * tn:(c + 1) * tn]).astype(BF16))
    put32(_dot(xn, w_ref[:, e16:]))


def _z_source(head, bufs, e16):
    if bufs is None:
        return head[0], head[1], lambda: None
    x_first, x_next, g_ref, w_ref = head
    zs, zgs = bufs
    step = pl.program_id(0) * pl.num_programs(1) + pl.program_id(1)
    slot = step % 2

    def project(x_ref, dst):
        def put16(cols, val):
            zs[dst, :, cols] = val

        def put32(val):
            zgs[dst] = val

        _project_rows(x_ref, g_ref, w_ref, put16, put32, e16)

    @pl.when(step == 0)
    def _():
        project(x_first, 0)

    return zs.at[slot], zgs.at[slot], lambda: project(x_next, 1 - slot)


def _z_inputs(src, R, nt, nblocks):
    if src[0] == "z":
        _, z, zg = src
        specs = [pl.BlockSpec((R, z.shape[1]), lambda b, t: (b * nt + t, 0)),
                 pl.BlockSpec((R, zg.shape[1]), lambda b, t: (b * nt + t, 0))]
        return [z, zg], specs, [], z.shape[0], None
    _, x, g3, layer, w, wlayer, e16 = src
    d, e = x.shape[1], w.shape[-1]
    specs = [pl.BlockSpec((R, d), lambda b, t: (0, 0)),
             pl.BlockSpec((R, d), lambda b, t: (jnp.minimum(b * nt + t + 1, nblocks - 1), 0)),
             _layer_spec((1, d), layer, 2),
             _layer_spec((d, e), wlayer, 2, pipeline_mode=pl.Buffered(1))]
    scratch = [pltpu.VMEM((2, R, e16), BF16), pltpu.VMEM((2, R, e - e16), F32)]
    return [x, x, g3, w], specs, scratch, x.shape[0], e16


def _proj_ffn_kernel(x_ref, mix_ref, wo_ref, g_ref, wup_ref, wdn_ref, gf_ref, o_ref, *, d_ff, tf, final):
    xnew = x_ref[...] + _dot(mix_ref[...], wo_ref[...])
    h2 = (_rms(xnew) * g_ref[...]).astype(BF16)
    o_ref[...] = xnew
    for f in range(d_ff // tf):
        gate = _dot(h2, wup_ref[:, f * tf:(f + 1) * tf])
        val = _dot(h2, wup_ref[:, d_ff + f * tf:d_ff + (f + 1) * tf])
        act = (_silu(gate) * val).astype(BF16)
        o_ref[...] += _dot(act, wdn_ref[f * tf:(f + 1) * tf, :])
    if final:
        o_ref[...] = _rms(o_ref[...]) * gf_ref[...]


def _proj_ffn(x, mix, wo, g, wup, wdn, gf, layer, wolayer, final, tm=512, tf=256):
    n, d = x.shape
    d_ff = wdn.shape[-2]
    once = dict(pipeline_mode=pl.Buffered(1))
    return pl.pallas_call(
        functools.partial(_proj_ffn_kernel, d_ff=d_ff, tf=tf, final=final),
        out_shape=jax.ShapeDtypeStruct((n, d), F32),
        grid=(n // tm,),
        in_specs=[
            pl.BlockSpec((tm, d), lambda i: (i, 0)),
            pl.BlockSpec((tm, d), lambda i: (i, 0)),
            _layer_spec((d, d), wolayer, 1, **once),
            _layer_spec((1, d), layer, 1, **once),
            _layer_spec((d, 2 * d_ff), layer, 1, **once),
            _layer_spec((d_ff, d), layer, 1, **once),
            pl.BlockSpec((1, d), lambda i: (0, 0), **once),
        ],
        out_specs=pl.BlockSpec((tm, d), lambda i: (i, 0)),
        compiler_params=pltpu.CompilerParams(
            dimension_semantics=("parallel",), vmem_limit_bytes=VMEM_LIMIT_FFN),
        name="out_proj_ffn",
    )(x, mix, wo, g, wup, wdn, gf)


def _even_mixer_kernel(*refs, layer_j, nseq, tb, seq, L, widths, pos0, heads, width, nprev, e16):
    nhead = 2 if e16 is None else 4
    (s_in, h_in, c_in, m3_ref, pm_ref, lb_ref, hnorm_ref, cw_ref, cb_ref, wa_ref,
     ba_ref, wx_ref, bx_ref, la_ref) = refs[nhead:nhead + 14]
    nout = nhead + 14 + nprev
    prev_states = refs[nhead + 14:nout]
    mix_ref, s_stack, h_out, c_out, mixf, xpad = refs[nout:nout + 6]
    z_ref, zg_ref, project_next = _z_source(refs[:nhead], refs[nout + 6:] or None, e16)
    s_out = s_stack.at[layer_j] if nprev else s_stack
    ti = pl.program_id(1)
    R = nseq * tb
    dk = width // heads

    @pl.when(ti == 0)
    def _():
        for i, prev in enumerate(prev_states):
            s_stack[i] = prev[...]
        s_out[...] = s_in[...]
        h_out[...] = h_in[...]
        c_out[...] = c_in[...]

    hl = lb_ref[...]
    e = jnp.exp(hl - jnp.max(hl, axis=0, keepdims=True))
    sm = e / jnp.sum(e, axis=0, keepdims=True)
    lb = jnp.zeros((1, width), F32)
    for i in range(1, layer_j + 1):
        lb = lb + sm[i:i + 1, :]
    log_lb = jnp.log(lb)
    lowers = _lower_masks(widths)

    for tidx in range(R // GROUP):
        r0 = tidx * GROUP
        rows = slice(r0, r0 + GROUP)
        seq_ids = [(r0 + L * j) // tb for j in range(GROUP // L)]
        x = zg_ref[rows, :]
        y = log_lb - x
        logf = _log_sigmoid(x) + jnp.maximum(y, 0.0) + _log1p_exp_neg_abs(y)
        key = 1.0 - jnp.exp(logf)
        qsc = z_ref[rows, 0:width].astype(F32) * (dk ** -0.5)
        e_all = jnp.exp(_level_sums(m3_ref[...], logf))
        ln = [slice(dk * h, dk * (h + 1)) for h in range(heads)]
        vs = [z_ref[rows, width + dk * h:width + dk * (h + 1)] for h in range(heads)]
        states = [[s_out[si, h] for h in range(heads)] for si in seq_ids]

        def put_state(j, i, new, seq_ids=seq_ids):
            s_out[seq_ids[j], i] = new

        outs = _gla_heads_tile([qsc[:, l] for l in ln], [key[:, l] for l in ln], [e_all[:, l] for l in ln], vs,
                               list(range(heads)), [None] * heads, pm_ref, lowers, L, states, put_state)
        for h, (l, o) in enumerate(zip(ln, outs)):
            gate = z_ref[rows, 2 * width + dk * h:2 * width + dk * (h + 1)].astype(F32)
            mixf[rows, l] = _rms(o) * hnorm_ref[:, l] * _silu(gate)

    xoff, yoff = 3 * width, 4 * width
    xpad[:, 8 - (CONV_K - 1):8, :] = c_out[...]
    xpad[:, 8:, :] = z_ref[:, xoff:xoff + width].astype(F32).reshape(nseq, tb, width)
    xc = cb_ref[...].reshape(1, 1, width) + jnp.zeros((nseq, tb, width), F32)
    for j in range(CONV_K):
        lo = 8 - (CONV_K - 1) + j
        xc = xc + xpad[:, lo:lo + tb, :] * cw_ref[j:j + 1, :].reshape(1, 1, width)
    c_out[...] = xpad[:, tb + 8 - (CONV_K - 1):tb + 8, :]
    xc = xc.reshape(R, width)
    xcb = xc.astype(BF16)
    rg = jax.nn.sigmoid(_dot(xcb, wa_ref[...]) + ba_ref[...])
    ig = jax.nn.sigmoid(_dot(xcb, wx_ref[...]) + bx_ref[...])
    log_a = (-LRU_C) * rg * _softplus(-la_ref[...])
    a = jnp.exp(log_a)
    mult = jnp.sqrt(1.0 - a * a)
    row_in_seq = lax.broadcasted_iota(jnp.int32, (R, width), 0) % tb
    if pos0 <= 0 < pos0 + seq:
        mult = jnp.where(row_in_seq + ti * tb + pos0 == 0, 1.0, mult)
    hh = mult * ig * xc
    aa = a
    shift = 1
    while shift < tb:
        m = row_in_seq >= shift
        hh_new = jnp.where(m, aa * pltpu.roll(hh, shift, 0) + hh, hh)
        aa = jnp.where(m, aa * pltpu.roll(aa, shift, 0), aa)
        hh = hh_new
        shift *= 2
    hseq = hh.reshape(nseq, tb, width) + aa.reshape(nseq, tb, width) * h_out[...]
    h_out[...] = hseq[:, tb - 1:tb, :]
    mixf[:, width:2 * width] = hseq.reshape(R, width) * _gelu_tanh(z_ref[:, yoff:yoff + width].astype(F32))
    mix_ref[...] = mixf[...].astype(BF16)
    project_next()


def _state_out(prevs, n_layers, batch, nseq, tail):
    zeros = (0,) * len(tail)
    if prevs:
        return (jax.ShapeDtypeStruct((n_layers, batch) + tail, F32),
                pl.BlockSpec((n_layers, nseq) + tail, lambda b, t: (0, b) + zeros))
    return (jax.ShapeDtypeStruct((batch,) + tail, F32), pl.BlockSpec((nseq,) + tail, lambda b, t: (b,) + zeros))


def _even_mixer(src, s_all, h_all, c_all, s_prevs, layer_j, batch, seq, nseq, tb, L, pos0, params):
    (hgrn_lb, hgrn_norm, conv_w, conv_b, wa, ba, wx, bx, lru_a) = params
    n_layers, _, heads, dk, _ = s_all.shape
    width = heads * dk
    R = nseq * tb
    nt = seq // tb
    z_in, z_specs, z_scratch, n, e16 = _z_inputs(src, R, nt, (batch // nseq) * nt)
    m3, pm, widths = _level_tables(L)
    const = lambda a: pl.BlockSpec(a.shape, lambda b, t: (0,) * a.ndim)
    lay = lambda a: _layer_spec(a.shape[1:], layer_j, 2)
    kern = functools.partial(_even_mixer_kernel, layer_j=layer_j, nseq=nseq, tb=tb, seq=seq, L=L, widths=widths,
                             pos0=pos0, heads=heads, width=width, nprev=len(s_prevs), e16=e16)
    s_shape, s_spec = _state_out(s_prevs, n_layers, batch, nseq, (heads, dk, dk))
    sblk = (None, nseq, heads, dk, dk)
    hblk = (None, nseq, 1, width)
    cblk = (None, nseq, CONV_K - 1, width)
    inputs = z_in + [s_all, h_all, c_all, m3, pm, hgrn_lb, hgrn_norm, conv_w, conv_b, wa, ba, wx, bx, lru_a]
    in_specs = z_specs + [
        pl.BlockSpec(sblk, lambda b, t: (layer_j, b, 0, 0, 0)),
        pl.BlockSpec(hblk, lambda b, t: (layer_j, b, 0, 0)),
        pl.BlockSpec(cblk, lambda b, t: (layer_j, b, 0, 0)),
        const(m3), const(pm), const(hgrn_lb), lay(hgrn_norm), lay(conv_w), lay(conv_b),
        lay(wa), lay(ba), lay(wx), lay(bx), lay(lru_a),
    ]
    inputs += list(s_prevs)
    in_specs += [pl.BlockSpec(sblk[1:], lambda b, t: (b, 0, 0, 0))] * len(s_prevs)
    return pl.pallas_call(
        kern,
        out_shape=(
            jax.ShapeDtypeStruct((n, 2 * width), BF16),
            s_shape,
            jax.ShapeDtypeStruct((batch, 1, width), F32),
            jax.ShapeDtypeStruct((batch, CONV_K - 1, width), F32),
        ),
        grid=(batch // nseq, nt),
        in_specs=in_specs,
        out_specs=(
            pl.BlockSpec((R, 2 * width), lambda b, t: (b * nt + t, 0)),
            s_spec,
            pl.BlockSpec(hblk[1:], lambda b, t: (b, 0, 0)),
            pl.BlockSpec(cblk[1:], lambda b, t: (b, 0, 0)),
        ),
        scratch_shapes=[
            pltpu.VMEM((R, 2 * width), F32),
            pltpu.VMEM((nseq, tb + 8, width), F32),
        ] + z_scratch,
        compiler_params=pltpu.CompilerParams(
            dimension_semantics=("arbitrary", "arbitrary"), vmem_limit_bytes=VMEM_LIMIT),
        name="even_mixer",
    )(*inputs)


def _odd_mixer_kernel(*refs, layer_j, nseq, tb, L, widths, heads, dh, gla_pairs, nprev, e16):
    nhead = 2 if e16 is None else 4
    (c_in, n_in, m_in, g_in, m3_ref, pm_ref, ex3_ref, bias_ref, mnorm_ref, wa2_ref, gba_ref,
     gnorm_ref) = refs[nhead:nhead + 12]
    nout = nhead + 12 + 2 * nprev
    prev_c, prev_g = refs[nhead + 12:nhead + 12 + nprev], refs[nhead + 12 + nprev:nout]
    mix_ref, c_stack, n_out, m_out, g_stack, mixf = refs[nout:nout + 6]
    z_ref, zg_ref, project_next = _z_source(refs[:nhead], refs[nout + 6:] or None, e16)
    c_out = c_stack.at[layer_j] if nprev else c_stack
    g_out = g_stack.at[layer_j] if nprev else g_stack
    ti = pl.program_id(1)
    R = nseq * tb
    cw = heads * dh
    gk = gla_pairs * LANES
    q_off, k_off, v_off, o_off = 0, cw, 2 * cw, 3 * cw
    gq_off = 4 * cw
    gk_off = gq_off + gk
    gv_off = gk_off + gk
    gg_off = gv_off + cw
    gla_dk = gk // heads
    nlev = len(widths)

    @pl.when(ti == 0)
    def _():
        for i in range(nprev):
            c_stack[i] = prev_c[i][...]
            g_stack[i] = prev_g[i][...]
        c_out[...] = c_in[...]
        n_out[...] = n_in[...]
        m_out[...] = m_in[...]
        g_out[...] = g_in[...]

    lane = lax.broadcasted_iota(jnp.int32, (1, LANES), 1)
    is_f = (lane >= heads) & (lane < 2 * heads)
    lo_half = lane < LANES // 2
    hi_half = lane >= LANES // 2
    trow = lax.broadcasted_iota(jnp.int32, (GROUP, heads * GROUP), 0)
    tcol = lax.broadcasted_iota(jnp.int32, (GROUP, heads * GROUP), 1) % GROUP
    seg_tril = (tcol <= trow) & ((trow // L) == (tcol // L))
    gla_masks = [lo_half, hi_half]
    pack = functools.partial(_pack_heads, lo_half=lo_half)

    def head_reduce(x, h, fn, fill):
        col = x[:, LANES * (h // 2):LANES * (h // 2 + 1)]
        return fn(jnp.where(lo_half if h % 2 == 0 else hi_half, col, fill), axis=-1, keepdims=True)
    lowers = _lower_masks(widths)

    for tidx in range(R // GROUP):
        r0 = tidx * GROUP
        rows = slice(r0, r0 + GROUP)
        seq_ids = [(r0 + L * j) // tb for j in range(GROUP // L)]
        small = zg_ref[rows, :]
        g0 = small + bias_ref[...]
        pre = _dot(small.astype(BF16), wa2_ref[...]) + gba_ref[...]
        la = _log_sigmoid(pre) * (1.0 / GLA_TAU)
        sums = _level_sums(m3_ref[...], jnp.concatenate([la, _log_sigmoid(g0)], axis=-1))
        e_all = jnp.exp(sums[:, 0:gk])
        gates = jnp.where(is_f, sums[GROUP * nlev:GROUP * (nlev + 1), gk:gk + LANES], g0)

        kl = [slice(LANES * p, LANES * (p + 1)) for p in range(gla_pairs)]
        vs = [z_ref[rows, gv_off + dh * h:gv_off + dh * (h + 1)] for h in range(heads)]
        qsc = [z_ref[rows, gq_off + LANES * p:gq_off + LANES * (p + 1)].astype(F32) * (gla_dk ** -0.5)
               for p in range(gla_pairs)]
        kk = [z_ref[rows, gk_off + LANES * p:gk_off + LANES * (p + 1)].astype(F32) for p in range(gla_pairs)]
        states = [[g_out[si, p] for p in range(gla_pairs)] for si in seq_ids]

        def put_state(j, i, new, seq_ids=seq_ids):
            g_out[seq_ids[j], i] = new

        outs = _gla_heads_tile(qsc, kk, [e_all[:, l] for l in kl], vs, [h // 2 for h in range(heads)],
                               [gla_masks[h % 2] for h in range(heads)], pm_ref, lowers, L, states, put_state)
        for h, o in enumerate(outs):
            vl = slice(dh * h, dh * (h + 1))
            gg = z_ref[rows, gg_off + dh * h:gg_off + dh * (h + 1)].astype(F32)
            mixf[rows, cw + dh * h:cw + dh * (h + 1)] = _rms(o) * gnorm_ref[:, vl] * _silu(gg)

        m_next = [m_out[si] for si in seq_ids]
        ex = _split_dot(gates, ex3_ref[...])
        bcol = [ex[:, dh * h:dh * (h + 1)] for h in range(heads)]
        icol = [ex[:, cw + dh * h:cw + dh * (h + 1)] for h in range(heads)]
        mprev = [[jnp.broadcast_to(mv[:, heads + h:heads + h + 1], (1, dh)) for mv in m_next] for h in range(heads)]
        inter = []
        for h in range(heads):
            parts = [bcol[h][L * j:L * (j + 1)] + mprev[h][j] for j in range(GROUP // L)]
            inter.append(parts[0] if len(parts) == 1 else jnp.concatenate(parts, axis=0))
        gtt = jnp.transpose(gates)
        urow = jnp.concatenate([gtt[h:h + 1, :] - gtt[heads + h:heads + h + 1, :] for h in range(heads)], axis=1)
        dlog = jnp.where(seg_tril, pack(bcol) + urow, NEG_INF)
        m_t = [jnp.maximum(inter[h], head_reduce(dlog, h, jnp.max, NEG_INF)) for h in range(heads)]
        w_intra = jnp.exp(dlog - pack(m_t))
        w_inter = [jnp.exp(inter[h] - m_t[h]) for h in range(heads)]
        q_t = [z_ref[rows, q_off + dh * h:q_off + dh * (h + 1)].astype(F32) for h in range(heads)]
        k_t = [z_ref[rows, k_off + dh * h:k_off + dh * (h + 1)].astype(F32) * (dh ** -0.5) for h in range(heads)]
        v_t = [z_ref[rows, v_off + dh * h:v_off + dh * (h + 1)] for h in range(heads)]
        qk = _dot_nt(jnp.concatenate(q_t, axis=1).astype(BF16), _block_rows(k_t, dh).astype(BF16)) * w_intra
        num_all = _dot(qk.astype(BF16), _block_rows(v_t, dh).astype(BF16))
        for h in range(heads):
            lanes = slice(dh * h, dh * (h + 1))
            qk_sum = head_reduce(qk, h, jnp.sum, 0.0)
            cells = []
            for j in range(GROUP // L):
                sl = slice(L * j, L * (j + 1))
                si = seq_ids[j]
                q, k, vb = q_t[h][sl], k_t[h][sl], v_t[h][sl].astype(BF16)
                cmat = c_out[si, h]
                nrow = n_out[si, :, lanes]
                wi = w_inter[h][sl]
                num = num_all[sl, lanes] + wi * _dot(q.astype(BF16), cmat.astype(BF16))
                den = qk_sum[sl] + wi * jnp.sum(q * nrow, axis=-1, keepdims=True)
                cells.append(num / jnp.maximum(jnp.abs(den), jnp.exp(-m_t[h][sl])))
                b_last = bcol[h][L * j + L - 1:L * j + L, :]
                m_prev = mprev[h][j]
                wlog = b_last - bcol[h][sl] + icol[h][sl]
                m_new = jnp.maximum(b_last + m_prev, jnp.max(wlog, axis=0, keepdims=True))
                w_s = jnp.exp(wlog - m_new)
                decay = jnp.exp(b_last + m_prev - m_new)
                kw = k * w_s
                c_out[si, h] = decay * cmat + _dot_tn(kw.astype(BF16), vb)
                n_out[si, :, lanes] = decay * nrow + jnp.sum(kw, axis=0, keepdims=True)
                m_next[j] = jnp.where(lane == heads + h, m_new, m_next[j])
            hcell = cells[0] if len(cells) == 1 else jnp.concatenate(cells, axis=0)
            ogate = z_ref[rows, o_off + dh * h:o_off + dh * (h + 1)].astype(F32)
            mixf[rows, lanes] = _rms(hcell) * mnorm_ref[:, lanes] * jax.nn.sigmoid(ogate)
        for j, si in enumerate(seq_ids):
            m_out[si] = m_next[j]
    mix_ref[...] = mixf[...].astype(BF16)
    project_next()


def _odd_mixer(src, c_all, n_all, m_all, g_all, c_prevs, g_prevs, layer_j, batch, seq, nseq, tb, L, params):
    (bias_small, mnorm, wa2_pad, gla_ba, gnorm) = params
    n_layers, _, heads, dh, _ = c_all.shape
    cw = heads * dh
    gla_pairs = g_all.shape[2]
    R = nseq * tb
    nt = seq // tb
    z_in, z_specs, z_scratch, n, e16 = _z_inputs(src, R, nt, (batch // nseq) * nt)
    m3, pm, widths = _level_tables(L)
    const = lambda a: pl.BlockSpec(a.shape, lambda b, t: (0,) * a.ndim)
    lay = lambda a: _layer_spec(a.shape[1:], layer_j, 2)
    kern = functools.partial(_odd_mixer_kernel, layer_j=layer_j, nseq=nseq, tb=tb, L=L, widths=widths, heads=heads,
                             dh=dh, gla_pairs=gla_pairs, nprev=len(c_prevs), e16=e16)
    c_shape, c_spec = _state_out(c_prevs, n_layers, batch, nseq, (heads, dh, dh))
    g_shape, g_spec = _state_out(g_prevs, n_layers, batch, nseq, (gla_pairs, LANES, dh))
    cblk = (None, nseq, heads, dh, dh)
    nblk = (None, nseq, 1, cw)
    mblk = (None, nseq, 1, LANES)
    gblk = (None, nseq, gla_pairs, LANES, dh)
    ex3 = _expand_table(heads, dh)
    inputs = z_in + [c_all, n_all, m_all, g_all, m3, pm, ex3, bias_small, mnorm, wa2_pad, gla_ba, gnorm]
    in_specs = z_specs + [
        pl.BlockSpec(cblk, lambda b, t: (layer_j, b, 0, 0, 0)),
        pl.BlockSpec(nblk, lambda b, t: (layer_j, b, 0, 0)),
        pl.BlockSpec(mblk, lambda b, t: (layer_j, b, 0, 0)),
        pl.BlockSpec(gblk, lambda b, t: (layer_j, b, 0, 0, 0)),
        const(m3), const(pm), const(ex3), lay(bias_small), lay(mnorm), lay(wa2_pad), lay(gla_ba), lay(gnorm),
    ]
    inputs += list(c_prevs) + list(g_prevs)
    in_specs += ([pl.BlockSpec(cblk[1:], lambda b, t: (b, 0, 0, 0))] * len(c_prevs)
                 + [pl.BlockSpec(gblk[1:], lambda b, t: (b, 0, 0, 0))] * len(g_prevs))
    return pl.pallas_call(
        kern,
        out_shape=(
            jax.ShapeDtypeStruct((n, 2 * cw), BF16),
            c_shape,
            jax.ShapeDtypeStruct((batch, 1, cw), F32),
            jax.ShapeDtypeStruct((batch, 1, LANES), F32),
            g_shape,
        ),
        grid=(batch // nseq, nt),
        in_specs=in_specs,
        out_specs=(
            pl.BlockSpec((R, 2 * cw), lambda b, t: (b * nt + t, 0)),
            c_spec,
            pl.BlockSpec(nblk[1:], lambda b, t: (b, 0, 0)),
            pl.BlockSpec(mblk[1:], lambda b, t: (b, 0, 0)),
            g_spec,
        ),
        scratch_shapes=[
            pltpu.VMEM((R, 2 * cw), F32),
        ] + z_scratch,
        compiler_params=pltpu.CompilerParams(
            dimension_semantics=("arbitrary", "arbitrary"), vmem_limit_bytes=VMEM_LIMIT),
        name="odd_mixer",
    )(*inputs)


def _block_diag(w):
    nb, ci, di = w.shape
    eye = jnp.eye(nb, dtype=w.dtype)
    return (eye[:, None, :, None] * w[:, :, None, :]).reshape(nb * ci, nb * di)


def kernel(x_prompt, x_sample, state_hgrn, state_lru_h, state_lru_conv, state_mlstm_C, state_mlstm_n, state_mlstm_m, state_gla, norm_mix, norm_ffn, norm_final, w_in_even, hgrn_lb, hgrn_norm, lru_conv_w, lru_conv_b, lru_wa, lru_ba, lru_wx, lru_bx, lru_a, w_out_even, w_in_odd, mlstm_bi, mlstm_bf, mlstm_norm, gla_wa2, gla_ba, gla_norm, w_out_odd, ffn_w_up, ffn_w_down):
    batch, seq, d_model = x_prompt.shape
    dec_batch, dec_seq, _ = x_sample.shape
    depth = norm_mix.shape[0]
    n_even, _, a_heads, a_dk, _ = state_hgrn.shape
    n_odd, _, c_heads, c_dh, _ = state_mlstm_C.shape
    b_width = state_lru_h.shape[-1]
    c_width = c_heads * c_dh
    d_heads, d_dk, d_dv = state_gla.shape[2:]
    gk = d_heads * d_dk
    d_width = d_heads * d_dv
    d_rank = gla_wa2.shape[1]
    gla_pairs = gk // LANES
    assert a_heads * a_dk == b_width and GROUP % dec_seq == 0 and seq % (PROMPT_TILES * GROUP) == 0

    a_width = a_heads * a_dk
    w_even = jnp.concatenate([w_in_even[:, :, :a_width], w_in_even[:, :, 2 * a_width:],
                              w_in_even[:, :, a_width:2 * a_width]], axis=-1).astype(BF16)
    even_e16 = w_even.shape[-1] - a_width
    main = 4 * c_width
    small_w = 2 * c_heads
    w_odd = jnp.concatenate([
        w_in_odd[:, :, :main],
        w_in_odd[:, :, main + small_w:main + small_w + 2 * gk + 2 * d_width],
        w_in_odd[:, :, main:main + small_w],
        w_in_odd[:, :, main + small_w + 2 * gk + 2 * d_width:],
        jnp.zeros((n_odd, d_model, LANES - small_w - d_rank), F32),
    ], axis=-1).astype(BF16)
    odd_e16 = w_odd.shape[-1] - LANES
    w_out_e = w_out_even.astype(BF16)
    w_out_o = w_out_odd.astype(BF16)
    w_up = ffn_w_up.astype(BF16)
    w_dn = ffn_w_down.astype(BF16)
    wa_dense = jax.vmap(_block_diag)(lru_wa).astype(BF16)
    wx_dense = jax.vmap(_block_diag)(lru_wx).astype(BF16)
    bias_small = jnp.concatenate(
        [mlstm_bi, mlstm_bf, jnp.zeros((n_odd, LANES - small_w), F32)], axis=-1)[:, None, :]
    wa2_pad = jnp.concatenate([
        jnp.zeros((n_odd, small_w, gk), F32), gla_wa2,
        jnp.zeros((n_odd, LANES - small_w - d_rank, gk), F32)], axis=1).astype(BF16)
    rows3 = lambda a: a[:, None, :]
    norm_mix3, norm_ffn3 = rows3(norm_mix), rows3(norm_ffn)
    even_params = (hgrn_lb, rows3(hgrn_norm), lru_conv_w, rows3(lru_conv_b), wa_dense, rows3(lru_ba),
                   wx_dense, rows3(lru_bx), rows3(lru_a))
    odd_params = (bias_small, rows3(mlstm_norm), wa2_pad, rows3(gla_ba), rows3(gla_norm))

    zeros = lambda *s: jnp.zeros(s, F32)
    groups = [
        dict(batch=batch, seq=seq, nseq=1, tb=PROMPT_TILES * GROUP, L=math.gcd(seq, CHUNK), pos0=0, fuse_proj=True,
             x=x_prompt.reshape(batch * seq, d_model),
             hgrn=zeros(n_even, batch, a_heads, a_dk, a_dk),
             lru_h=zeros(n_even, batch, 1, b_width),
             lru_conv=zeros(n_even, batch, CONV_K - 1, b_width),
             mC=zeros(n_odd, batch, c_heads, c_dh, c_dh),
             mn=zeros(n_odd, batch, 1, c_width),
             mm=zeros(n_odd, batch, 1, LANES),
             gla=zeros(n_odd, batch, gla_pairs, LANES, d_dv)),
        dict(batch=dec_batch, seq=dec_seq, nseq=GROUP // dec_seq, tb=dec_seq, L=math.gcd(dec_seq, CHUNK),
             pos0=PAST_LEN, fuse_proj=False,
             x=x_sample.reshape(dec_batch * dec_seq, d_model),
             hgrn=state_hgrn,
             lru_h=state_lru_h[:, :, None, :],
             lru_conv=state_lru_conv,
             mC=state_mlstm_C,
             mn=state_mlstm_n.reshape(n_odd, dec_batch, 1, c_width),
             mm=jnp.pad(state_mlstm_m, ((0, 0), (0, 0), (c_heads, LANES - 2 * c_heads)))[:, :, None, :],
             gla=state_gla.reshape(n_odd, dec_batch, gla_pairs, LANES, d_dv)),
    ]

    results = []
    for grp in groups:
        x = grp["x"]
        bsz = grp["batch"]
        hgrn_done, mc_done, gla_done = [], [], []
        small = dict(lru_h=[], lru_conv=[], mn=[], mm=[])
        prevs = lambda done, j, n_kind: done if (j == n_kind - 1 and j > 0) else []

        def z_src(x, w, l, j, e16, fuse=grp["fuse_proj"]):
            if fuse:
                return ("x", x, norm_mix3, l, w, j, e16)
            return ("z",) + tuple(_norm_matmul(x, norm_mix3, w, l, j, e16))

        for l in range(depth):
            j = l // 2
            if l % 2 == 0:
                src = z_src(x, w_even, l, j, even_e16)
                mix, s, h, cb = _even_mixer(src, grp["hgrn"], grp["lru_h"], grp["lru_conv"],
                                            prevs(hgrn_done, j, n_even), j, bsz, grp["seq"], grp["nseq"], grp["tb"],
                                            grp["L"], grp["pos0"], even_params)
                hgrn_done.append(s)
                small["lru_h"].append(h[:, 0, :])
                small["lru_conv"].append(cb)
                w_out = w_out_e
            else:
                src = z_src(x, w_odd, l, j, odd_e16)
                mix, cm, nn, mm, sg = _odd_mixer(src, grp["mC"], grp["mn"], grp["mm"], grp["gla"],
                                                 prevs(mc_done, j, n_odd), prevs(gla_done, j, n_odd), j, bsz,
                                                 grp["seq"], grp["nseq"], grp["tb"], grp["L"], odd_params)
                mc_done.append(cm)
                gla_done.append(sg)
                small["mn"].append(nn.reshape(bsz, c_heads, c_dh))
                small["mm"].append(mm[:, 0, c_heads:2 * c_heads])
                w_out = w_out_o
            x = _proj_ffn(x, mix, w_out, norm_ffn3, w_up, w_dn, norm_final[None, :], l, j,
                          final=(l == depth - 1))
        st = {k: jnp.stack(v) for k, v in small.items()}
        stacked = lambda done: done[-1] if len(done) > 1 else done[-1][None]
        st["hgrn"] = stacked(hgrn_done)
        st["mC"] = stacked(mc_done)
        st["gla"] = stacked(gla_done).reshape(n_odd, bsz, d_heads, d_dk, d_dv)
        results.append((x.reshape(bsz, grp["seq"], d_model), st))

    (y_p, st_p), (y_s, st_s) = results
    return (y_p, y_s, st_p["hgrn"], st_s["hgrn"], st_p["lru_h"], st_s["lru_h"], st_p["lru_conv"], st_s["lru_conv"],
            st_p["mC"], st_s["mC"], st_p["mn"], st_s["mn"], st_p["mm"], st_s["mm"], st_p["gla"], st_s["gla"])
```

```python
import functools
import math

import jax
import jax.numpy as jnp
import numpy as np
from jax import lax
from jax.experimental import pallas as pl
from jax.experimental.pallas import tpu as pltpu

F32 = jnp.float32
BF16 = jnp.bfloat16
EPS = 1e-6
LANES = 128
GROUP = 64
PROMPT_TILES = 8
CHUNK = 64
CONV_K = 4
LRU_C = 8.0
GLA_TAU = 16.0
PAST_LEN = 16384
MIB = 1024 * 1024
V7X_VMEM_BYTES = 64 * MIB
VMEM_LIMIT = 3 * V7X_VMEM_BYTES // 4
VMEM_LIMIT_FFN = 13 * V7X_VMEM_BYTES // 16
NEG_INF = float("-inf")


def _log1p_exp_neg_abs(x):
    return jnp.log(1.0 + jnp.exp(-jnp.abs(x)))


def _log_sigmoid(x):
    return jnp.minimum(x, 0.0) - _log1p_exp_neg_abs(x)


def _softplus(x):
    return jnp.maximum(x, 0.0) + _log1p_exp_neg_abs(x)


def _silu(x):
    return x * jax.nn.sigmoid(x)


def _gelu_tanh(x):
    return 0.5 * x * (1.0 + jnp.tanh(math.sqrt(2.0 / math.pi) * (x + 0.044715 * (x * x * x))))


def _rms(x):
    return x * lax.rsqrt(jnp.mean(x * x, axis=-1, keepdims=True) + EPS)


def _dot(a, b):
    return jnp.dot(a, b, preferred_element_type=F32)


def _dot_nt(a, b):
    return lax.dot_general(a, b, (((1,), (1,)), ((), ())), preferred_element_type=F32)


def _dot_tn(a, b):
    return lax.dot_general(a, b, (((0,), (0,)), ((), ())), preferred_element_type=F32)


def _layer_spec(shape, layer, grid_rank, **kw):
    zeros = (0,) * len(shape)
    if grid_rank == 1:
        return pl.BlockSpec((None,) + tuple(shape), lambda i: (layer,) + zeros, **kw)
    return pl.BlockSpec((None,) + tuple(shape), lambda i, j: (layer,) + zeros, **kw)


def _level_tables(L):
    t = np.arange(GROUP)[:, None]
    s = np.arange(GROUP)[None, :]
    widths = []
    w = L // 2
    while w >= 1:
        widths.append(w)
        w //= 2
    mats, pmasks = [], []
    for w in widths:
        same = (t // (2 * w)) == (s // (2 * w))
        low_t, low_s = (t % (2 * w)) >= w, (s % (2 * w)) >= w
        mats.append((same & low_t & low_s & (s <= t)) | (same & ~low_t & ~low_s & (s > t)))
        pmasks.append(same & low_t & ~low_s)
    seg = (t // L) == (s // L)
    mats.append(seg & (s <= t))
    mats.append(seg & (s > t))
    pmasks.append(t == s)
    m_all = np.concatenate(mats, axis=0).astype(np.float32)
    m3 = np.concatenate([m_all, m_all, m_all], axis=1)
    pm = np.stack(pmasks).astype(np.float32)
    return jnp.asarray(m3, BF16), jnp.asarray(np.concatenate([pm, pm, pm, pm], axis=-1)), widths


def _level_sums(m3, g):
    g_hi = g.astype(BF16)
    r1 = g - g_hi.astype(F32)
    g_mid = r1.astype(BF16)
    g_lo = (r1 - g_mid.astype(F32)).astype(BF16)
    return _dot(m3, jnp.concatenate([g_hi, g_mid, g_lo], axis=0))


def _lower_masks(widths):
    row = lax.broadcasted_iota(jnp.int32, (GROUP, LANES), 0)
    return [(row % (2 * w)) >= w for w in widths]


def _split_dot(x, table3):
    x_hi = x.astype(BF16)
    r1 = x - x_hi.astype(F32)
    x_mid = r1.astype(BF16)
    x_lo = (r1 - x_mid.astype(F32)).astype(BF16)
    return _dot(jnp.concatenate([x_hi, x_mid, x_lo], axis=1), table3)


def _expand_table(heads, dh):
    c = np.arange(LANES)[:, None]
    col = np.arange(heads * dh)[None, :] // dh
    t = np.concatenate([c == heads + col, c == col], axis=1).astype(np.float32)
    return jnp.asarray(np.concatenate([t, t, t], axis=0), BF16)


def _block_rows(xs, width):
    zero = jnp.zeros((xs[0].shape[0], width), xs[0].dtype)
    return jnp.concatenate(
        [jnp.concatenate([x if i == h else zero for i in range(len(xs))], axis=1) for h, x in enumerate(xs)], axis=0)


def _pack_heads(cols, lo_half):
    return jnp.concatenate([jnp.where(lo_half, cols[2 * p], cols[2 * p + 1]) for p in range(len(cols) // 2)], axis=1)


def _gla_heads_tile(qs, ks, es, vs, group_of, masks, pm_ref, lowers, L, states, put_state):
    nlev = len(lowers)
    nh, ng = len(vs), len(qs)
    lo_half = lax.broadcasted_iota(jnp.int32, (1, LANES), 1) < GROUP

    def blk(e, i):
        return e[GROUP * i:GROUP * (i + 1), :]

    def own(a, h):
        return a if masks[h] is None else jnp.where(masks[h], a, 0.0)

    def rhs_rows(ys):
        zero = jnp.zeros((GROUP, LANES), F32)
        return jnp.concatenate(
            [jnp.concatenate([own(ys[i], h) if i == group_of[h] else zero for i in range(ng)], axis=1)
             for h in range(nh)], axis=0)

    scores = None
    for lv in range(nlev):
        ys = [jnp.where(lowers[lv], q, k) * blk(e, lv) for q, k, e in zip(qs, ks, es)]
        part = _dot_nt(jnp.concatenate(ys, axis=1).astype(BF16), rhs_rows(ys).astype(BF16)) * pm_ref[lv]
        scores = part if scores is None else scores + part
    qks = [q * k for q, k in zip(qs, ks)]
    d = [jnp.sum(own(qks[group_of[h]], h), axis=-1, keepdims=True) for h in range(nh)]
    diag = pm_ref[nlev] * _pack_heads(d, lo_half)
    p = diag if scores is None else scores + diag
    intra = _dot(p.astype(BF16), _block_rows(vs, LANES).astype(BF16))
    vbs = [v.astype(BF16) for v in vs]
    ebs = [blk(e, nlev) for e in es]
    qes = [q * eb for q, eb in zip(qs, ebs)]
    kds = [k * blk(e, nlev + 1) for k, e in zip(ks, es)]
    pieces = [[] for _ in range(nh)]
    for j in range(GROUP // L):
        sl = slice(L * j, L * (j + 1))
        sbs = [s.astype(BF16) for s in states[j]]
        upds = [None] * ng
        for h in range(nh):
            i = group_of[h]
            pieces[h].append(intra[sl, LANES * h:LANES * (h + 1)] + _dot(own(qes[i][sl], h).astype(BF16), sbs[i]))
            t = _dot_tn(own(kds[i][sl], h).astype(BF16), vbs[h][sl])
            upds[i] = t if upds[i] is None else upds[i] + t
        for i, upd in enumerate(upds):
            last = ebs[i][L * j + L - 1:L * j + L, :]
            decay_col = jnp.transpose(jnp.broadcast_to(last, (8, LANES)))[:, 0:1]
            put_state(j, i, decay_col * states[j][i] + upd)
    return [p[0] if len(p) == 1 else jnp.concatenate(p, axis=0) for p in pieces]


def _norm_matmul_kernel(x_ref, g_ref, w_ref, o16_ref, o32_ref, *, e16, tn):
    def put16(cols, val):
        o16_ref[:, cols] = val

    def put32(val):
        o32_ref[...] = val

    _project_rows(x_ref, g_ref, w_ref, put16, put32, e16, tn)


def _norm_matmul(x, g, w, layer, wlayer, e16, tm=512, tn=512):
    n, d = x.shape
    e = w.shape[-1]
    return pl.pallas_call(
        functools.partial(_norm_matmul_kernel, e16=e16, tn=tn),
        out_shape=(jax.ShapeDtypeStruct((n, e16), BF16), jax.ShapeDtypeStruct((n, e - e16), F32)),
        grid=(n // tm,),
        in_specs=[
            pl.BlockSpec((tm, d), lambda i: (i, 0)),
            _layer_spec((1, d), layer, 1),
            _layer_spec((d, e), wlayer, 1, pipeline_mode=pl.Buffered(1)),
        ],
        out_specs=(pl.BlockSpec((tm, e16), lambda i: (i, 0)), pl.BlockSpec((tm, e - e16), lambda i: (i, 0))),
        compiler_params=pltpu.CompilerParams(
            dimension_semantics=("parallel",), vmem_limit_bytes=VMEM_LIMIT),
        name="norm_in_proj",
    )(x, g, w)


def _project_rows(x_ref, g_ref, w_ref, put16, put32, e16, tn=512):
    xn = (_rms(x_ref[...]) * g_ref[...]).astype(BF16)
    for c in range(e16 // tn):
        put16(slice(c * tn, (c + 1) * tn), _dot(xn, w_ref[:, c * tn:(c + 1) * tn]).astype(BF16))
    put32(_dot(xn, w_ref[:, e16:]))


def _z_source(head, bufs, e16):
    if bufs is None:
        return head[0], head[1], lambda: None
    x_first, x_next, g_ref, w_ref = head
    zs, zgs = bufs
    step = pl.program_id(0) * pl.num_programs(1) + pl.program_id(1)
    slot = step % 2

    def project(x_ref, dst):
        def put16(cols, val):
            zs[dst, :, cols] = val

        def put32(val):
            zgs[dst] = val

        _project_rows(x_ref, g_ref, w_ref, put16, put32, e16)

    @pl.when(step == 0)
    def _():
        project(x_first, 0)

    return zs.at[slot], zgs.at[slot], lambda: project(x_next, 1 - slot)


def _z_inputs(src, R, nt, nblocks):
    if src[0] == "z":
        _, z, zg = src
        specs = [pl.BlockSpec((R, z.shape[1]), lambda b, t: (b * nt + t, 0)),
                 pl.BlockSpec((R, zg.shape[1]), lambda b, t: (b * nt + t, 0))]
        return [z, zg], specs, [], z.shape[0], None
    _, x, g3, layer, w, wlayer, e16 = src
    d, e = x.shape[1], w.shape[-1]
    specs = [pl.BlockSpec((R, d), lambda b, t: (0, 0)),
             pl.BlockSpec((R, d), lambda b, t: (jnp.minimum(b * nt + t + 1, nblocks - 1), 0)),
             _layer_spec((1, d), layer, 2),
             _layer_spec((d, e), wlayer, 2, pipeline_mode=pl.Buffered(1))]
    scratch = [pltpu.VMEM((2, R, e16), BF16), pltpu.VMEM((2, R, e - e16), F32)]
    return [x, x, g3, w], specs, scratch, x.shape[0], e16


def _proj_ffn_kernel(x_ref, mix_ref, wo_ref, g_ref, wup_ref, wdn_ref, gf_ref, o_ref, *, d_ff, tf, final):
    xnew = x_ref[...] + _dot(mix_ref[...], wo_ref[...])
    h2 = (_rms(xnew) * g_ref[...]).astype(BF16)
    o_ref[...] = xnew
    for f in range(d_ff // tf):
        gate = _dot(h2, wup_ref[:, f * tf:(f + 1) * tf])
        val = _dot(h2, wup_ref[:, d_ff + f * tf:d_ff + (f + 1) * tf])
        act = (_silu(gate) * val).astype(BF16)
        o_ref[...] += _dot(act, wdn_ref[f * tf:(f + 1) * tf, :])
    if final:
        o_ref[...] = _rms(o_ref[...]) * gf_ref[...]


def _proj_ffn(x, mix, wo, g, wup, wdn, gf, layer, wolayer, final, tm=1024, tf=256):
    n, d = x.shape
    d_ff = wdn.shape[-2]
    once = dict(pipeline_mode=pl.Buffered(1))
    return pl.pallas_call(
        functools.partial(_proj_ffn_kernel, d_ff=d_ff, tf=tf, final=final),
        out_shape=jax.ShapeDtypeStruct((n, d), F32),
        grid=(n // tm,),
        in_specs=[
            pl.BlockSpec((tm, d), lambda i: (i, 0)),
            pl.BlockSpec((tm, d), lambda i: (i, 0)),
            _layer_spec((d, d), wolayer, 1, **once),
            _layer_spec((1, d), layer, 1, **once),
            _layer_spec((d, 2 * d_ff), layer, 1, **once),
            _layer_spec((d_ff, d), layer, 1, **once),
            pl.BlockSpec((1, d), lambda i: (0, 0), **once),
        ],
        out_specs=pl.BlockSpec((tm, d), lambda i: (i, 0)),
        compiler_params=pltpu.CompilerParams(
            dimension_semantics=("parallel",), vmem_limit_bytes=VMEM_LIMIT_FFN),
        name="out_proj_ffn",
    )(x, mix, wo, g, wup, wdn, gf)


def _even_mixer_kernel(*refs, layer_j, nseq, tb, seq, L, widths, pos0, heads, width, nprev, e16):
    nhead = 2 if e16 is None else 4
    (s_in, h_in, c_in, m3_ref, pm_ref, lb_ref, hnorm_ref, cw_ref, cb_ref, wa_ref,
     ba_ref, wx_ref, bx_ref, la_ref) = refs[nhead:nhead + 14]
    nout = nhead + 14 + nprev
    prev_states = refs[nhead + 14:nout]
    mix_ref, s_stack, h_out, c_out, mixf, xpad = refs[nout:nout + 6]
    z_ref, zg_ref, project_next = _z_source(refs[:nhead], refs[nout + 6:] or None, e16)
    s_out = s_stack.at[layer_j] if nprev else s_stack
    ti = pl.program_id(1)
    R = nseq * tb
    dk = width // heads

    @pl.when(ti == 0)
    def _():
        for i, prev in enumerate(prev_states):
            s_stack[i] = prev[...]
        s_out[...] = s_in[...]
        h_out[...] = h_in[...]
        c_out[...] = c_in[...]

    hl = lb_ref[...]
    e = jnp.exp(hl - jnp.max(hl, axis=0, keepdims=True))
    sm = e / jnp.sum(e, axis=0, keepdims=True)
    lb = jnp.zeros((1, width), F32)
    for i in range(1, layer_j + 1):
        lb = lb + sm[i:i + 1, :]
    log_lb = jnp.log(lb)
    lowers = _lower_masks(widths)

    for tidx in range(R // GROUP):
        r0 = tidx * GROUP
        rows = slice(r0, r0 + GROUP)
        seq_ids = [(r0 + L * j) // tb for j in range(GROUP // L)]
        x = zg_ref[rows, :]
        y = log_lb - x
        logf = _log_sigmoid(x) + jnp.maximum(y, 0.0) + _log1p_exp_neg_abs(y)
        key = 1.0 - jnp.exp(logf)
        qsc = z_ref[rows, 0:width].astype(F32) * (dk ** -0.5)
        e_all = jnp.exp(_level_sums(m3_ref[...], logf))
        ln = [slice(dk * h, dk * (h + 1)) for h in range(heads)]
        vs = [z_ref[rows, width + dk * h:width + dk * (h + 1)] for h in range(heads)]
        states = [[s_out[si, h] for h in range(heads)] for si in seq_ids]

        def put_state(j, i, new, seq_ids=seq_ids):
            s_out[seq_ids[j], i] = new

        outs = _gla_heads_tile([qsc[:, l] for l in ln], [key[:, l] for l in ln], [e_all[:, l] for l in ln], vs,
                               list(range(heads)), [None] * heads, pm_ref, lowers, L, states, put_state)
        for h, (l, o) in enumerate(zip(ln, outs)):
            gate = z_ref[rows, 2 * width + dk * h:2 * width + dk * (h + 1)].astype(F32)
            mixf[rows, l] = _rms(o) * hnorm_ref[:, l] * _silu(gate)

    xoff, yoff = 3 * width, 4 * width
    xpad[:, 8 - (CONV_K - 1):8, :] = c_out[...]
    xpad[:, 8:, :] = z_ref[:, xoff:xoff + width].astype(F32).reshape(nseq, tb, width)
    xc = cb_ref[...].reshape(1, 1, width) + jnp.zeros((nseq, tb, width), F32)
    for j in range(CONV_K):
        lo = 8 - (CONV_K - 1) + j
        xc = xc + xpad[:, lo:lo + tb, :] * cw_ref[j:j + 1, :].reshape(1, 1, width)
    c_out[...] = xpad[:, tb + 8 - (CONV_K - 1):tb + 8, :]
    xc = xc.reshape(R, width)
    xcb = xc.astype(BF16)
    rg = jax.nn.sigmoid(_dot(xcb, wa_ref[...]) + ba_ref[...])
    ig = jax.nn.sigmoid(_dot(xcb, wx_ref[...]) + bx_ref[...])
    log_a = (-LRU_C) * rg * _softplus(-la_ref[...])
    a = jnp.exp(log_a)
    mult = jnp.sqrt(1.0 - a * a)
    row_in_seq = lax.broadcasted_iota(jnp.int32, (R, width), 0) % tb
    if pos0 <= 0 < pos0 + seq:
        mult = jnp.where(row_in_seq + ti * tb + pos0 == 0, 1.0, mult)
    hh = mult * ig * xc
    aa = a
    shift = 1
    while shift < tb:
        m = row_in_seq >= shift
        hh_new = jnp.where(m, aa * pltpu.roll(hh, shift, 0) + hh, hh)
        aa = jnp.where(m, aa * pltpu.roll(aa, shift, 0), aa)
        hh = hh_new
        shift *= 2
    hseq = hh.reshape(nseq, tb, width) + aa.reshape(nseq, tb, width) * h_out[...]
    h_out[...] = hseq[:, tb - 1:tb, :]
    mixf[:, width:2 * width] = hseq.reshape(R, width) * _gelu_tanh(z_ref[:, yoff:yoff + width].astype(F32))
    mix_ref[...] = mixf[...].astype(BF16)
    project_next()


def _state_out(prevs, n_layers, batch, nseq, tail):
    zeros = (0,) * len(tail)
    if prevs:
        return (jax.ShapeDtypeStruct((n_layers, batch) + tail, F32),
                pl.BlockSpec((n_layers, nseq) + tail, lambda b, t: (0, b) + zeros))
    return (jax.ShapeDtypeStruct((batch,) + tail, F32), pl.BlockSpec((nseq,) + tail, lambda b, t: (b,) + zeros))


def _even_mixer(src, s_all, h_all, c_all, s_prevs, layer_j, batch, seq, nseq, tb, L, pos0, params):
    (hgrn_lb, hgrn_norm, conv_w, conv_b, wa, ba, wx, bx, lru_a) = params
    n_layers, _, heads, dk, _ = s_all.shape
    width = heads * dk
    R = nseq * tb
    nt = seq // tb
    z_in, z_specs, z_scratch, n, e16 = _z_inputs(src, R, nt, (batch // nseq) * nt)
    m3, pm, widths = _level_tables(L)
    const = lambda a: pl.BlockSpec(a.shape, lambda b, t: (0,) * a.ndim)
    lay = lambda a: _layer_spec(a.shape[1:], layer_j, 2)
    kern = functools.partial(_even_mixer_kernel, layer_j=layer_j, nseq=nseq, tb=tb, seq=seq, L=L, widths=widths,
                             pos0=pos0, heads=heads, width=width, nprev=len(s_prevs), e16=e16)
    s_shape, s_spec = _state_out(s_prevs, n_layers, batch, nseq, (heads, dk, dk))
    sblk = (None, nseq, heads, dk, dk)
    hblk = (None, nseq, 1, width)
    cblk = (None, nseq, CONV_K - 1, width)
    inputs = z_in + [s_all, h_all, c_all, m3, pm, hgrn_lb, hgrn_norm, conv_w, conv_b, wa, ba, wx, bx, lru_a]
    in_specs = z_specs + [
        pl.BlockSpec(sblk, lambda b, t: (layer_j, b, 0, 0, 0)),
        pl.BlockSpec(hblk, lambda b, t: (layer_j, b, 0, 0)),
        pl.BlockSpec(cblk, lambda b, t: (layer_j, b, 0, 0)),
        const(m3), const(pm), const(hgrn_lb), lay(hgrn_norm), lay(conv_w), lay(conv_b),
        lay(wa), lay(ba), lay(wx), lay(bx), lay(lru_a),
    ]
    inputs += list(s_prevs)
    in_specs += [pl.BlockSpec(sblk[1:], lambda b, t: (b, 0, 0, 0))] * len(s_prevs)
    return pl.pallas_call(
        kern,
        out_shape=(
            jax.ShapeDtypeStruct((n, 2 * width), BF16),
            s_shape,
            jax.ShapeDtypeStruct((batch, 1, width), F32),
            jax.ShapeDtypeStruct((batch, CONV_K - 1, width), F32),
        ),
        grid=(batch // nseq, nt),
        in_specs=in_specs,
        out_specs=(
            pl.BlockSpec((R, 2 * width), lambda b, t: (b * nt + t, 0)),
            s_spec,
            pl.BlockSpec(hblk[1:], lambda b, t: (b, 0, 0)),
            pl.BlockSpec(cblk[1:], lambda b, t: (b, 0, 0)),
        ),
        scratch_shapes=[
            pltpu.VMEM((R, 2 * width), F32),
            pltpu.VMEM((nseq, tb + 8, width), F32),
        ] + z_scratch,
        compiler_params=pltpu.CompilerParams(
            dimension_semantics=("arbitrary", "arbitrary"), vmem_limit_bytes=VMEM_LIMIT),
        name="even_mixer",
    )(*inputs)


def _odd_mixer_kernel(*refs, layer_j, nseq, tb, L, widths, heads, dh, gla_pairs, nprev, e16):
    nhead = 2 if e16 is None else 4
    (c_in, n_in, m_in, g_in, m3_ref, pm_ref, ex3_ref, bias_ref, mnorm_ref, wa2_ref, gba_ref,
     gnorm_ref) = refs[nhead:nhead + 12]
    nout = nhead + 12 + 2 * nprev
    prev_c, prev_g = refs[nhead + 12:nhead + 12 + nprev], refs[nhead + 12 + nprev:nout]
    mix_ref, c_stack, n_out, m_out, g_stack, mixf = refs[nout:nout + 6]
    z_ref, zg_ref, project_next = _z_source(refs[:nhead], refs[nout + 6:] or None, e16)
    c_out = c_stack.at[layer_j] if nprev else c_stack
    g_out = g_stack.at[layer_j] if nprev else g_stack
    ti = pl.program_id(1)
    R = nseq * tb
    cw = heads * dh
    gk = gla_pairs * LANES
    q_off, k_off, v_off, o_off = 0, cw, 2 * cw, 3 * cw
    gq_off = 4 * cw
    gk_off = gq_off + gk
    gv_off = gk_off + gk
    gg_off = gv_off + cw
    gla_dk = gk // heads
    nlev = len(widths)

    @pl.when(ti == 0)
    def _():
        for i in range(nprev):
            c_stack[i] = prev_c[i][...]
            g_stack[i] = prev_g[i][...]
        c_out[...] = c_in[...]
        n_out[...] = n_in[...]
        m_out[...] = m_in[...]
        g_out[...] = g_in[...]

    lane = lax.broadcasted_iota(jnp.int32, (1, LANES), 1)
    is_f = (lane >= heads) & (lane < 2 * heads)
    lo_half = lane < LANES // 2
    hi_half = lane >= LANES // 2
    trow = lax.broadcasted_iota(jnp.int32, (GROUP, heads * GROUP), 0)
    tcol = lax.broadcasted_iota(jnp.int32, (GROUP, heads * GROUP), 1) % GROUP
    seg_tril = (tcol <= trow) & ((trow // L) == (tcol // L))
    gla_masks = [lo_half, hi_half]
    pack = functools.partial(_pack_heads, lo_half=lo_half)

    def head_reduce(x, h, fn, fill):
        col = x[:, LANES * (h // 2):LANES * (h // 2 + 1)]
        return fn(jnp.where(lo_half if h % 2 == 0 else hi_half, col, fill), axis=-1, keepdims=True)
    lowers = _lower_masks(widths)

    for tidx in range(R // GROUP):
        r0 = tidx * GROUP
        rows = slice(r0, r0 + GROUP)
        seq_ids = [(r0 + L * j) // tb for j in range(GROUP // L)]
        small = zg_ref[rows, :]
        g0 = small + bias_ref[...]
        pre = _dot(small.astype(BF16), wa2_ref[...]) + gba_ref[...]
        la = _log_sigmoid(pre) * (1.0 / GLA_TAU)
        sums = _level_sums(m3_ref[...], jnp.concatenate([la, _log_sigmoid(g0)], axis=-1))
        e_all = jnp.exp(sums[:, 0:gk])
        gates = jnp.where(is_f, sums[GROUP * nlev:GROUP * (nlev + 1), gk:gk + LANES], g0)

        kl = [slice(LANES * p, LANES * (p + 1)) for p in range(gla_pairs)]
        vs = [z_ref[rows, gv_off + dh * h:gv_off + dh * (h + 1)] for h in range(heads)]
        qsc = [z_ref[rows, gq_off + LANES * p:gq_off + LANES * (p + 1)].astype(F32) * (gla_dk ** -0.5)
               for p in range(gla_pairs)]
        kk = [z_ref[rows, gk_off + LANES * p:gk_off + LANES * (p + 1)].astype(F32) for p in range(gla_pairs)]
        states = [[g_out[si, p] for p in range(gla_pairs)] for si in seq_ids]

        def put_state(j, i, new, seq_ids=seq_ids):
            g_out[seq_ids[j], i] = new

        outs = _gla_heads_tile(qsc, kk, [e_all[:, l] for l in kl], vs, [h // 2 for h in range(heads)],
                               [gla_masks[h % 2] for h in range(heads)], pm_ref, lowers, L, states, put_state)
        for h, o in enumerate(outs):
            vl = slice(dh * h, dh * (h + 1))
            gg = z_ref[rows, gg_off + dh * h:gg_off + dh * (h + 1)].astype(F32)
            mixf[rows, cw + dh * h:cw + dh * (h + 1)] = _rms(o) * gnorm_ref[:, vl] * _silu(gg)

        m_next = [m_out[si] for si in seq_ids]
        ex = _split_dot(gates, ex3_ref[...])
        bcol = [ex[:, dh * h:dh * (h + 1)] for h in range(heads)]
        icol = [ex[:, cw + dh * h:cw + dh * (h + 1)] for h in range(heads)]
        mprev = [[jnp.broadcast_to(mv[:, heads + h:heads + h + 1], (1, dh)) for mv in m_next] for h in range(heads)]
        inter = []
        for h in range(heads):
            parts = [bcol[h][L * j:L * (j + 1)] + mprev[h][j] for j in range(GROUP // L)]
            inter.append(parts[0] if len(parts) == 1 else jnp.concatenate(parts, axis=0))
        gtt = jnp.transpose(gates)
        urow = jnp.concatenate([gtt[h:h + 1, :] - gtt[heads + h:heads + h + 1, :] for h in range(heads)], axis=1)
        dlog = jnp.where(seg_tril, pack(bcol) + urow, NEG_INF)
        m_t = [jnp.maximum(inter[h], head_reduce(dlog, h, jnp.max, NEG_INF)) for h in range(heads)]
        w_intra = jnp.exp(dlog - pack(m_t))
        w_inter = [jnp.exp(inter[h] - m_t[h]) for h in range(heads)]
        q_t = [z_ref[rows, q_off + dh * h:q_off + dh * (h + 1)].astype(F32) for h in range(heads)]
        k_t = [z_ref[rows, k_off + dh * h:k_off + dh * (h + 1)].astype(F32) * (dh ** -0.5) for h in range(heads)]
        v_t = [z_ref[rows, v_off + dh * h:v_off + dh * (h + 1)] for h in range(heads)]
        qk = _dot_nt(jnp.concatenate(q_t, axis=1).astype(BF16), _block_rows(k_t, dh).astype(BF16)) * w_intra
        num_all = _dot(qk.astype(BF16), _block_rows(v_t, dh).astype(BF16))
        for h in range(heads):
            lanes = slice(dh * h, dh * (h + 1))
            qk_sum = head_reduce(qk, h, jnp.sum, 0.0)
            cells = []
            for j in range(GROUP // L):
                sl = slice(L * j, L * (j + 1))
                si = seq_ids[j]
                q, k, vb = q_t[h][sl], k_t[h][sl], v_t[h][sl].astype(BF16)
                cmat = c_out[si, h]
                nrow = n_out[si, :, lanes]
                wi = w_inter[h][sl]
                num = num_all[sl, lanes] + wi * _dot(q.astype(BF16), cmat.astype(BF16))
                den = qk_sum[sl] + wi * jnp.sum(q * nrow, axis=-1, keepdims=True)
                cells.append(num / jnp.maximum(jnp.abs(den), jnp.exp(-m_t[h][sl])))
                b_last = bcol[h][L * j + L - 1:L * j + L, :]
                m_prev = mprev[h][j]
                wlog = b_last - bcol[h][sl] + icol[h][sl]
                m_new = jnp.maximum(b_last + m_prev, jnp.max(wlog, axis=0, keepdims=True))
                w_s = jnp.exp(wlog - m_new)
                decay = jnp.exp(b_last + m_prev - m_new)
                kw = k * w_s
                c_out[si, h] = decay * cmat + _dot_tn(kw.astype(BF16), vb)
                n_out[si, :, lanes] = decay * nrow + jnp.sum(kw, axis=0, keepdims=True)
                m_next[j] = jnp.where(lane == heads + h, m_new, m_next[j])
            hcell = cells[0] if len(cells) == 1 else jnp.concatenate(cells, axis=0)
            ogate = z_ref[rows, o_off + dh * h:o_off + dh * (h + 1)].astype(F32)
            mixf[rows, lanes] = _rms(hcell) * mnorm_ref[:, lanes] * jax.nn.sigmoid(ogate)
        for j, si in enumerate(seq_ids):
            m_out[si] = m_next[j]
    mix_ref[...] = mixf[...].astype(BF16)
    project_next()


def _odd_mixer(src, c_all, n_all, m_all, g_all, c_prevs, g_prevs, layer_j, batch, seq, nseq, tb, L, params):
    (bias_small, mnorm, wa2_pad, gla_ba, gnorm) = params
    n_layers, _, heads, dh, _ = c_all.shape
    cw = heads * dh
    gla_pairs = g_all.shape[2]
    R = nseq * tb
    nt = seq // tb
    z_in, z_specs, z_scratch, n, e16 = _z_inputs(src, R, nt, (batch // nseq) * nt)
    m3, pm, widths = _level_tables(L)
    const = lambda a: pl.BlockSpec(a.shape, lambda b, t: (0,) * a.ndim)
    lay = lambda a: _layer_spec(a.shape[1:], layer_j, 2)
    kern = functools.partial(_odd_mixer_kernel, layer_j=layer_j, nseq=nseq, tb=tb, L=L, widths=widths, heads=heads,
                             dh=dh, gla_pairs=gla_pairs, nprev=len(c_prevs), e16=e16)
    c_shape, c_spec = _state_out(c_prevs, n_layers, batch, nseq, (heads, dh, dh))
    g_shape, g_spec = _state_out(g_prevs, n_layers, batch, nseq, (gla_pairs, LANES, dh))
    cblk = (None, nseq, heads, dh, dh)
    nblk = (None, nseq, 1, cw)
    mblk = (None, nseq, 1, LANES)
    gblk = (None, nseq, gla_pairs, LANES, dh)
    ex3 = _expand_table(heads, dh)
    inputs = z_in + [c_all, n_all, m_all, g_all, m3, pm, ex3, bias_small, mnorm, wa2_pad, gla_ba, gnorm]
    in_specs = z_specs + [
        pl.BlockSpec(cblk, lambda b, t: (layer_j, b, 0, 0, 0)),
        pl.BlockSpec(nblk, lambda b, t: (layer_j, b, 0, 0)),
        pl.BlockSpec(mblk, lambda b, t: (layer_j, b, 0, 0)),
        pl.BlockSpec(gblk, lambda b, t: (layer_j, b, 0, 0, 0)),
        const(m3), const(pm), const(ex3), lay(bias_small), lay(mnorm), lay(wa2_pad), lay(gla_ba), lay(gnorm),
    ]
    inputs += list(c_prevs) + list(g_prevs)
    in_specs += ([pl.BlockSpec(cblk[1:], lambda b, t: (b, 0, 0, 0))] * len(c_prevs)
                 + [pl.BlockSpec(gblk[1:], lambda b, t: (b, 0, 0, 0))] * len(g_prevs))
    return pl.pallas_call(
        kern,
        out_shape=(
            jax.ShapeDtypeStruct((n, 2 * cw), BF16),
            c_shape,
            jax.ShapeDtypeStruct((batch, 1, cw), F32),
            jax.ShapeDtypeStruct((batch, 1, LANES), F32),
            g_shape,
        ),
        grid=(batch // nseq, nt),
        in_specs=in_specs,
        out_specs=(
            pl.BlockSpec((R, 2 * cw), lambda b, t: (b * nt + t, 0)),
            c_spec,
            pl.BlockSpec(nblk[1:], lambda b, t: (b, 0, 0)),
            pl.BlockSpec(mblk[1:], lambda b, t: (b, 0, 0)),
            g_spec,
        ),
        scratch_shapes=[
            pltpu.VMEM((R, 2 * cw), F32),
        ] + z_scratch,
        compiler_params=pltpu.CompilerParams(
            dimension_semantics=("arbitrary", "arbitrary"), vmem_limit_bytes=VMEM_LIMIT),
        name="odd_mixer",
    )(*inputs)


def _block_diag(w):
    nb, ci, di = w.shape
    eye = jnp.eye(nb, dtype=w.dtype)
    return (eye[:, None, :, None] * w[:, :, None, :]).reshape(nb * ci, nb * di)


def kernel(x_prompt, x_sample, state_hgrn, state_lru_h, state_lru_conv, state_mlstm_C, state_mlstm_n, state_mlstm_m, state_gla, norm_mix, norm_ffn, norm_final, w_in_even, hgrn_lb, hgrn_norm, lru_conv_w, lru_conv_b, lru_wa, lru_ba, lru_wx, lru_bx, lru_a, w_out_even, w_in_odd, mlstm_bi, mlstm_bf, mlstm_norm, gla_wa2, gla_ba, gla_norm, w_out_odd, ffn_w_up, ffn_w_down):
    batch, seq, d_model = x_prompt.shape
    dec_batch, dec_seq, _ = x_sample.shape
    depth = norm_mix.shape[0]
    n_even, _, a_heads, a_dk, _ = state_hgrn.shape
    n_odd, _, c_heads, c_dh, _ = state_mlstm_C.shape
    b_width = state_lru_h.shape[-1]
    c_width = c_heads * c_dh
    d_heads, d_dk, d_dv = state_gla.shape[2:]
    gk = d_heads * d_dk
    d_width = d_heads * d_dv
    d_rank = gla_wa2.shape[1]
    gla_pairs = gk // LANES
    assert a_heads * a_dk == b_width and GROUP % dec_seq == 0 and seq % (PROMPT_TILES * GROUP) == 0

    a_width = a_heads * a_dk
    w_even = jnp.concatenate([w_in_even[:, :, :a_width], w_in_even[:, :, 2 * a_width:],
                              w_in_even[:, :, a_width:2 * a_width]], axis=-1).astype(BF16)
    even_e16 = w_even.shape[-1] - a_width
    main = 4 * c_width
    small_w = 2 * c_heads
    w_odd = jnp.concatenate([
        w_in_odd[:, :, :main],
        w_in_odd[:, :, main + small_w:main + small_w + 2 * gk + 2 * d_width],
        w_in_odd[:, :, main:main + small_w],
        w_in_odd[:, :, main + small_w + 2 * gk + 2 * d_width:],
        jnp.zeros((n_odd, d_model, LANES - small_w - d_rank), F32),
    ], axis=-1).astype(BF16)
    odd_e16 = w_odd.shape[-1] - LANES
    w_out_e = w_out_even.astype(BF16)
    w_out_o = w_out_odd.astype(BF16)
    w_up = ffn_w_up.astype(BF16)
    w_dn = ffn_w_down.astype(BF16)
    wa_dense = jax.vmap(_block_diag)(lru_wa).astype(BF16)
    wx_dense = jax.vmap(_block_diag)(lru_wx).astype(BF16)
    bias_small = jnp.concatenate(
        [mlstm_bi, mlstm_bf, jnp.zeros((n_odd, LANES - small_w), F32)], axis=-1)[:, None, :]
    wa2_pad = jnp.concatenate([
        jnp.zeros((n_odd, small_w, gk), F32), gla_wa2,
        jnp.zeros((n_odd, LANES - small_w - d_rank, gk), F32)], axis=1).astype(BF16)
    rows3 = lambda a: a[:, None, :]
    norm_mix3, norm_ffn3 = rows3(norm_mix), rows3(norm_ffn)
    even_params = (hgrn_lb, rows3(hgrn_norm), lru_conv_w, rows3(lru_conv_b), wa_dense, rows3(lru_ba),
                   wx_dense, rows3(lru_bx), rows3(lru_a))
    odd_params = (bias_small, rows3(mlstm_norm), wa2_pad, rows3(gla_ba), rows3(gla_norm))

    zeros = lambda *s: jnp.zeros(s, F32)
    groups = [
        dict(batch=batch, seq=seq, nseq=1, tb=PROMPT_TILES * GROUP, L=math.gcd(seq, CHUNK), pos0=0, fuse_proj=True,
             x=x_prompt.reshape(batch * seq, d_model),
             hgrn=zeros(n_even, batch, a_heads, a_dk, a_dk),
             lru_h=zeros(n_even, batch, 1, b_width),
             lru_conv=zeros(n_even, batch, CONV_K - 1, b_width),
             mC=zeros(n_odd, batch, c_heads, c_dh, c_dh),
             mn=zeros(n_odd, batch, 1, c_width),
             mm=zeros(n_odd, batch, 1, LANES),
             gla=zeros(n_odd, batch, gla_pairs, LANES, d_dv)),
        dict(batch=dec_batch, seq=dec_seq, nseq=GROUP // dec_seq, tb=dec_seq, L=math.gcd(dec_seq, CHUNK),
             pos0=PAST_LEN, fuse_proj=False,
             x=x_sample.reshape(dec_batch * dec_seq, d_model),
             hgrn=state_hgrn,
             lru_h=state_lru_h[:, :, None, :],
             lru_conv=state_lru_conv,
             mC=state_mlstm_C,
             mn=state_mlstm_n.reshape(n_odd, dec_batch, 1, c_width),
             mm=jnp.pad(state_mlstm_m, ((0, 0), (0, 0), (c_heads, LANES - 2 * c_heads)))[:, :, None, :],
             gla=state_gla.reshape(n_odd, dec_batch, gla_pairs, LANES, d_dv)),
    ]

    results = []
    for grp in groups:
        x = grp["x"]
        bsz = grp["batch"]
        hgrn_done, mc_done, gla_done = [], [], []
        small = dict(lru_h=[], lru_conv=[], mn=[], mm=[])
        prevs = lambda done, j, n_kind: done if (j == n_kind - 1 and j > 0) else []

        def z_src(x, w, l, j, e16, fuse=grp["fuse_proj"]):
            if fuse:
                return ("x", x, norm_mix3, l, w, j, e16)
            return ("z",) + tuple(_norm_matmul(x, norm_mix3, w, l, j, e16))

        for l in range(depth):
            j = l // 2
            if l % 2 == 0:
                src = z_src(x, w_even, l, j, even_e16)
                mix, s, h, cb = _even_mixer(src, grp["hgrn"], grp["lru_h"], grp["lru_conv"],
                                            prevs(hgrn_done, j, n_even), j, bsz, grp["seq"], grp["nseq"], grp["tb"],
                                            grp["L"], grp["pos0"], even_params)
                hgrn_done.append(s)
                small["lru_h"].append(h[:, 0, :])
                small["lru_conv"].append(cb)
                w_out = w_out_e
            else:
                src = z_src(x, w_odd, l, j, odd_e16)
                mix, cm, nn, mm, sg = _odd_mixer(src, grp["mC"], grp["mn"], grp["mm"], grp["gla"],
                                                 prevs(mc_done, j, n_odd), prevs(gla_done, j, n_odd), j, bsz,
                                                 grp["seq"], grp["nseq"], grp["tb"], grp["L"], odd_params)
                mc_done.append(cm)
                gla_done.append(sg)
                small["mn"].append(nn.reshape(bsz, c_heads, c_dh))
                small["mm"].append(mm[:, 0, c_heads:2 * c_heads])
                w_out = w_out_o
            x = _proj_ffn(x, mix, w_out, norm_ffn3, w_up, w_dn, norm_final[None, :], l, j,
                          final=(l == depth - 1))
        st = {k: jnp.stack(v) for k, v in small.items()}
        stacked = lambda done: done[-1] if len(done) > 1 else done[-1][None]
        st["hgrn"] = stacked(hgrn_done)
        st["mC"] = stacked(mc_done)
        st["gla"] = stacked(gla_done).reshape(n_odd, bsz, d_heads, d_dk, d_dv)
        results.append((x.reshape(bsz, grp["seq"], d_model), st))

    (y_p, st_p), (y_s, st_s) = results
    return (y_p, y_s, st_p["hgrn"], st_s["hgrn"], st_p["lru_h"], st_s["lru_h"], st_p["lru_conv"], st_s["lru_conv"],
            st_p["mC"], st_s["mC"], st_p["mn"], st_s["mn"], st_p["mm"], st_s["mm"], st_p["gla"], st_s["gla"])
```

```python
import functools
import math

import jax
import jax.numpy as jnp
import numpy as np
from jax import lax
from jax.experimental import pallas as pl
from jax.experimental.pallas import tpu as pltpu

F32 = jnp.float32
BF16 = jnp.bfloat16
EPS = 1e-6
LANES = 128
GROUP = 64
PROMPT_TILES = 8
CHUNK = 64
CONV_K = 4
LRU_C = 8.0
GLA_TAU = 16.0
PAST_LEN = 16384
MIB = 1024 * 1024
V7X_VMEM_BYTES = 64 * MIB
VMEM_LIMIT = 3 * V7X_VMEM_BYTES // 4
VMEM_LIMIT_FFN = 13 * V7X_VMEM_BYTES // 16
NEG_INF = float("-inf")


def _log1p_exp_neg_abs(x):
    return jnp.log(1.0 + jnp.exp(-jnp.abs(x)))


def _log_sigmoid(x):
    return jnp.minimum(x, 0.0) - _log1p_exp_neg_abs(x)


def _softplus(x):
    return jnp.maximum(x, 0.0) + _log1p_exp_neg_abs(x)


def _silu(x):
    return x * jax.nn.sigmoid(x)


def _gelu_tanh(x):
    return 0.5 * x * (1.0 + jnp.tanh(math.sqrt(2.0 / math.pi) * (x + 0.044715 * (x * x * x))))


def _rms(x):
    return x * lax.rsqrt(jnp.mean(x * x, axis=-1, keepdims=True) + EPS)


def _dot(a, b):
    return jnp.dot(a, b, preferred_element_type=F32)


def _dot_nt(a, b):
    return lax.dot_general(a, b, (((1,), (1,)), ((), ())), preferred_element_type=F32)


def _dot_tn(a, b):
    return lax.dot_general(a, b, (((0,), (0,)), ((), ())), preferred_element_type=F32)


def _layer_spec(shape, layer, grid_rank, **kw):
    zeros = (0,) * len(shape)
    if grid_rank == 1:
        return pl.BlockSpec((None,) + tuple(shape), lambda i: (layer,) + zeros, **kw)
    return pl.BlockSpec((None,) + tuple(shape), lambda i, j: (layer,) + zeros, **kw)


def _level_tables(L):
    t = np.arange(GROUP)[:, None]
    s = np.arange(GROUP)[None, :]
    widths = []
    w = L // 2
    while w >= 1:
        widths.append(w)
        w //= 2
    mats, pmasks = [], []
    for w in widths:
        same = (t // (2 * w)) == (s // (2 * w))
        low_t, low_s = (t % (2 * w)) >= w, (s % (2 * w)) >= w
        mats.append((same & low_t & low_s & (s <= t)) | (same & ~low_t & ~low_s & (s > t)))
        pmasks.append(same & low_t & ~low_s)
    seg = (t // L) == (s // L)
    mats.append(seg & (s <= t))
    mats.append(seg & (s > t))
    pmasks.append(t == s)
    m_all = np.concatenate(mats, axis=0).astype(np.float32)
    m3 = np.concatenate([m_all, m_all, m_all], axis=1)
    pm = np.stack(pmasks).astype(np.float32)
    return jnp.asarray(m3, BF16), jnp.asarray(np.concatenate([pm, pm, pm, pm], axis=-1)), widths


def _level_sums(m3, g):
    g_hi = g.astype(BF16)
    r1 = g - g_hi.astype(F32)
    g_mid = r1.astype(BF16)
    g_lo = (r1 - g_mid.astype(F32)).astype(BF16)
    return _dot(m3, jnp.concatenate([g_hi, g_mid, g_lo], axis=0))


def _lower_masks(widths):
    row = lax.broadcasted_iota(jnp.int32, (GROUP, LANES), 0)
    return [(row % (2 * w)) >= w for w in widths]


def _split_dot(x, table3):
    x_hi = x.astype(BF16)
    r1 = x - x_hi.astype(F32)
    x_mid = r1.astype(BF16)
    x_lo = (r1 - x_mid.astype(F32)).astype(BF16)
    return _dot(jnp.concatenate([x_hi, x_mid, x_lo], axis=1), table3)


def _expand_table(heads, dh):
    c = np.arange(LANES)[:, None]
    col = np.arange(heads * dh)[None, :] // dh
    t = np.concatenate([c == heads + col, c == col], axis=1).astype(np.float32)
    return jnp.asarray(np.concatenate([t, t, t], axis=0), BF16)


def _block_rows(xs, width):
    zero = jnp.zeros((xs[0].shape[0], width), xs[0].dtype)
    return jnp.concatenate(
        [jnp.concatenate([x if i == h else zero for i in range(len(xs))], axis=1) for h, x in enumerate(xs)], axis=0)


def _pack_heads(cols, lo_half):
    return jnp.concatenate([jnp.where(lo_half, cols[2 * p], cols[2 * p + 1]) for p in range(len(cols) // 2)], axis=1)


def _gla_heads_tile(qs, ks, es, vs, group_of, masks, pm_ref, lowers, L, states, put_state):
    nlev = len(lowers)
    nh, ng = len(vs), len(qs)
    lo_half = lax.broadcasted_iota(jnp.int32, (1, LANES), 1) < GROUP

    def blk(e, i):
        return e[GROUP * i:GROUP * (i + 1), :]

    def own(a, h):
        return a if masks[h] is None else jnp.where(masks[h], a, 0.0)

    def rhs_rows(ys):
        zero = jnp.zeros((GROUP, LANES), F32)
        return jnp.concatenate(
            [jnp.concatenate([own(ys[i], h) if i == group_of[h] else zero for i in range(ng)], axis=1)
             for h in range(nh)], axis=0)

    scores = None
    for lv in range(nlev):
        ys = [jnp.where(lowers[lv], q, k) * blk(e, lv) for q, k, e in zip(qs, ks, es)]
        part = _dot_nt(jnp.concatenate(ys, axis=1).astype(BF16), rhs_rows(ys).astype(BF16)) * pm_ref[lv]
        scores = part if scores is None else scores + part
    qks = [q * k for q, k in zip(qs, ks)]
    d = [jnp.sum(own(qks[group_of[h]], h), axis=-1, keepdims=True) for h in range(nh)]
    diag = pm_ref[nlev] * _pack_heads(d, lo_half)
    p = diag if scores is None else scores + diag
    intra = _dot(p.astype(BF16), _block_rows(vs, LANES).astype(BF16))
    vbs = [v.astype(BF16) for v in vs]
    ebs = [blk(e, nlev) for e in es]
    qes = [q * eb for q, eb in zip(qs, ebs)]
    kds = [k * blk(e, nlev + 1) for k, e in zip(ks, es)]
    pieces = [[] for _ in range(nh)]
    for j in range(GROUP // L):
        sl = slice(L * j, L * (j + 1))
        sbs = [s.astype(BF16) for s in states[j]]
        upds = [None] * ng
        for h in range(nh):
            i = group_of[h]
            pieces[h].append(intra[sl, LANES * h:LANES * (h + 1)] + _dot(own(qes[i][sl], h).astype(BF16), sbs[i]))
            t = _dot_tn(own(kds[i][sl], h).astype(BF16), vbs[h][sl])
            upds[i] = t if upds[i] is None else upds[i] + t
        for i, upd in enumerate(upds):
            last = ebs[i][L * j + L - 1:L * j + L, :]
            decay_col = jnp.transpose(jnp.broadcast_to(last, (8, LANES)))[:, 0:1]
            put_state(j, i, decay_col * states[j][i] + upd)
    return [p[0] if len(p) == 1 else jnp.concatenate(p, axis=0) for p in pieces]


def _norm_matmul_kernel(x_ref, g_ref, w_ref, o16_ref, o32_ref, *, e16, tn):
    def put16(cols, val):
        o16_ref[:, cols] = val

    def put32(val):
        o32_ref[...] = val

    _project_rows(x_ref, g_ref, w_ref, put16, put32, e16, tn)


def _norm_matmul(x, g, w, layer, wlayer, e16, tm=512, tn=512):
    n, d = x.shape
    e = w.shape[-1]
    return pl.pallas_call(
        functools.partial(_norm_matmul_kernel, e16=e16, tn=tn),
        out_shape=(jax.ShapeDtypeStruct((n, e16), BF16), jax.ShapeDtypeStruct((n, e - e16), F32)),
        grid=(n // tm,),
        in_specs=[
            pl.BlockSpec((tm, d), lambda i: (i, 0)),
            _layer_spec((1, d), layer, 1),
            _layer_spec((d, e), wlayer, 1, pipeline_mode=pl.Buffered(1)),
        ],
        out_specs=(pl.BlockSpec((tm, e16), lambda i: (i, 0)), pl.BlockSpec((tm, e - e16), lambda i: (i, 0))),
        compiler_params=pltpu.CompilerParams(
            dimension_semantics=("parallel",), vmem_limit_bytes=VMEM_LIMIT),
        name="norm_in_proj",
    )(x, g, w)


def _project_rows(x_ref, g_ref, w_ref, put16, put32, e16, tn=512):
    xn = (_rms(x_ref[...]) * g_ref[...]).astype(BF16)
    for c in range(e16 // tn):
        put16(slice(c * tn, (c + 1) * tn), _dot(xn, w_ref[:, c * tn:(c + 1) * tn]).astype(BF16))
    put32(_dot(xn, w_ref[:, e16:]))


def _z_source(head, bufs, e16):
    if bufs is None:
        return head[0], head[1], lambda: None
    x_first, x_next, g_ref, w_ref = head
    zs, zgs = bufs
    step = pl.program_id(0) * pl.num_programs(1) + pl.program_id(1)
    slot = step % 2

    def project(x_ref, dst):
        def put16(cols, val):
            zs[dst, :, cols] = val

        def put32(val):
            zgs[dst] = val

        _project_rows(x_ref, g_ref, w_ref, put16, put32, e16)

    @pl.when(step == 0)
    def _():
        project(x_first, 0)

    return zs.at[slot], zgs.at[slot], lambda: project(x_next, 1 - slot)


def _z_inputs(src, R, nt, nblocks):
    if src[0] == "z":
        _, z, zg = src
        specs = [pl.BlockSpec((R, z.shape[1]), lambda b, t: (b * nt + t, 0)),
                 pl.BlockSpec((R, zg.shape[1]), lambda b, t: (b * nt + t, 0))]
        return [z, zg], specs, [], z.shape[0], None
    _, x, g3, layer, w, wlayer, e16 = src
    d, e = x.shape[1], w.shape[-1]
    specs = [pl.BlockSpec((R, d), lambda b, t: (0, 0)),
             pl.BlockSpec((R, d), lambda b, t: (jnp.minimum(b * nt + t + 1, nblocks - 1), 0)),
             _layer_spec((1, d), layer, 2),
             _layer_spec((d, e), wlayer, 2, pipeline_mode=pl.Buffered(1))]
    scratch = [pltpu.VMEM((2, R, e16), BF16), pltpu.VMEM((2, R, e - e16), F32)]
    return [x, x, g3, w], specs, scratch, x.shape[0], e16


def _proj_ffn_kernel(x_ref, mix_ref, wo_ref, g_ref, wup_ref, wdn_ref, gf_ref, o_ref, act_ref, *, d_ff, tf, final):
    xnew = x_ref[...] + _dot(mix_ref[...], wo_ref[...])
    h2 = (_rms(xnew) * g_ref[...]).astype(BF16)
    for f in range(d_ff // tf):
        cols = slice(f * tf, (f + 1) * tf)
        gate = _dot(h2, wup_ref[:, cols])
        val = _dot(h2, wup_ref[:, d_ff + f * tf:d_ff + (f + 1) * tf])
        act_ref[:, cols] = (_silu(gate) * val).astype(BF16)
    y = xnew + _dot(act_ref[...], wdn_ref[...])
    o_ref[...] = _rms(y) * gf_ref[...] if final else y


def _proj_ffn(x, mix, wo, g, wup, wdn, gf, layer, wolayer, final, tm=512, tf=256):
    n, d = x.shape
    d_ff = wdn.shape[-2]
    once = dict(pipeline_mode=pl.Buffered(1))
    return pl.pallas_call(
        functools.partial(_proj_ffn_kernel, d_ff=d_ff, tf=tf, final=final),
        out_shape=jax.ShapeDtypeStruct((n, d), F32),
        grid=(n // tm,),
        in_specs=[
            pl.BlockSpec((tm, d), lambda i: (i, 0)),
            pl.BlockSpec((tm, d), lambda i: (i, 0)),
            _layer_spec((d, d), wolayer, 1, **once),
            _layer_spec((1, d), layer, 1, **once),
            _layer_spec((d, 2 * d_ff), layer, 1, **once),
            _layer_spec((d_ff, d), layer, 1, **once),
            pl.BlockSpec((1, d), lambda i: (0, 0), **once),
        ],
        out_specs=pl.BlockSpec((tm, d), lambda i: (i, 0)),
        scratch_shapes=[pltpu.VMEM((tm, d_ff), BF16)],
        compiler_params=pltpu.CompilerParams(
            dimension_semantics=("parallel",), vmem_limit_bytes=VMEM_LIMIT_FFN),
        name="out_proj_ffn",
    )(x, mix, wo, g, wup, wdn, gf)


def _even_mixer_kernel(*refs, layer_j, nseq, tb, seq, L, widths, pos0, heads, width, nprev, e16):
    nhead = 2 if e16 is None else 4
    (s_in, h_in, c_in, m3_ref, pm_ref, lb_ref, hnorm_ref, cw_ref, cb_ref, wa_ref,
     ba_ref, wx_ref, bx_ref, la_ref) = refs[nhead:nhead + 14]
    nout = nhead + 14 + nprev
    prev_states = refs[nhead + 14:nout]
    mix_ref, s_stack, h_out, c_out, mixf, xpad = refs[nout:nout + 6]
    z_ref, zg_ref, project_next = _z_source(refs[:nhead], refs[nout + 6:] or None, e16)
    s_out = s_stack.at[layer_j] if nprev else s_stack
    ti = pl.program_id(1)
    R = nseq * tb
    dk = width // heads

    @pl.when(ti == 0)
    def _():
        for i, prev in enumerate(prev_states):
            s_stack[i] = prev[...]
        s_out[...] = s_in[...]
        h_out[...] = h_in[...]
        c_out[...] = c_in[...]

    hl = lb_ref[...]
    e = jnp.exp(hl - jnp.max(hl, axis=0, keepdims=True))
    sm = e / jnp.sum(e, axis=0, keepdims=True)
    lb = jnp.zeros((1, width), F32)
    for i in range(1, layer_j + 1):
        lb = lb + sm[i:i + 1, :]
    log_lb = jnp.log(lb)
    lowers = _lower_masks(widths)

    for tidx in range(R // GROUP):
        r0 = tidx * GROUP
        rows = slice(r0, r0 + GROUP)
        seq_ids = [(r0 + L * j) // tb for j in range(GROUP // L)]
        x = zg_ref[rows, :]
        y = log_lb - x
        logf = _log_sigmoid(x) + jnp.maximum(y, 0.0) + _log1p_exp_neg_abs(y)
        key = 1.0 - jnp.exp(logf)
        qsc = z_ref[rows, 0:width].astype(F32) * (dk ** -0.5)
        e_all = jnp.exp(_level_sums(m3_ref[...], logf))
        ln = [slice(dk * h, dk * (h + 1)) for h in range(heads)]
        vs = [z_ref[rows, width + dk * h:width + dk * (h + 1)] for h in range(heads)]
        states = [[s_out[si, h] for h in range(heads)] for si in seq_ids]

        def put_state(j, i, new, seq_ids=seq_ids):
            s_out[seq_ids[j], i] = new

        outs = _gla_heads_tile([qsc[:, l] for l in ln], [key[:, l] for l in ln], [e_all[:, l] for l in ln], vs,
                               list(range(heads)), [None] * heads, pm_ref, lowers, L, states, put_state)
        for h, (l, o) in enumerate(zip(ln, outs)):
            gate = z_ref[rows, 2 * width + dk * h:2 * width + dk * (h + 1)].astype(F32)
            mixf[rows, l] = _rms(o) * hnorm_ref[:, l] * _silu(gate)

    xoff, yoff = 3 * width, 4 * width
    xpad[:, 8 - (CONV_K - 1):8, :] = c_out[...]
    xpad[:, 8:, :] = z_ref[:, xoff:xoff + width].astype(F32).reshape(nseq, tb, width)
    xc = cb_ref[...].reshape(1, 1, width) + jnp.zeros((nseq, tb, width), F32)
    for j in range(CONV_K):
        lo = 8 - (CONV_K - 1) + j
        xc = xc + xpad[:, lo:lo + tb, :] * cw_ref[j:j + 1, :].reshape(1, 1, width)
    c_out[...] = xpad[:, tb + 8 - (CONV_K - 1):tb + 8, :]
    xc = xc.reshape(R, width)
    xcb = xc.astype(BF16)
    rg = jax.nn.sigmoid(_dot(xcb, wa_ref[...]) + ba_ref[...])
    ig = jax.nn.sigmoid(_dot(xcb, wx_ref[...]) + bx_ref[...])
    log_a = (-LRU_C) * rg * _softplus(-la_ref[...])
    a = jnp.exp(log_a)
    mult = jnp.sqrt(1.0 - a * a)
    row_in_seq = lax.broadcasted_iota(jnp.int32, (R, width), 0) % tb
    if pos0 <= 0 < pos0 + seq:
        mult = jnp.where(row_in_seq + ti * tb + pos0 == 0, 1.0, mult)
    hh = mult * ig * xc
    aa = a
    shift = 1
    while shift < tb:
        m = row_in_seq >= shift
        hh_new = jnp.where(m, aa * pltpu.roll(hh, shift, 0) + hh, hh)
        aa = jnp.where(m, aa * pltpu.roll(aa, shift, 0), aa)
        hh = hh_new
        shift *= 2
    hseq = hh.reshape(nseq, tb, width) + aa.reshape(nseq, tb, width) * h_out[...]
    h_out[...] = hseq[:, tb - 1:tb, :]
    mixf[:, width:2 * width] = hseq.reshape(R, width) * _gelu_tanh(z_ref[:, yoff:yoff + width].astype(F32))
    mix_ref[...] = mixf[...].astype(BF16)
    project_next()


def _state_out(prevs, n_layers, batch, nseq, tail):
    zeros = (0,) * len(tail)
    if prevs:
        return (jax.ShapeDtypeStruct((n_layers, batch) + tail, F32),
                pl.BlockSpec((n_layers, nseq) + tail, lambda b, t: (0, b) + zeros))
    return (jax.ShapeDtypeStruct((batch,) + tail, F32), pl.BlockSpec((nseq,) + tail, lambda b, t: (b,) + zeros))


def _even_mixer(src, s_all, h_all, c_all, s_prevs, layer_j, batch, seq, nseq, tb, L, pos0, params):
    (hgrn_lb, hgrn_norm, conv_w, conv_b, wa, ba, wx, bx, lru_a) = params
    n_layers, _, heads, dk, _ = s_all.shape
    width = heads * dk
    R = nseq * tb
    nt = seq // tb
    z_in, z_specs, z_scratch, n, e16 = _z_inputs(src, R, nt, (batch // nseq) * nt)
    m3, pm, widths = _level_tables(L)
    const = lambda a: pl.BlockSpec(a.shape, lambda b, t: (0,) * a.ndim)
    lay = lambda a: _layer_spec(a.shape[1:], layer_j, 2)
    kern = functools.partial(_even_mixer_kernel, layer_j=layer_j, nseq=nseq, tb=tb, seq=seq, L=L, widths=widths,
                             pos0=pos0, heads=heads, width=width, nprev=len(s_prevs), e16=e16)
    s_shape, s_spec = _state_out(s_prevs, n_layers, batch, nseq, (heads, dk, dk))
    sblk = (None, nseq, heads, dk, dk)
    hblk = (None, nseq, 1, width)
    cblk = (None, nseq, CONV_K - 1, width)
    inputs = z_in + [s_all, h_all, c_all, m3, pm, hgrn_lb, hgrn_norm, conv_w, conv_b, wa, ba, wx, bx, lru_a]
    in_specs = z_specs + [
        pl.BlockSpec(sblk, lambda b, t: (layer_j, b, 0, 0, 0)),
        pl.BlockSpec(hblk, lambda b, t: (layer_j, b, 0, 0)),
        pl.BlockSpec(cblk, lambda b, t: (layer_j, b, 0, 0)),
        const(m3), const(pm), const(hgrn_lb), lay(hgrn_norm), lay(conv_w), lay(conv_b),
        lay(wa), lay(ba), lay(wx), lay(bx), lay(lru_a),
    ]
    inputs += list(s_prevs)
    in_specs += [pl.BlockSpec(sblk[1:], lambda b, t: (b, 0, 0, 0))] * len(s_prevs)
    return pl.pallas_call(
        kern,
        out_shape=(
            jax.ShapeDtypeStruct((n, 2 * width), BF16),
            s_shape,
            jax.ShapeDtypeStruct((batch, 1, width), F32),
            jax.ShapeDtypeStruct((batch, CONV_K - 1, width), F32),
        ),
        grid=(batch // nseq, nt),
        in_specs=in_specs,
        out_specs=(
            pl.BlockSpec((R, 2 * width), lambda b, t: (b * nt + t, 0)),
            s_spec,
            pl.BlockSpec(hblk[1:], lambda b, t: (b, 0, 0)),
            pl.BlockSpec(cblk[1:], lambda b, t: (b, 0, 0)),
        ),
        scratch_shapes=[
            pltpu.VMEM((R, 2 * width), F32),
            pltpu.VMEM((nseq, tb + 8, width), F32),
        ] + z_scratch,
        compiler_params=pltpu.CompilerParams(
            dimension_semantics=("arbitrary", "arbitrary"), vmem_limit_bytes=VMEM_LIMIT),
        name="even_mixer",
    )(*inputs)


def _odd_mixer_kernel(*refs, layer_j, nseq, tb, L, widths, heads, dh, gla_pairs, nprev, e16):
    nhead = 2 if e16 is None else 4
    (c_in, n_in, m_in, g_in, m3_ref, pm_ref, ex3_ref, bias_ref, mnorm_ref, wa2_ref, gba_ref,
     gnorm_ref) = refs[nhead:nhead + 12]
    nout = nhead + 12 + 2 * nprev
    prev_c, prev_g = refs[nhead + 12:nhead + 12 + nprev], refs[nhead + 12 + nprev:nout]
    mix_ref, c_stack, n_out, m_out, g_stack, mixf = refs[nout:nout + 6]
    z_ref, zg_ref, project_next = _z_source(refs[:nhead], refs[nout + 6:] or None, e16)
    c_out = c_stack.at[layer_j] if nprev else c_stack
    g_out = g_stack.at[layer_j] if nprev else g_stack
    ti = pl.program_id(1)
    R = nseq * tb
    cw = heads * dh
    gk = gla_pairs * LANES
    q_off, k_off, v_off, o_off = 0, cw, 2 * cw, 3 * cw
    gq_off = 4 * cw
    gk_off = gq_off + gk
    gv_off = gk_off + gk
    gg_off = gv_off + cw
    gla_dk = gk // heads
    nlev = len(widths)

    @pl.when(ti == 0)
    def _():
        for i in range(nprev):
            c_stack[i] = prev_c[i][...]
            g_stack[i] = prev_g[i][...]
        c_out[...] = c_in[...]
        n_out[...] = n_in[...]
        m_out[...] = m_in[...]
        g_out[...] = g_in[...]

    lane = lax.broadcasted_iota(jnp.int32, (1, LANES), 1)
    is_f = (lane >= heads) & (lane < 2 * heads)
    lo_half = lane < LANES // 2
    hi_half = lane >= LANES // 2
    trow = lax.broadcasted_iota(jnp.int32, (GROUP, heads * GROUP), 0)
    tcol = lax.broadcasted_iota(jnp.int32, (GROUP, heads * GROUP), 1) % GROUP
    seg_tril = (tcol <= trow) & ((trow // L) == (tcol // L))
    gla_masks = [lo_half, hi_half]
    pack = functools.partial(_pack_heads, lo_half=lo_half)

    def head_reduce(x, h, fn, fill):
        col = x[:, LANES * (h // 2):LANES * (h // 2 + 1)]
        return fn(jnp.where(lo_half if h % 2 == 0 else hi_half, col, fill), axis=-1, keepdims=True)
    lowers = _lower_masks(widths)

    for tidx in range(R // GROUP):
        r0 = tidx * GROUP
        rows = slice(r0, r0 + GROUP)
        seq_ids = [(r0 + L * j) // tb for j in range(GROUP // L)]
        small = zg_ref[rows, :]
        g0 = small + bias_ref[...]
        pre = _dot(small.astype(BF16), wa2_ref[...]) + gba_ref[...]
        la = _log_sigmoid(pre) * (1.0 / GLA_TAU)
        sums = _level_sums(m3_ref[...], jnp.concatenate([la, _log_sigmoid(g0)], axis=-1))
        e_all = jnp.exp(sums[:, 0:gk])
        gates = jnp.where(is_f, sums[GROUP * nlev:GROUP * (nlev + 1), gk:gk + LANES], g0)

        kl = [slice(LANES * p, LANES * (p + 1)) for p in range(gla_pairs)]
        vs = [z_ref[rows, gv_off + dh * h:gv_off + dh * (h + 1)] for h in range(heads)]
        qsc = [z_ref[rows, gq_off + LANES * p:gq_off + LANES * (p + 1)].astype(F32) * (gla_dk ** -0.5)
               for p in range(gla_pairs)]
        kk = [z_ref[rows, gk_off + LANES * p:gk_off + LANES * (p + 1)].astype(F32) for p in range(gla_pairs)]
        states = [[g_out[si, p] for p in range(gla_pairs)] for si in seq_ids]

        def put_state(j, i, new, seq_ids=seq_ids):
            g_out[seq_ids[j], i] = new

        outs = _gla_heads_tile(qsc, kk, [e_all[:, l] for l in kl], vs, [h // 2 for h in range(heads)],
                               [gla_masks[h % 2] for h in range(heads)], pm_ref, lowers, L, states, put_state)
        for h, o in enumerate(outs):
            vl = slice(dh * h, dh * (h + 1))
            gg = z_ref[rows, gg_off + dh * h:gg_off + dh * (h + 1)].astype(F32)
            mixf[rows, cw + dh * h:cw + dh * (h + 1)] = _rms(o) * gnorm_ref[:, vl] * _silu(gg)

        m_next = [m_out[si] for si in seq_ids]
        ex = _split_dot(gates, ex3_ref[...])
        bcol = [ex[:, dh * h:dh * (h + 1)] for h in range(heads)]
        icol = [ex[:, cw + dh * h:cw + dh * (h + 1)] for h in range(heads)]
        mprev = [[jnp.broadcast_to(mv[:, heads + h:heads + h + 1], (1, dh)) for mv in m_next] for h in range(heads)]
        inter = []
        for h in range(heads):
            parts = [bcol[h][L * j:L * (j + 1)] + mprev[h][j] for j in range(GROUP // L)]
            inter.append(parts[0] if len(parts) == 1 else jnp.concatenate(parts, axis=0))
        gtt = jnp.transpose(gates)
        urow = jnp.concatenate([gtt[h:h + 1, :] - gtt[heads + h:heads + h + 1, :] for h in range(heads)], axis=1)
        dlog = jnp.where(seg_tril, pack(bcol) + urow, NEG_INF)
        m_t = [jnp.maximum(inter[h], head_reduce(dlog, h, jnp.max, NEG_INF)) for h in range(heads)]
        w_intra = jnp.exp(dlog - pack(m_t))
        w_inter = [jnp.exp(inter[h] - m_t[h]) for h in range(heads)]
        q_t = [z_ref[rows, q_off + dh * h:q_off + dh * (h + 1)].astype(F32) for h in range(heads)]
        k_t = [z_ref[rows, k_off + dh * h:k_off + dh * (h + 1)].astype(F32) * (dh ** -0.5) for h in range(heads)]
        v_t = [z_ref[rows, v_off + dh * h:v_off + dh * (h + 1)] for h in range(heads)]
        qk = _dot_nt(jnp.concatenate(q_t, axis=1).astype(BF16), _block_rows(k_t, dh).astype(BF16)) * w_intra
        num_all = _dot(qk.astype(BF16), _block_rows(v_t, dh).astype(BF16))
        for h in range(heads):
            lanes = slice(dh * h, dh * (h + 1))
            qk_sum = head_reduce(qk, h, jnp.sum, 0.0)
            cells = []
            for j in range(GROUP // L):
                sl = slice(L * j, L * (j + 1))
                si = seq_ids[j]
                q, k, vb = q_t[h][sl], k_t[h][sl], v_t[h][sl].astype(BF16)
                cmat = c_out[si, h]
                nrow = n_out[si, :, lanes]
                wi = w_inter[h][sl]
                num = num_all[sl, lanes] + wi * _dot(q.astype(BF16), cmat.astype(BF16))
                den = qk_sum[sl] + wi * jnp.sum(q * nrow, axis=-1, keepdims=True)
                cells.append(num / jnp.maximum(jnp.abs(den), jnp.exp(-m_t[h][sl])))
                b_last = bcol[h][L * j + L - 1:L * j + L, :]
                m_prev = mprev[h][j]
                wlog = b_last - bcol[h][sl] + icol[h][sl]
                m_new = jnp.maximum(b_last + m_prev, jnp.max(wlog, axis=0, keepdims=True))
                w_s = jnp.exp(wlog - m_new)
                decay = jnp.exp(b_last + m_prev - m_new)
                kw = k * w_s
                c_out[si, h] = decay * cmat + _dot_tn(kw.astype(BF16), vb)
                n_out[si, :, lanes] = decay * nrow + jnp.sum(kw, axis=0, keepdims=True)
                m_next[j] = jnp.where(lane == heads + h, m_new, m_next[j])
            hcell = cells[0] if len(cells) == 1 else jnp.concatenate(cells, axis=0)
            ogate = z_ref[rows, o_off + dh * h:o_off + dh * (h + 1)].astype(F32)
            mixf[rows, lanes] = _rms(hcell) * mnorm_ref[:, lanes] * jax.nn.sigmoid(ogate)
        for j, si in enumerate(seq_ids):
            m_out[si] = m_next[j]
    mix_ref[...] = mixf[...].astype(BF16)
    project_next()


def _odd_mixer(src, c_all, n_all, m_all, g_all, c_prevs, g_prevs, layer_j, batch, seq, nseq, tb, L, params):
    (bias_small, mnorm, wa2_pad, gla_ba, gnorm) = params
    n_layers, _, heads, dh, _ = c_all.shape
    cw = heads * dh
    gla_pairs = g_all.shape[2]
    R = nseq * tb
    nt = seq // tb
    z_in, z_specs, z_scratch, n, e16 = _z_inputs(src, R, nt, (batch // nseq) * nt)
    m3, pm, widths = _level_tables(L)
    const = lambda a: pl.BlockSpec(a.shape, lambda b, t: (0,) * a.ndim)
    lay = lambda a: _layer_spec(a.shape[1:], layer_j, 2)
    kern = functools.partial(_odd_mixer_kernel, layer_j=layer_j, nseq=nseq, tb=tb, L=L, widths=widths, heads=heads,
                             dh=dh, gla_pairs=gla_pairs, nprev=len(c_prevs), e16=e16)
    c_shape, c_spec = _state_out(c_prevs, n_layers, batch, nseq, (heads, dh, dh))
    g_shape, g_spec = _state_out(g_prevs, n_layers, batch, nseq, (gla_pairs, LANES, dh))
    cblk = (None, nseq, heads, dh, dh)
    nblk = (None, nseq, 1, cw)
    mblk = (None, nseq, 1, LANES)
    gblk = (None, nseq, gla_pairs, LANES, dh)
    ex3 = _expand_table(heads, dh)
    inputs = z_in + [c_all, n_all, m_all, g_all, m3, pm, ex3, bias_small, mnorm, wa2_pad, gla_ba, gnorm]
    in_specs = z_specs + [
        pl.BlockSpec(cblk, lambda b, t: (layer_j, b, 0, 0, 0)),
        pl.BlockSpec(nblk, lambda b, t: (layer_j, b, 0, 0)),
        pl.BlockSpec(mblk, lambda b, t: (layer_j, b, 0, 0)),
        pl.BlockSpec(gblk, lambda b, t: (layer_j, b, 0, 0, 0)),
        const(m3), const(pm), const(ex3), lay(bias_small), lay(mnorm), lay(wa2_pad), lay(gla_ba), lay(gnorm),
    ]
    inputs += list(c_prevs) + list(g_prevs)
    in_specs += ([pl.BlockSpec(cblk[1:], lambda b, t: (b, 0, 0, 0))] * len(c_prevs)
                 + [pl.BlockSpec(gblk[1:], lambda b, t: (b, 0, 0, 0))] * len(g_prevs))
    return pl.pallas_call(
        kern,
        out_shape=(
            jax.ShapeDtypeStruct((n, 2 * cw), BF16),
            c_shape,
            jax.ShapeDtypeStruct((batch, 1, cw), F32),
            jax.ShapeDtypeStruct((batch, 1, LANES), F32),
            g_shape,
        ),
        grid=(batch // nseq, nt),
        in_specs=in_specs,
        out_specs=(
            pl.BlockSpec((R, 2 * cw), lambda b, t: (b * nt + t, 0)),
            c_spec,
            pl.BlockSpec(nblk[1:], lambda b, t: (b, 0, 0)),
            pl.BlockSpec(mblk[1:], lambda b, t: (b, 0, 0)),
            g_spec,
        ),
        scratch_shapes=[
            pltpu.VMEM((R, 2 * cw), F32),
        ] + z_scratch,
        compiler_params=pltpu.CompilerParams(
            dimension_semantics=("arbitrary", "arbitrary"), vmem_limit_bytes=VMEM_LIMIT),
        name="odd_mixer",
    )(*inputs)


def _block_diag(w):
    nb, ci, di = w.shape
    eye = jnp.eye(nb, dtype=w.dtype)
    return (eye[:, None, :, None] * w[:, :, None, :]).reshape(nb * ci, nb * di)


def kernel(x_prompt, x_sample, state_hgrn, state_lru_h, state_lru_conv, state_mlstm_C, state_mlstm_n, state_mlstm_m, state_gla, norm_mix, norm_ffn, norm_final, w_in_even, hgrn_lb, hgrn_norm, lru_conv_w, lru_conv_b, lru_wa, lru_ba, lru_wx, lru_bx, lru_a, w_out_even, w_in_odd, mlstm_bi, mlstm_bf, mlstm_norm, gla_wa2, gla_ba, gla_norm, w_out_odd, ffn_w_up, ffn_w_down):
    batch, seq, d_model = x_prompt.shape
    dec_batch, dec_seq, _ = x_sample.shape
    depth = norm_mix.shape[0]
    n_even, _, a_heads, a_dk, _ = state_hgrn.shape
    n_odd, _, c_heads, c_dh, _ = state_mlstm_C.shape
    b_width = state_lru_h.shape[-1]
    c_width = c_heads * c_dh
    d_heads, d_dk, d_dv = state_gla.shape[2:]
    gk = d_heads * d_dk
    d_width = d_heads * d_dv
    d_rank = gla_wa2.shape[1]
    gla_pairs = gk // LANES
    assert a_heads * a_dk == b_width and GROUP % dec_seq == 0 and seq % (PROMPT_TILES * GROUP) == 0

    a_width = a_heads * a_dk
    w_even = jnp.concatenate([w_in_even[:, :, :a_width], w_in_even[:, :, 2 * a_width:],
                              w_in_even[:, :, a_width:2 * a_width]], axis=-1).astype(BF16)
    even_e16 = w_even.shape[-1] - a_width
    main = 4 * c_width
    small_w = 2 * c_heads
    w_odd = jnp.concatenate([
        w_in_odd[:, :, :main],
        w_in_odd[:, :, main + small_w:main + small_w + 2 * gk + 2 * d_width],
        w_in_odd[:, :, main:main + small_w],
        w_in_odd[:, :, main + small_w + 2 * gk + 2 * d_width:],
        jnp.zeros((n_odd, d_model, LANES - small_w - d_rank), F32),
    ], axis=-1).astype(BF16)
    odd_e16 = w_odd.shape[-1] - LANES
    w_out_e = w_out_even.astype(BF16)
    w_out_o = w_out_odd.astype(BF16)
    w_up = ffn_w_up.astype(BF16)
    w_dn = ffn_w_down.astype(BF16)
    wa_dense = jax.vmap(_block_diag)(lru_wa).astype(BF16)
    wx_dense = jax.vmap(_block_diag)(lru_wx).astype(BF16)
    bias_small = jnp.concatenate(
        [mlstm_bi, mlstm_bf, jnp.zeros((n_odd, LANES - small_w), F32)], axis=-1)[:, None, :]
    wa2_pad = jnp.concatenate([
        jnp.zeros((n_odd, small_w, gk), F32), gla_wa2,
        jnp.zeros((n_odd, LANES - small_w - d_rank, gk), F32)], axis=1).astype(BF16)
    rows3 = lambda a: a[:, None, :]
    norm_mix3, norm_ffn3 = rows3(norm_mix), rows3(norm_ffn)
    even_params = (hgrn_lb, rows3(hgrn_norm), lru_conv_w, rows3(lru_conv_b), wa_dense, rows3(lru_ba),
                   wx_dense, rows3(lru_bx), rows3(lru_a))
    odd_params = (bias_small, rows3(mlstm_norm), wa2_pad, rows3(gla_ba), rows3(gla_norm))

    zeros = lambda *s: jnp.zeros(s, F32)
    groups = [
        dict(batch=batch, seq=seq, nseq=1, tb=PROMPT_TILES * GROUP, L=math.gcd(seq, CHUNK), pos0=0, fuse_proj=True,
             x=x_prompt.reshape(batch * seq, d_model),
             hgrn=zeros(n_even, batch, a_heads, a_dk, a_dk),
             lru_h=zeros(n_even, batch, 1, b_width),
             lru_conv=zeros(n_even, batch, CONV_K - 1, b_width),
             mC=zeros(n_odd, batch, c_heads, c_dh, c_dh),
             mn=zeros(n_odd, batch, 1, c_width),
             mm=zeros(n_odd, batch, 1, LANES),
             gla=zeros(n_odd, batch, gla_pairs, LANES, d_dv)),
        dict(batch=dec_batch, seq=dec_seq, nseq=GROUP // dec_seq, tb=dec_seq, L=math.gcd(dec_seq, CHUNK),
             pos0=PAST_LEN, fuse_proj=False,
             x=x_sample.reshape(dec_batch * dec_seq, d_model),
             hgrn=state_hgrn,
             lru_h=state_lru_h[:, :, None, :],
             lru_conv=state_lru_conv,
             mC=state_mlstm_C,
             mn=state_mlstm_n.reshape(n_odd, dec_batch, 1, c_width),
             mm=jnp.pad(state_mlstm_m, ((0, 0), (0, 0), (c_heads, LANES - 2 * c_heads)))[:, :, None, :],
             gla=state_gla.reshape(n_odd, dec_batch, gla_pairs, LANES, d_dv)),
    ]

    results = []
    for grp in groups:
        x = grp["x"]
        bsz = grp["batch"]
        hgrn_done, mc_done, gla_done = [], [], []
        small = dict(lru_h=[], lru_conv=[], mn=[], mm=[])
        prevs = lambda done, j, n_kind: done if (j == n_kind - 1 and j > 0) else []

        def z_src(x, w, l, j, e16, fuse=grp["fuse_proj"]):
            if fuse:
                return ("x", x, norm_mix3, l, w, j, e16)
            return ("z",) + tuple(_norm_matmul(x, norm_mix3, w, l, j, e16))

        for l in range(depth):
            j = l // 2
            if l % 2 == 0:
                src = z_src(x, w_even, l, j, even_e16)
                mix, s, h, cb = _even_mixer(src, grp["hgrn"], grp["lru_h"], grp["lru_conv"],
                                            prevs(hgrn_done, j, n_even), j, bsz, grp["seq"], grp["nseq"], grp["tb"],
                                            grp["L"], grp["pos0"], even_params)
                hgrn_done.append(s)
                small["lru_h"].append(h[:, 0, :])
                small["lru_conv"].append(cb)
                w_out = w_out_e
            else:
                src = z_src(x, w_odd, l, j, odd_e16)
                mix, cm, nn, mm, sg = _odd_mixer(src, grp["mC"], grp["mn"], grp["mm"], grp["gla"],
                                                 prevs(mc_done, j, n_odd), prevs(gla_done, j, n_odd), j, bsz,
                                                 grp["seq"], grp["nseq"], grp["tb"], grp["L"], odd_params)
                mc_done.append(cm)
                gla_done.append(sg)
                small["mn"].append(nn.reshape(bsz, c_heads, c_dh))
                small["mm"].append(mm[:, 0, c_heads:2 * c_heads])
                w_out = w_out_o
            x = _proj_ffn(x, mix, w_out, norm_ffn3, w_up, w_dn, norm_final[None, :], l, j,
                          final=(l == depth - 1))
        st = {k: jnp.stack(v) for k, v in small.items()}
        stacked = lambda done: done[-1] if len(done) > 1 else done[-1][None]
        st["hgrn"] = stacked(hgrn_done)
        st["mC"] = stacked(mc_done)
        st["gla"] = stacked(gla_done).reshape(n_odd, bsz, d_heads, d_dk, d_dv)
        results.append((x.reshape(bsz, grp["seq"], d_model), st))

    (y_p, st_p), (y_s, st_s) = results
    return (y_p, y_s, st_p["hgrn"], st_s["hgrn"], st_p["lru_h"], st_s["lru_h"], st_p["lru_conv"], st_s["lru_conv"],
            st_p["mC"], st_s["mC"], st_p["mn"], st_s["mn"], st_p["mm"], st_s["mm"], st_p["gla"], st_s["gla"])
```
